```python
import jax, jax.numpy as jnp
from jax import lax
import numpy as np

D_MODEL = 1024
BATCH = 2
SEQ = 8192
DEPTH = 2
DEC_BATCH = 32
DEC_SEQ = 16
PAST_LEN = 4096

CHUNK = 64
N_A_LAYERS = DEPTH // 2
N_B_LAYERS = DEPTH - N_A_LAYERS
A_HEADS = 8
A_DQK = D_MODEL // 16
A_DV = D_MODEL // A_HEADS
A_QK_WIDTH = A_HEADS * A_DQK
A_V_WIDTH = A_HEADS * A_DV
A_IN_WIDTH = 2 * A_QK_WIDTH + 2 * A_V_WIDTH + 2 * A_HEADS
GATE_CAP = 15.0
B_HEADS = 16
B_HD = D_MODEL // B_HEADS
Q_BLOCK = 128
KVF_WIDTH = 2 * D_MODEL + B_HEADS
D_FF = ((8 * D_MODEL // 3 + 255) // 256) * 256
EPS = 1e-6

kernel_name = "yoco_mlstm_fox_stream_step"


def rms_norm(x, gain):
    xf = x.astype(jnp.float32)
    y = xf * lax.rsqrt(jnp.mean(xf * xf, axis=-1, keepdims=True) + EPS)
    return (y * gain.astype(jnp.float32)).astype(x.dtype)


def soft_cap(z):
    return GATE_CAP * jnp.tanh(z / GATE_CAP)


def swiglu(xn, w_gu, w_down):
    gate, up = jnp.split(xn @ w_gu, [D_FF], axis=-1)
    return (jax.nn.silu(gate) * up) @ w_down


def mlstm_chunk(carry, blk):
    c, n, m = carry
    q, k, v, li, lf = blk
    L = q.shape[2]
    b = jnp.cumsum(lf, axis=-1)
    causal = jnp.tril(jnp.ones((L, L), dtype=bool))
    dmat = jnp.where(causal, b[..., :, None] - b[..., None, :] + li[..., None, :], -jnp.inf)
    inter = b + m[..., None]
    m_t = jnp.maximum(inter, jnp.max(dmat, axis=-1))
    g = jnp.exp(inter - m_t)
    s = jnp.einsum('bhtd,bhsd->bhts', q, k) * jnp.exp(dmat - m_t[..., None])
    num = g[..., None] * jnp.einsum('bhvd,bhtd->bhtv', c, q) + jnp.einsum('bhts,bhsv->bhtv', s, v)
    den = g * jnp.einsum('bhd,bhtd->bht', n, q) + jnp.sum(s, axis=-1)
    h = num / jnp.maximum(jnp.abs(den), jnp.exp(-m_t))[..., None]
    b_last = b[..., -1]
    w_log = b_last[..., None] - b + li
    m_new = jnp.maximum(b_last + m, jnp.max(w_log, axis=-1))
    g_c = jnp.exp(b_last + m - m_new)
    ws = jnp.exp(w_log - m_new[..., None])
    c_new = g_c[..., None, None] * c + jnp.einsum('bhsv,bhsd->bhvd', ws[..., None] * v, k)
    n_new = g_c[..., None] * n + jnp.einsum('bhs,bhsd->bhd', ws, k)
    return (c_new, n_new, m_new), h


def mlstm_mixer(xn, w_in, b_i, b_f, g_head, w_out, c0, n0, m0):
    f32 = jnp.float32
    bsz, t, _ = xn.shape
    q, k, v, o, ig, fg = jnp.split(xn @ w_in, [A_QK_WIDTH, 2 * A_QK_WIDTH, 2 * A_QK_WIDTH + A_V_WIDTH,
                                              2 * A_QK_WIDTH + 2 * A_V_WIDTH,
                                              2 * A_QK_WIDTH + 2 * A_V_WIDTH + A_HEADS], axis=-1)
    L = min(t, CHUNK)
    nc = t // L

    def to_chunks(z, d):
        return z.astype(f32).reshape(bsz, nc, L, A_HEADS, d).transpose(1, 0, 3, 2, 4)

    def gate_chunks(z):
        return z.reshape(bsz, nc, L, A_HEADS).transpose(1, 0, 3, 2)

    qc = to_chunks(q, A_DQK)
    kc = to_chunks(k, A_DQK) * (A_DQK ** -0.5)
    vc = to_chunks(v, A_DV)
    li = gate_chunks(soft_cap(ig.astype(f32) + b_i.astype(f32)))
    lf = gate_chunks(jax.nn.log_sigmoid(soft_cap(fg.astype(f32) + b_f.astype(f32))))
    (c, n, m), h = lax.scan(mlstm_chunk, (c0.astype(f32), n0.astype(f32), m0.astype(f32)),
                            (qc, kc, vc, li, lf))
    h = h.transpose(1, 0, 3, 2, 4).reshape(bsz, t, A_HEADS, A_DV)
    h = h * lax.rsqrt(jnp.mean(h * h, axis=-1, keepdims=True) + EPS)
    h = h.reshape(bsz, t, A_V_WIDTH) * g_head.astype(f32)
    h = (h * jax.nn.sigmoid(o.astype(f32))).astype(xn.dtype)
    return h @ w_out, c, n, m


def shared_kv(s, g_kv, w_kvf, b_f, g_k):
    bsz, t, _ = s.shape
    k, v, f = jnp.split(rms_norm(s, g_kv) @ w_kvf, [D_MODEL, 2 * D_MODEL], axis=-1)
    k = rms_norm(k.reshape(bsz, t, B_HEADS, B_HD), g_k)
    v = v.reshape(bsz, t, B_HEADS, B_HD)
    logf = jax.nn.log_sigmoid(f.astype(jnp.float32) + b_f.astype(jnp.float32)).astype(s.dtype)
    return k, v, logf


def fox_attend(q, k, v, cum_q, cum_k, q_pos):
    s = jnp.einsum('bqhd,bkhd->bhqk', q, k).astype(jnp.float32) * (B_HD ** -0.5)
    s = s + cum_q.transpose(0, 2, 1)[..., :, None] - cum_k.transpose(0, 2, 1)[..., None, :]
    mask = q_pos[:, None] >= jnp.arange(k.shape[1])[None, :]
    p = jax.nn.softmax(jnp.where(mask, s, -jnp.inf), axis=-1)
    return jnp.einsum('bhqk,bkhd->bqhd', p.astype(v.dtype), v)


def fox_mixer(xn, w_qg, g_q, w_o, k_all, v_all, logf_all):
    bsz, t, _ = xn.shape
    n_past = k_all.shape[1] - t
    q, g = jnp.split(xn @ w_qg, [D_MODEL], axis=-1)
    q = rms_norm(q.reshape(bsz, t, B_HEADS, B_HD), g_q)
    cum = jnp.cumsum(logf_all.astype(jnp.float32), axis=1)
    qb = min(t, Q_BLOCK)
    nb = t // qb

    def block(i):
        start = i * qb
        qi = lax.dynamic_slice_in_dim(q, start, qb, axis=1)
        ci = lax.dynamic_slice_in_dim(cum, n_past + start, qb, axis=1)
        pos = n_past + start + jnp.arange(qb)
        return fox_attend(qi, k_all, v_all, ci, cum, pos)

    o = lax.map(block, jnp.arange(nb))
    o = o.transpose(1, 0, 2, 3, 4).reshape(bsz, t, D_MODEL)
    o = (o.astype(jnp.float32) * jax.nn.sigmoid(g.astype(jnp.float32))).astype(xn.dtype)
    return o @ w_o


def trunk(x, c0, n0, m0, past_k, past_v, past_logf, params):
    (g_mix, g_ffn, w_in_a, b_i_a, b_f_a, g_head_a, w_out_a, g_kv, w_kvf, b_f_b, g_k,
     w_qg_b, g_q_b, w_o_b, w_gu, w_down) = params
    cs, ns, ms = [], [], []
    k_new = v_new = logf_new = None
    k_all = v_all = logf_all = None
    for layer in range(DEPTH):
        h = rms_norm(x, g_mix[layer])
        if layer < N_A_LAYERS:
            y, c, n, m = mlstm_mixer(h, w_in_a[layer], b_i_a[layer], b_f_a[layer], g_head_a[layer],
                                     w_out_a[layer], c0[:, layer], n0[:, layer], m0[:, layer])
            cs.append(c)
            ns.append(n)
            ms.append(m)
        else:
            if layer == N_A_LAYERS:
                k_new, v_new, logf_new = shared_kv(x, g_kv, w_kvf, b_f_b, g_k)
                k_all = jnp.concatenate([past_k.astype(k_new.dtype), k_new], axis=1)
                v_all = jnp.concatenate([past_v.astype(v_new.dtype), v_new], axis=1)
                logf_all = jnp.concatenate([past_logf.astype(logf_new.dtype), logf_new], axis=1)
            j = layer - N_A_LAYERS
            y = fox_mixer(h, w_qg_b[j], g_q_b[j], w_o_b[j], k_all, v_all, logf_all)
        x = x + y
        x = x + swiglu(rms_norm(x, g_ffn[layer]), w_gu[layer], w_down[layer])
    dt = x.dtype
    c_out = jnp.stack(cs, axis=1).astype(dt)
    n_out = jnp.stack(ns, axis=1).astype(dt)
    m_out = jnp.stack(ms, axis=1).astype(dt)
    return x, k_new, v_new, logf_new, c_out, n_out, m_out


def setup_inputs(seed: int = 0) -> dict:
    key = jax.random.key(seed)
    ks = jax.random.split(key, 24)
    f32 = jnp.float32

    def nrm(k, shape, scale=1.0):
        return scale * jax.random.normal(k, shape, f32)

    def w(k, shape, fan_in):
        return jax.random.normal(k, shape, f32) * fan_in ** -0.5

    def gain(k, shape):
        return 1.0 + 0.05 * jax.random.normal(k, shape, f32)

    return dict(
        x_prompt=nrm(ks[0], (BATCH, SEQ, D_MODEL)),
        x_sample=nrm(ks[1], (DEC_BATCH, DEC_SEQ, D_MODEL)),
        cache_k=nrm(ks[2], (DEC_BATCH, PAST_LEN, B_HEADS, B_HD)),
        cache_v=nrm(ks[3], (DEC_BATCH, PAST_LEN, B_HEADS, B_HD)),
        cache_logf=jax.nn.log_sigmoid(jax.random.uniform(ks[4], (DEC_BATCH, PAST_LEN, B_HEADS), f32, 1.0, 4.0)),
        state_c=nrm(ks[5], (DEC_BATCH, N_A_LAYERS, A_HEADS, A_DV, A_DQK), 0.3),
        state_n=nrm(ks[6], (DEC_BATCH, N_A_LAYERS, A_HEADS, A_DQK), 0.3),
        state_m=jax.random.uniform(ks[7], (DEC_BATCH, N_A_LAYERS, A_HEADS), f32, -1.0, 1.0),
        g_mix=gain(ks[8], (DEPTH, D_MODEL)),
        g_ffn=gain(ks[9], (DEPTH, D_MODEL)),
        w_in_a=w(ks[10], (N_A_LAYERS, D_MODEL, A_IN_WIDTH), D_MODEL),
        b_i_a=nrm(ks[11], (N_A_LAYERS, A_HEADS), 0.1),
        b_f_a=jax.random.uniform(ks[12], (N_A_LAYERS, A_HEADS), f32, 3.0, 6.0),
        g_head_a=gain(ks[13], (N_A_LAYERS, A_V_WIDTH)),
        w_out_a=w(ks[14], (N_A_LAYERS, A_V_WIDTH, D_MODEL), A_V_WIDTH),
        g_kv=gain(ks[15], (D_MODEL,)),
        w_kvf=w(ks[16], (D_MODEL, KVF_WIDTH), D_MODEL),
        b_f_b=jax.random.uniform(ks[17], (B_HEADS,), f32, 1.0, 4.0),
        g_k=gain(ks[18], (B_HD,)),
        w_qg_b=w(ks[19], (N_B_LAYERS, D_MODEL, 2 * D_MODEL), D_MODEL),
        g_q_b=gain(ks[20], (N_B_LAYERS, B_HD)),
        w_o_b=w(ks[21], (N_B_LAYERS, D_MODEL, D_MODEL), D_MODEL),
        w_gu=w(ks[22], (DEPTH, D_MODEL, 2 * D_FF), D_MODEL),
        w_down=w(ks[23], (DEPTH, D_FF, D_MODEL), D_FF),
    )


def reference(x_prompt, x_sample, cache_k, cache_v, cache_logf, state_c, state_n, state_m,
              g_mix, g_ffn, w_in_a, b_i_a, b_f_a, g_head_a, w_out_a, g_kv, w_kvf, b_f_b, g_k,
              w_qg_b, g_q_b, w_o_b, w_gu, w_down):
    params = (g_mix, g_ffn, w_in_a, b_i_a, b_f_a, g_head_a, w_out_a, g_kv, w_kvf, b_f_b, g_k,
              w_qg_b, g_q_b, w_o_b, w_gu, w_down)
    bsz = x_prompt.shape[0]
    dt = x_prompt.dtype
    c0 = jnp.zeros((bsz, N_A_LAYERS, A_HEADS, A_DV, A_DQK), jnp.float32)
    n0 = jnp.zeros((bsz, N_A_LAYERS, A_HEADS, A_DQK), jnp.float32)
    m0 = jnp.zeros((bsz, N_A_LAYERS, A_HEADS), jnp.float32)
    empty_kv = jnp.zeros((bsz, 0, B_HEADS, B_HD), dt)
    empty_f = jnp.zeros((bsz, 0, B_HEADS), dt)
    y_prompt, k_p, v_p, logf_p, c_p, n_p, m_p = trunk(x_prompt, c0, n0, m0, empty_kv, empty_kv, empty_f, params)
    y_sample, k_s, v_s, logf_s, c_s, n_s, m_s = trunk(x_sample, state_c, state_n, state_m,
                                                      cache_k, cache_v, cache_logf, params)
    return (y_prompt, y_sample, k_p, v_p, logf_p, c_p, n_p, m_p, k_s, v_s, logf_s, c_s, n_s, m_s)
```

```python
import functools

import jax
import jax.numpy as jnp
from jax import lax
from jax.experimental import pallas as pl
from jax.experimental.pallas import tpu as pltpu

F32 = jnp.float32
BF16 = jnp.bfloat16

EPS = 1e-6
GATE_CAP = 15.0
A_HEADS = 8
A_DQK = 64
A_DV = 128
B_HEADS = 16
B_HD = 64
LANES = 128
HALF = 64
MXU_N = 256
FF_CHUNK = MXU_N
NEG_BIG = -1e30
VMEM_LIMIT = 56 * 1024 * 1024


def _params(*sem):
    return pltpu.CompilerParams(dimension_semantics=sem, vmem_limit_bytes=VMEM_LIMIT)


def _resident(shape):
    nd = len(shape)
    return pl.BlockSpec(shape, lambda *_: (0,) * nd, pipeline_mode=pl.Buffered(1))


def _dot(a, b):
    return jnp.dot(a, b, preferred_element_type=F32)


def _dot_nt(a, b):
    return lax.dot_general(a, b, (((1,), (1,)), ((), ())), preferred_element_type=F32)


def _split2(x):
    hi = x.astype(BF16)
    lo = (x - hi.astype(F32)).astype(BF16)
    return hi, lo


def _split3(x):
    hi = x.astype(BF16)
    r = x - hi.astype(F32)
    mid = r.astype(BF16)
    lo = (r - mid.astype(F32)).astype(BF16)
    return hi, mid, lo


def _dot3(xh, xl, wh, wl):
    return _dot(xh, wh) + _dot(xh, wl) + _dot(xl, wh)


def _log_sigmoid(x):
    return jnp.minimum(x, 0.0) - jnp.log(1.0 + jnp.exp(-jnp.abs(x)))


def _sigmoid(x):
    return 1.0 / (1.0 + jnp.exp(-x))


def _soft_cap(z):
    return GATE_CAP * jnp.tanh(z / GATE_CAP)


def _rms_hat(x):
    return x * lax.rsqrt(jnp.mean(x * x, axis=-1, keepdims=True) + EPS)


def _pair_select(lo_mask, a, b):
    return jnp.where(lo_mask, a, b)


def _head_norm_group(xc, gc, lo_mask):
    sq = xc * xc
    s_lo = jnp.sum(jnp.where(lo_mask, sq, 0.0), axis=-1, keepdims=True)
    s_hi = jnp.sum(jnp.where(lo_mask, 0.0, sq), axis=-1, keepdims=True)
    ms = jnp.where(lo_mask, s_lo, s_hi) * (1.0 / B_HD)
    return xc * lax.rsqrt(ms + EPS) * gc


def _in_proj_kernel(x_ref, g_ref, w_ref, wgh_ref, wgl_ref, q_ref, k_ref, v_ref, o_ref, gates_ref):
    xn = _rms_hat(x_ref[...]) * g_ref[...]
    xh, xl = _split2(xn)
    qk_w = A_HEADS * A_DQK
    v_w = A_HEADS * A_DV
    q_ref[...] = _dot(xh, w_ref[:, 0:qk_w]).astype(BF16)
    k_ref[...] = (_dot(xh, w_ref[:, qk_w:2 * qk_w]) * (A_DQK ** -0.5)).astype(BF16)
    v_ref[...] = _dot(xh, w_ref[:, 2 * qk_w:2 * qk_w + v_w]).astype(BF16)
    o_ref[...] = _dot(xh, w_ref[:, 2 * qk_w + v_w:2 * qk_w + 2 * v_w])
    gz = _dot3(xh, xl, wgh_ref[...], wgl_ref[...])
    gates_ref[...] = gz[:, 0:2 * A_HEADS]


def _in_proj(x2d, g, w_main, wg_hi, wg_lo, tm):
    m, d = x2d.shape
    qk_w = A_HEADS * A_DQK
    v_w = A_HEADS * A_DV
    row = lambda i: (i, 0)
    return pl.pallas_call(
        _in_proj_kernel,
        grid=(m // tm,),
        in_specs=[pl.BlockSpec((tm, d), row), _resident(g.shape), _resident(w_main.shape),
                  _resident(wg_hi.shape), _resident(wg_lo.shape)],
        out_specs=[pl.BlockSpec((tm, qk_w), row), pl.BlockSpec((tm, qk_w), row),
                   pl.BlockSpec((tm, v_w), row), pl.BlockSpec((tm, v_w), row),
                   pl.BlockSpec((tm, 2 * A_HEADS), row)],
        out_shape=[jax.ShapeDtypeStruct((m, qk_w), BF16), jax.ShapeDtypeStruct((m, qk_w), BF16),
                   jax.ShapeDtypeStruct((m, v_w), BF16), jax.ShapeDtypeStruct((m, v_w), F32),
                   jax.ShapeDtypeStruct((m, 2 * A_HEADS), F32)],
        compiler_params=_params("parallel"),
        name="in_proj",
    )(x2d, g, w_main, wg_hi, wg_lo)


def _mlstm_kernel(q_ref, k_ref, v_ref, o_ref, gc_ref, gr_ref, bc_ref, br_ref, gh_ref,
                  c0_ref, n0_ref, m0_ref,
                  h_ref, c_out, n_out, m_out, ct_s, n_s, m_s, *, chunk):
    j = pl.program_id(1)
    L = chunk

    @pl.when(j == 0)
    def _():
        ct_s[...] = c0_ref[0]
        n_s[...] = n0_ref[0]
        m_s[...] = m0_ref[0]

    cap_c = _soft_cap(gc_ref[0] + bc_ref[...])
    cap_r = _soft_cap(gr_ref[0] + br_ref[...])
    li_c = cap_c[:, 0:A_HEADS]
    lf_c = _log_sigmoid(cap_c[:, A_HEADS:2 * A_HEADS])
    li_r = cap_r[0:A_HEADS, :]
    lf_r = _log_sigmoid(cap_r[A_HEADS:2 * A_HEADS, :])

    t_idx = lax.broadcasted_iota(jnp.int32, (L, L), 0)
    s_idx = lax.broadcasted_iota(jnp.int32, (L, L), 1)
    causal = s_idx <= t_idx
    lane = lax.broadcasted_iota(jnp.int32, (1, LANES), 1)
    lo_lane = lane < HALF
    lo_row = lax.broadcasted_iota(jnp.int32, (LANES, 1), 0) < HALF
    eye = (lax.broadcasted_iota(jnp.int32, (LANES, LANES), 0)
           == lax.broadcasted_iota(jnp.int32, (LANES, LANES), 1)).astype(BF16)

    for p in range(A_HEADS // 2):
        q2 = q_ref[0, :, p * LANES:(p + 1) * LANES]
        k2 = k_ref[0, :, p * LANES:(p + 1) * LANES]
        q2f = q2.astype(F32)
        k2f = k2.astype(F32)
        kt2 = _dot_nt(eye, k2).astype(BF16)
        ct2 = ct_s[p]
        ct2b = ct2.astype(BF16)
        n2 = n_s[p]
        qn = q2f * n2
        upd = []
        gcs = []
        wss = []
        for hh in range(2):
            h = 2 * p + hh
            half = lo_lane if hh == 0 else jnp.logical_not(lo_lane)
            li_col = li_c[:, h:h + 1]
            lf_col = lf_c[:, h:h + 1]
            li_row = li_r[h:h + 1, :]
            lf_row = lf_r[h:h + 1, :]
            m_prev = m_s[h][:, 0:1]

            b_col = jnp.sum(jnp.where(causal, lf_row, 0.0), axis=1, keepdims=True)
            b_row = jnp.sum(jnp.where(t_idx <= s_idx, lf_col, 0.0), axis=0, keepdims=True)
            dmat = jnp.where(causal, b_col - b_row + li_row, NEG_BIG)
            inter = b_col + m_prev
            m_t = jnp.maximum(inter, jnp.max(dmat, axis=1, keepdims=True))
            g = jnp.exp(inter - m_t)
            dec = jnp.where(causal, jnp.exp(dmat - m_t), 0.0)

            qm = jnp.where(half, q2, jnp.zeros_like(q2))
            s = _dot_nt(qm, k2) * dec
            v_h = v_ref[0, :, h * A_DV:(h + 1) * A_DV]
            num = g * _dot(qm, ct2b) + _dot(s.astype(BF16), v_h)
            den = (g * jnp.sum(jnp.where(half, qn, 0.0), axis=1, keepdims=True)
                   + jnp.sum(s, axis=1, keepdims=True))
            hv = num / jnp.maximum(jnp.abs(den), jnp.exp(-m_t))
            hv = hv * lax.rsqrt(jnp.mean(hv * hv, axis=-1, keepdims=True) + EPS)
            hv = hv * gh_ref[:, h * A_DV:(h + 1) * A_DV]
            hv = hv * _sigmoid(o_ref[0, :, h * A_DV:(h + 1) * A_DV])
            h_ref[0, :, h * A_DV:(h + 1) * A_DV] = hv.astype(h_ref.dtype)

            b_last = jnp.sum(lf_row, axis=1, keepdims=True)
            w_col = b_last - b_col + li_col
            m_new = jnp.maximum(b_last + m_prev, jnp.max(w_col, axis=0, keepdims=True))
            g_c = jnp.exp(b_last + m_prev - m_new)
            ws_col = jnp.exp(w_col - m_new)
            wsv = (ws_col * v_h.astype(F32)).astype(BF16)
            upd.append(_dot(kt2, wsv))
            gcs.append(g_c)
            wss.append(ws_col)
            m_s[h] = jnp.broadcast_to(m_new, (1, LANES))

        ct_s[p] = jnp.where(lo_row, gcs[0], gcs[1]) * ct2 + jnp.where(lo_row, upd[0], upd[1])
        ws2 = jnp.where(lo_lane, wss[0], wss[1])
        n_s[p] = (jnp.where(lo_lane, gcs[0], gcs[1]) * n2
                  + jnp.sum(ws2 * k2f, axis=0, keepdims=True))

    @pl.when(j == pl.num_programs(1) - 1)
    def _():
        c_out[0] = ct_s[...]
        n_out[0] = n_s[...]
        m_out[0] = m_s[...]


def _mlstm(q, k, v, o, gates, gates_t, b_col, b_row, g_head, ct0, n0, m0, chunk):
    bsz, t, _ = q.shape
    nc = t // chunk
    np_ = A_HEADS // 2
    tok = lambda b, j: (b, j, 0)
    st4 = lambda b, j: (b, 0, 0, 0)
    return pl.pallas_call(
        functools.partial(_mlstm_kernel, chunk=chunk),
        grid=(bsz, nc),
        in_specs=[pl.BlockSpec((1, chunk, q.shape[2]), tok), pl.BlockSpec((1, chunk, k.shape[2]), tok),
                  pl.BlockSpec((1, chunk, v.shape[2]), tok), pl.BlockSpec((1, chunk, o.shape[2]), tok),
                  pl.BlockSpec((1, chunk, 2 * A_HEADS), tok),
                  pl.BlockSpec((1, 2 * A_HEADS, chunk), lambda b, j: (b, 0, j)),
                  _resident(b_col.shape), _resident(b_row.shape), _resident(g_head.shape),
                  pl.BlockSpec((1, np_, LANES, A_DV), st4), pl.BlockSpec((1, np_, 1, LANES), st4),
                  pl.BlockSpec((1, A_HEADS, 1, LANES), st4)],
        out_specs=[pl.BlockSpec((1, chunk, v.shape[2]), tok),
                   pl.BlockSpec((1, np_, LANES, A_DV), st4), pl.BlockSpec((1, np_, 1, LANES), st4),
                   pl.BlockSpec((1, A_HEADS, 1, LANES), st4)],
        out_shape=[jax.ShapeDtypeStruct((bsz, t, v.shape[2]), BF16),
                   jax.ShapeDtypeStruct((bsz, np_, LANES, A_DV), F32),
                   jax.ShapeDtypeStruct((bsz, np_, 1, LANES), F32),
                   jax.ShapeDtypeStruct((bsz, A_HEADS, 1, LANES), F32)],
        scratch_shapes=[pltpu.VMEM((np_, LANES, A_DV), F32), pltpu.VMEM((np_, 1, LANES), F32),
                        pltpu.VMEM((A_HEADS, 1, LANES), F32)],
        compiler_params=_params("parallel", "arbitrary"),
        name="mlstm",
    )(q, k, v, o, gates, gates_t, b_col, b_row, g_head, ct0, n0, m0)


def _mix_ffn_kernel(a_ref, res_ref, wo_ref, g_ref, wg_ref, wu_ref, wd_ref, out_ref, xn_s, acc_s):
    x1 = res_ref[...] + _dot(a_ref[...], wo_ref[...])
    xn_s[...] = (_rms_hat(x1) * g_ref[...]).astype(BF16)
    acc_s[...] = x1

    def body(c, carry):
        xn = xn_s[...]
        gate = _dot(xn, wg_ref[c])
        up = _dot(xn, wu_ref[c])
        act = (gate * _sigmoid(gate) * up).astype(BF16)
        acc_s[...] += _dot(act, wd_ref[c])
        return carry

    lax.fori_loop(0, wg_ref.shape[0], body, 0)
    out_ref[...] = acc_s[...]


def _mix_ffn(a, res, w_o, g, wg, wu, wd, tm):
    m, d = res.shape
    row = lambda i: (i, 0)
    return pl.pallas_call(
        _mix_ffn_kernel,
        grid=(m // tm,),
        in_specs=[pl.BlockSpec((tm, a.shape[1]), row), pl.BlockSpec((tm, d), row),
                  _resident(w_o.shape), _resident(g.shape), _resident(wg.shape),
                  _resident(wu.shape), _resident(wd.shape)],
        out_specs=pl.BlockSpec((tm, d), row),
        out_shape=jax.ShapeDtypeStruct((m, d), F32),
        scratch_shapes=[pltpu.VMEM((tm, d), BF16), pltpu.VMEM((tm, d), F32)],
        compiler_params=_params("parallel"),
        name="mix_ffn",
    )(a, res, w_o, g, wg, wu, wd)


def _fox_proj_kernel(x_ref, gkv_ref, gmix_ref, wkv_ref, wfh_ref, wfl_ref, wqg_ref, bf_ref, gk_ref, gq_ref,
                     k_ref, v_ref, logf_ref, kb_ref, vb_ref, q_ref, gate_ref):
    d = x_ref.shape[1]
    xhat = _rms_hat(x_ref[...])
    sh, sl = _split2(xhat * gkv_ref[...])
    hq = (xhat * gmix_ref[...]).astype(BF16)
    lo_lane = lax.broadcasted_iota(jnp.int32, (1, LANES), 1) < HALF

    f = _dot3(sh, sl, wfh_ref[...], wfl_ref[...])[:, 0:B_HEADS]
    logf_ref[...] = _log_sigmoid(f + bf_ref[...])

    for c in range(d // MXU_N):
        kraw = _dot(sh, wkv_ref[:, c * MXU_N:(c + 1) * MXU_N])
        qraw = _dot(hq, wqg_ref[:, c * MXU_N:(c + 1) * MXU_N])
        for u in range(MXU_N // LANES):
            us = slice(u * LANES, (u + 1) * LANES)
            cs = slice(c * MXU_N + u * LANES, c * MXU_N + (u + 1) * LANES)
            kn = _head_norm_group(kraw[:, us], gk_ref[:, cs], lo_lane)
            k_ref[:, cs] = kn
            kb_ref[:, cs] = kn.astype(BF16)
            qn = _head_norm_group(qraw[:, us], gq_ref[:, cs], lo_lane)
            q_ref[:, cs] = (qn * (B_HD ** -0.5)).astype(BF16)
    v = _dot(sh, wkv_ref[:, d:2 * d])
    v_ref[...] = v
    vb_ref[...] = v.astype(BF16)
    gate_ref[...] = _dot(hq, wqg_ref[:, d:2 * d])


def _fox_proj(x2d, g_kv, g_mix, w_kv, wf_hi, wf_lo, w_qg, b_f, gk_t, gq_t, tm):
    m, d = x2d.shape
    row = lambda i: (i, 0)
    big = pl.BlockSpec((tm, d), row)
    return pl.pallas_call(
        _fox_proj_kernel,
        grid=(m // tm,),
        in_specs=[big] + [_resident(a.shape) for a in (g_kv, g_mix, w_kv, wf_hi, wf_lo, w_qg, b_f, gk_t, gq_t)],
        out_specs=[big, big, pl.BlockSpec((tm, B_HEADS), row), big, big, big, big],
        out_shape=[jax.ShapeDtypeStruct((m, d), F32), jax.ShapeDtypeStruct((m, d), F32),
                   jax.ShapeDtypeStruct((m, B_HEADS), F32),
                   jax.ShapeDtypeStruct((m, d), BF16), jax.ShapeDtypeStruct((m, d), BF16),
                   jax.ShapeDtypeStruct((m, d), BF16), jax.ShapeDtypeStruct((m, d), F32)],
        compiler_params=_params("parallel"),
        name="fox_proj",
    )(x2d, g_kv, g_mix, w_kv, wf_hi, wf_lo, w_qg, b_f, gk_t, gq_t)


def _cumsum_kernel(x_ref, init_ref, out_ref, carry_s):
    @pl.when(pl.program_id(1) == 0)
    def _():
        carry_s[...] = init_ref[0]

    tb = x_ref.shape[1]
    tri = (lax.broadcasted_iota(jnp.int32, (tb, tb), 1)
           <= lax.broadcasted_iota(jnp.int32, (tb, tb), 0)).astype(BF16)
    hi, mid, lo = _split3(x_ref[0])
    cum = _dot(tri, hi) + _dot(tri, mid) + _dot(tri, lo) + carry_s[...]
    out_ref[0] = cum
    carry_s[...] = cum[tb - 1:tb, :]


def _cumsum(x, init, tb):
    bsz, t, h = x.shape
    return pl.pallas_call(
        _cumsum_kernel,
        grid=(bsz, t // tb),
        in_specs=[pl.BlockSpec((1, tb, h), lambda b, j: (b, j, 0)),
                  pl.BlockSpec((1, 1, h), lambda b, j: (b, 0, 0))],
        out_specs=pl.BlockSpec((1, tb, h), lambda b, j: (b, j, 0)),
        out_shape=jax.ShapeDtypeStruct((bsz, t, h), F32),
        scratch_shapes=[pltpu.VMEM((1, h), F32)],
        compiler_params=_params("parallel", "arbitrary"),
        name="cumsum",
    )(x, init)


def _attn_pair_step(q2, k2, v2, ck0, ck1, mask, m_s, l_s, acc_s, p, lo_lane):
    for hh, ck in ((0, ck0), (1, ck1)):
        h = 2 * p + hh
        half = lo_lane if hh == 0 else jnp.logical_not(lo_lane)
        qm = jnp.where(half, q2, jnp.zeros_like(q2))
        s = _dot_nt(qm, k2) - ck
        if mask is not None:
            s = jnp.where(mask, s, NEG_BIG)
        m_old = m_s[h]
        m_new = jnp.maximum(m_old, jnp.max(s, axis=1, keepdims=True))
        alpha = jnp.exp(m_old - m_new)
        pr = jnp.exp(s - m_new)
        if mask is not None:
            pr = jnp.where(mask, pr, 0.0)
        l_s[h] = alpha * l_s[h] + jnp.sum(pr, axis=1, keepdims=True)
        acc_s[h] = alpha * acc_s[h] + _dot(pr.astype(BF16), v2)
        m_s[h] = m_new


def _attn_init(m_s, l_s, acc_s):
    m_s[...] = jnp.full(m_s.shape, NEG_BIG, F32)
    l_s[...] = jnp.zeros(l_s.shape, F32)
    acc_s[...] = jnp.zeros(acc_s.shape, F32)


def _attn_finish(out_ref, gate_ref, m_s, l_s, acc_s, lo_lane):
    for p in range(B_HEADS // 2):
        o2 = jnp.where(lo_lane, acc_s[2 * p] / l_s[2 * p], acc_s[2 * p + 1] / l_s[2 * p + 1])
        gate = gate_ref[0, :, p * LANES:(p + 1) * LANES]
        out_ref[0, :, p * LANES:(p + 1) * LANES] = (o2 * _sigmoid(gate)).astype(out_ref.dtype)


def _prompt_attn_kernel(q_ref, k_ref, v_ref, ct_ref, gate_ref, out_ref, m_s, l_s, acc_s, *, tq, tk):
    i = pl.program_id(1)
    j = pl.program_id(2)
    lo_lane = lax.broadcasted_iota(jnp.int32, (1, LANES), 1) < HALF

    @pl.when(j == 0)
    def _():
        _attn_init(m_s, l_s, acc_s)

    def run(masked):
        mask = None
        if masked:
            t_pos = i * tq + lax.broadcasted_iota(jnp.int32, (tq, tk), 0)
            s_pos = j * tk + lax.broadcasted_iota(jnp.int32, (tq, tk), 1)
            mask = s_pos <= t_pos
        for p in range(B_HEADS // 2):
            cs = slice(p * LANES, (p + 1) * LANES)
            _attn_pair_step(q_ref[0, :, cs], k_ref[0, :, cs], v_ref[0, :, cs],
                            ct_ref[0, 2 * p:2 * p + 1, :], ct_ref[0, 2 * p + 1:2 * p + 2, :],
                            mask, m_s, l_s, acc_s, p, lo_lane)

    last = ((i + 1) * tq - 1) // tk
    first_diag = (i * tq) // tk

    @pl.when(j < first_diag)
    def _():
        run(False)

    @pl.when(jnp.logical_and(j >= first_diag, j <= last))
    def _():
        run(True)

    @pl.when(j == last)
    def _():
        _attn_finish(out_ref, gate_ref, m_s, l_s, acc_s, lo_lane)


def _prompt_attn(q, kb, vb, cum_t, gate, tq, tk):
    bsz, t, d = q.shape
    nq, nk = t // tq, t // tk
    last = lambda i: ((i + 1) * tq - 1) // tk
    qmap = lambda b, i, j: (b, i, 0)
    kvmap = lambda b, i, j: (b, jnp.minimum(j, last(i)), 0)
    return pl.pallas_call(
        functools.partial(_prompt_attn_kernel, tq=tq, tk=tk),
        grid=(bsz, nq, nk),
        in_specs=[pl.BlockSpec((1, tq, d), qmap), pl.BlockSpec((1, tk, d), kvmap),
                  pl.BlockSpec((1, tk, d), kvmap),
                  pl.BlockSpec((1, B_HEADS, tk), lambda b, i, j: (b, 0, jnp.minimum(j, last(i)))),
                  pl.BlockSpec((1, tq, d), qmap)],
        out_specs=pl.BlockSpec((1, tq, d), qmap),
        out_shape=jax.ShapeDtypeStruct((bsz, t, d), BF16),
        scratch_shapes=[pltpu.VMEM((B_HEADS, tq, 1), F32), pltpu.VMEM((B_HEADS, tq, 1), F32),
                        pltpu.VMEM((B_HEADS, tq, LANES), F32)],
        compiler_params=_params("parallel", "parallel", "arbitrary"),
        name="prompt_attn",
    )(q, kb, vb, cum_t, gate)


def _sample_attn_kernel(q_ref, ck_ref, cv_ref, cct_ref, nk_ref, nv_ref, nct_ref, gate_ref, out_ref,
                        m_s, l_s, acc_s):
    j = pl.program_id(1)
    tq = q_ref.shape[1]
    lo_lane = lax.broadcasted_iota(jnp.int32, (1, LANES), 1) < HALF

    @pl.when(j == 0)
    def _():
        _attn_init(m_s, l_s, acc_s)

    for p in range(B_HEADS // 2):
        cs = slice(p * LANES, (p + 1) * LANES)
        _attn_pair_step(q_ref[0, :, cs], ck_ref[0, :, cs].astype(BF16), cv_ref[0, :, cs].astype(BF16),
                        cct_ref[0, 2 * p:2 * p + 1, :], cct_ref[0, 2 * p + 1:2 * p + 2, :],
                        None, m_s, l_s, acc_s, p, lo_lane)

    @pl.when(j == pl.num_programs(1) - 1)
    def _():
        mask = (lax.broadcasted_iota(jnp.int32, (tq, tq), 1)
                <= lax.broadcasted_iota(jnp.int32, (tq, tq), 0))
        for p in range(B_HEADS // 2):
            cs = slice(p * LANES, (p + 1) * LANES)
            _attn_pair_step(q_ref[0, :, cs], nk_ref[0, :, cs], nv_ref[0, :, cs],
                            nct_ref[0, 2 * p:2 * p + 1, :], nct_ref[0, 2 * p + 1:2 * p + 2, :],
                            mask, m_s, l_s, acc_s, p, lo_lane)
        _attn_finish(out_ref, gate_ref, m_s, l_s, acc_s, lo_lane)


def _sample_attn(q, cache_k, cache_v, cache_cum_t, kb, vb, new_cum_t, gate, tk):
    bsz, tq, d = q.shape
    past = cache_k.shape[1]
    one = lambda b, j: (b, 0, 0)
    blk = lambda b, j: (b, j, 0)
    return pl.pallas_call(
        _sample_attn_kernel,
        grid=(bsz, past // tk),
        in_specs=[pl.BlockSpec((1, tq, d), one), pl.BlockSpec((1, tk, d), blk), pl.BlockSpec((1, tk, d), blk),
                  pl.BlockSpec((1, B_HEADS, tk), lambda b, j: (b, 0, j)),
                  pl.BlockSpec((1, tq, d), one), pl.BlockSpec((1, tq, d), one),
                  pl.BlockSpec((1, B_HEADS, tq), one), pl.BlockSpec((1, tq, d), one)],
        out_specs=pl.BlockSpec((1, tq, d), one),
        out_shape=jax.ShapeDtypeStruct((bsz, tq, d), BF16),
        scratch_shapes=[pltpu.VMEM((B_HEADS, tq, 1), F32), pltpu.VMEM((B_HEADS, tq, 1), F32),
                        pltpu.VMEM((B_HEADS, tq, LANES), F32)],
        compiler_params=_params("parallel", "arbitrary"),
        name="sample_attn",
    )(q, cache_k, cache_v, cache_cum_t, kb, vb, new_cum_t, gate)


def _pad_cols(w, n):
    return jnp.pad(w, ((0, 0), (0, n - w.shape[1])))


def _prep_weights(g_mix, g_ffn, w_in_a, b_i_a, b_f_a, g_head_a, w_out_a, g_kv, w_kvf, b_f_b, g_k,
                  w_qg_b, g_q_b, w_o_b, w_gu, w_down):
    d = g_mix.shape[1]
    main_w = 2 * A_HEADS * A_DQK + 2 * A_HEADS * A_DV
    w_in = w_in_a[0]
    wg = _pad_cols(w_in[:, main_w:], LANES)
    wg_hi, wg_lo = _split2(wg)
    wf = _pad_cols(w_kvf[:, 2 * d:], LANES)
    wf_hi, wf_lo = _split2(wf)
    gates_b = jnp.concatenate([b_i_a[0], b_f_a[0]])
    d_ff = w_down.shape[1]
    nch = d_ff // FF_CHUNK

    def ffn(layer):
        w = w_gu[layer].astype(BF16)
        wgate = w[:, :d_ff].reshape(d, nch, FF_CHUNK).transpose(1, 0, 2)
        wup = w[:, d_ff:].reshape(d, nch, FF_CHUNK).transpose(1, 0, 2)
        wdn = w_down[layer].astype(BF16).reshape(nch, FF_CHUNK, d)
        return wgate, wup, wdn

    return dict(
        g_mix0=g_mix[0][None], g_mix1=g_mix[1][None], g_ffn0=g_ffn[0][None], g_ffn1=g_ffn[1][None],
        w_in=w_in[:, :main_w].astype(BF16), wg_hi=wg_hi, wg_lo=wg_lo,
        gb_col=gates_b[None, :], gb_row=gates_b[:, None], g_head=g_head_a[0][None],
        w_out=w_out_a[0].astype(BF16),
        g_kv=g_kv[None], w_kv=w_kvf[:, :2 * d].astype(BF16), wf_hi=wf_hi, wf_lo=wf_lo, b_f=b_f_b[None],
        gk_t=jnp.tile(g_k, B_HEADS)[None], gq_t=jnp.tile(g_q_b[0], B_HEADS)[None],
        w_qg=w_qg_b[0].astype(BF16), w_o=w_o_b[0].astype(BF16),
        ffn0=ffn(0), ffn1=ffn(1),
    )


def _state_to_pairs(c, n, m):
    bsz = c.shape[0]
    np_ = A_HEADS // 2
    ct = c.reshape(bsz, np_, 2, A_DV, A_DQK).transpose(0, 1, 2, 4, 3).reshape(bsz, np_, 2 * A_DQK, A_DV)
    n2 = n.reshape(bsz, np_, 1, 2 * A_DQK)
    m2 = jnp.broadcast_to(m[:, :, None, None], (bsz, A_HEADS, 1, LANES))
    return ct.astype(F32), n2.astype(F32), m2.astype(F32)


def _state_from_pairs(ct, n2, m2):
    bsz = ct.shape[0]
    np_ = A_HEADS // 2
    c = ct.reshape(bsz, np_, 2, A_DQK, A_DV).transpose(0, 1, 2, 4, 3).reshape(bsz, 1, A_HEADS, A_DV, A_DQK)
    n = n2.reshape(bsz, 1, A_HEADS, A_DQK)
    m = m2[:, :, 0, 0].reshape(bsz, 1, A_HEADS)
    return c, n, m


def _layer_a(x, c0, n0, m0, w, chunk, tm):
    bsz, t, d = x.shape
    x2d = x.reshape(bsz * t, d)
    q, k, v, o, gates = _in_proj(x2d, w["g_mix0"], w["w_in"], w["wg_hi"], w["wg_lo"], tm)
    r3 = lambda z: z.reshape(bsz, t, z.shape[1])
    gates3 = r3(gates)
    ct0, n20, m20 = _state_to_pairs(c0, n0, m0)
    hn, ct, n2, m2 = _mlstm(r3(q), r3(k), r3(v), r3(o), gates3, gates3.transpose(0, 2, 1),
                            w["gb_col"], w["gb_row"], w["g_head"], ct0, n20, m20, chunk)
    x2 = _mix_ffn(hn.reshape(bsz * t, -1), x2d, w["w_out"], w["g_ffn0"], *w["ffn0"], tm)
    return x2, _state_from_pairs(ct, n2, m2)


def _fox_inputs(x2, w, tm):
    return _fox_proj(x2, w["g_kv"], w["g_mix1"], w["w_kv"], w["wf_hi"], w["wf_lo"], w["w_qg"],
                     w["b_f"], w["gk_t"], w["gq_t"], tm)


def _prompt_trunk(x, w):
    bsz, t, d = x.shape
    tm = 512
    zeros = lambda *s: jnp.zeros(s, F32)
    x2, (c, n, m) = _layer_a(x, zeros(bsz, A_HEADS, A_DV, A_DQK), zeros(bsz, A_HEADS, A_DQK),
                             zeros(bsz, A_HEADS), w, min(t, 128), tm)
    k, v, logf, kb, vb, q, gate = _fox_inputs(x2, w, tm)
    r3 = lambda z: z.reshape(bsz, t, z.shape[1])
    logf3 = r3(logf)
    cum = _cumsum(logf3, zeros(bsz, 1, B_HEADS), min(t, 512))
    attn = _prompt_attn(r3(q), r3(kb), r3(vb), cum.transpose(0, 2, 1), r3(gate), min(t, 512), min(t, 512))
    y = _mix_ffn(attn.reshape(bsz * t, d), x2, w["w_o"], w["g_ffn1"], *w["ffn1"], tm)
    return (y.reshape(bsz, t, d), k.reshape(bsz, t, B_HEADS, B_HD), v.reshape(bsz, t, B_HEADS, B_HD),
            logf3, c, n, m)


def _sample_trunk(x, c0, n0, m0, cache_k, cache_v, cache_logf, w):
    bsz, t, d = x.shape
    past = cache_k.shape[1]
    tm = min(bsz * t, 512)
    x2, (c, n, m) = _layer_a(x, c0[:, 0], n0[:, 0], m0[:, 0], w, t, tm)
    k, v, logf, kb, vb, q, gate = _fox_inputs(x2, w, tm)
    r3 = lambda z: z.reshape(bsz, t, z.shape[1])
    logf3 = r3(logf)
    cache_cum = _cumsum(cache_logf, jnp.zeros((bsz, 1, B_HEADS), F32), min(past, 512))
    new_cum = _cumsum(logf3, cache_cum[:, past - 1:past, :], t)
    attn = _sample_attn(r3(q), cache_k.reshape(bsz, past, d), cache_v.reshape(bsz, past, d),
                        cache_cum.transpose(0, 2, 1), r3(kb), r3(vb), new_cum.transpose(0, 2, 1),
                        r3(gate), min(past, 512))
    y = _mix_ffn(attn.reshape(bsz * t, d), x2, w["w_o"], w["g_ffn1"], *w["ffn1"], tm)
    return (y.reshape(bsz, t, d), k.reshape(bsz, t, B_HEADS, B_HD), v.reshape(bsz, t, B_HEADS, B_HD),
            logf3, c, n, m)


def kernel(x_prompt, x_sample, cache_k, cache_v, cache_logf, state_c, state_n, state_m, g_mix, g_ffn, w_in_a, b_i_a, b_f_a, g_head_a, w_out_a, g_kv, w_kvf, b_f_b, g_k, w_qg_b, g_q_b, w_o_b, w_gu, w_down):
    w = _prep_weights(g_mix, g_ffn, w_in_a, b_i_a, b_f_a, g_head_a, w_out_a, g_kv, w_kvf, b_f_b, g_k,
                      w_qg_b, g_q_b, w_o_b, w_gu, w_down)
    y_p, k_p, v_p, f_p, c_p, n_p, m_p = _prompt_trunk(x_prompt, w)
    y_s, k_s, v_s, f_s, c_s, n_s, m_s = _sample_trunk(x_sample, state_c, state_n, state_m,
                                                      cache_k, cache_v, cache_logf, w)
    return (y_p, y_s, k_p, v_p, f_p, c_p, n_p, m_p, k_s, v_s, f_s, c_s, n_s, m_s)
```

```python
import functools

import jax
import jax.numpy as jnp
from jax import lax
from jax.experimental import pallas as pl
from jax.experimental.pallas import tpu as pltpu

F32 = jnp.float32
BF16 = jnp.bfloat16

EPS = 1e-6
GATE_CAP = 15.0
A_HEADS = 8
A_DQK = 64
A_DV = 128
B_HEADS = 16
B_HD = 64
LANES = 128
HALF = 64
MXU_N = 256
FF_CHUNK = MXU_N
NEG_BIG = -1e30
LOG2E = 1.4426950408889634
N_BIAS_PIECES = 3
VMEM_LIMIT = 56 * 1024 * 1024


def _params(*sem):
    return pltpu.CompilerParams(dimension_semantics=sem, vmem_limit_bytes=VMEM_LIMIT)


def _resident(shape):
    nd = len(shape)
    return pl.BlockSpec(shape, lambda *_: (0,) * nd, pipeline_mode=pl.Buffered(1))


def _dot(a, b):
    return jnp.dot(a, b, preferred_element_type=F32)


def _dot_nt(a, b):
    return lax.dot_general(a, b, (((1,), (1,)), ((), ())), preferred_element_type=F32)


def _split2(x):
    hi = x.astype(BF16)
    lo = (x - hi.astype(F32)).astype(BF16)
    return hi, lo


def _split3(x):
    hi = x.astype(BF16)
    r = x - hi.astype(F32)
    mid = r.astype(BF16)
    lo = (r - mid.astype(F32)).astype(BF16)
    return hi, mid, lo


def _dot3(xh, xl, wh, wl):
    return _dot(xh, wh) + _dot(xh, wl) + _dot(xl, wh)


def _log_sigmoid(x):
    return jnp.minimum(x, 0.0) - jnp.log(1.0 + jnp.exp(-jnp.abs(x)))


def _sigmoid(x):
    return 1.0 / (1.0 + jnp.exp(-x))


def _soft_cap(z):
    return GATE_CAP * jnp.tanh(z / GATE_CAP)


def _rms_hat(x):
    return x * lax.rsqrt(jnp.mean(x * x, axis=-1, keepdims=True) + EPS)


def _pair_select(lo_mask, a, b):
    return jnp.where(lo_mask, a, b)


def _head_norm_group(xc, gc, lo_mask):
    sq = xc * xc
    s_lo = jnp.sum(jnp.where(lo_mask, sq, 0.0), axis=-1, keepdims=True)
    s_hi = jnp.sum(jnp.where(lo_mask, 0.0, sq), axis=-1, keepdims=True)
    ms = jnp.where(lo_mask, s_lo, s_hi) * (1.0 / B_HD)
    return xc * lax.rsqrt(ms + EPS) * gc


def _in_proj_kernel(x_ref, g_ref, w_ref, wgh_ref, wgl_ref, q_ref, k_ref, v_ref, o_ref, gates_ref):
    xn = _rms_hat(x_ref[...]) * g_ref[...]
    xh, xl = _split2(xn)
    qk_w = A_HEADS * A_DQK
    v_w = A_HEADS * A_DV
    q_ref[...] = _dot(xh, w_ref[:, 0:qk_w]).astype(BF16)
    k_ref[...] = (_dot(xh, w_ref[:, qk_w:2 * qk_w]) * (A_DQK ** -0.5)).astype(BF16)
    v_ref[...] = _dot(xh, w_ref[:, 2 * qk_w:2 * qk_w + v_w]).astype(BF16)
    o_ref[...] = _dot(xh, w_ref[:, 2 * qk_w + v_w:2 * qk_w + 2 * v_w])
    gz = _dot3(xh, xl, wgh_ref[...], wgl_ref[...])
    gates_ref[...] = gz[:, 0:2 * A_HEADS]


def _in_proj(x2d, g, w_main, wg_hi, wg_lo, tm):
    m, d = x2d.shape
    qk_w = A_HEADS * A_DQK
    v_w = A_HEADS * A_DV
    row = lambda i: (i, 0)
    return pl.pallas_call(
        _in_proj_kernel,
        grid=(m // tm,),
        in_specs=[pl.BlockSpec((tm, d), row), _resident(g.shape), _resident(w_main.shape),
                  _resident(wg_hi.shape), _resident(wg_lo.shape)],
        out_specs=[pl.BlockSpec((tm, qk_w), row), pl.BlockSpec((tm, qk_w), row),
                   pl.BlockSpec((tm, v_w), row), pl.BlockSpec((tm, v_w), row),
                   pl.BlockSpec((tm, 2 * A_HEADS), row)],
        out_shape=[jax.ShapeDtypeStruct((m, qk_w), BF16), jax.ShapeDtypeStruct((m, qk_w), BF16),
                   jax.ShapeDtypeStruct((m, v_w), BF16), jax.ShapeDtypeStruct((m, v_w), F32),
                   jax.ShapeDtypeStruct((m, 2 * A_HEADS), F32)],
        compiler_params=_params("parallel"),
        name="in_proj",
    )(x2d, g, w_main, wg_hi, wg_lo)


def _mlstm_kernel(q_ref, k_ref, v_ref, o_ref, gc_ref, gr_ref, bc_ref, br_ref, gh_ref,
                  c0_ref, n0_ref, m0_ref,
                  h_ref, c_out, n_out, m_out, ct_s, n_s, m_s, *, chunk):
    j = pl.program_id(1)
    L = chunk

    @pl.when(j == 0)
    def _():
        ct_s[...] = c0_ref[0]
        n_s[...] = n0_ref[0]
        m_s[...] = m0_ref[0]

    cap_c = _soft_cap(gc_ref[0] + bc_ref[...])
    cap_r = _soft_cap(gr_ref[0] + br_ref[...])
    li_c = cap_c[:, 0:A_HEADS]
    lf_c = _log_sigmoid(cap_c[:, A_HEADS:2 * A_HEADS])
    li_r = cap_r[0:A_HEADS, :]
    lf_r = _log_sigmoid(cap_r[A_HEADS:2 * A_HEADS, :])

    t_idx = lax.broadcasted_iota(jnp.int32, (L, L), 0)
    s_idx = lax.broadcasted_iota(jnp.int32, (L, L), 1)
    causal = s_idx <= t_idx
    lane = lax.broadcasted_iota(jnp.int32, (1, LANES), 1)
    lo_lane = lane < HALF
    lo_row = lax.broadcasted_iota(jnp.int32, (LANES, 1), 0) < HALF
    eye = (lax.broadcasted_iota(jnp.int32, (LANES, LANES), 0)
           == lax.broadcasted_iota(jnp.int32, (LANES, LANES), 1)).astype(BF16)

    for p in range(A_HEADS // 2):
        q2 = q_ref[0, :, p * LANES:(p + 1) * LANES]
        k2 = k_ref[0, :, p * LANES:(p + 1) * LANES]
        q2f = q2.astype(F32)
        k2f = k2.astype(F32)
        kt2 = _dot_nt(eye, k2).astype(BF16)
        ct2 = ct_s[p]
        ct2b = ct2.astype(BF16)
        n2 = n_s[p]
        qn = q2f * n2
        upd = []
        gcs = []
        wss = []
        for hh in range(2):
            h = 2 * p + hh
            half = lo_lane if hh == 0 else jnp.logical_not(lo_lane)
            li_col = li_c[:, h:h + 1]
            lf_col = lf_c[:, h:h + 1]
            li_row = li_r[h:h + 1, :]
            lf_row = lf_r[h:h + 1, :]
            m_prev = m_s[h][:, 0:1]

            b_col = jnp.sum(jnp.where(causal, lf_row, 0.0), axis=1, keepdims=True)
            b_row = jnp.sum(jnp.where(t_idx <= s_idx, lf_col, 0.0), axis=0, keepdims=True)
            dmat = jnp.where(causal, b_col - b_row + li_row, NEG_BIG)
            inter = b_col + m_prev
            m_t = jnp.maximum(inter, jnp.max(dmat, axis=1, keepdims=True))
            g = jnp.exp(inter - m_t)
            dec = jnp.where(causal, jnp.exp(dmat - m_t), 0.0)

            qm = jnp.where(half, q2, jnp.zeros_like(q2))
            s = _dot_nt(qm, k2) * dec
            v_h = v_ref[0, :, h * A_DV:(h + 1) * A_DV]
            num = g * _dot(qm, ct2b) + _dot(s.astype(BF16), v_h)
            den = (g * jnp.sum(jnp.where(half, qn, 0.0), axis=1, keepdims=True)
                   + jnp.sum(s, axis=1, keepdims=True))
            hv = num / jnp.maximum(jnp.abs(den), jnp.exp(-m_t))
            hv = hv * lax.rsqrt(jnp.mean(hv * hv, axis=-1, keepdims=True) + EPS)
            hv = hv * gh_ref[:, h * A_DV:(h + 1) * A_DV]
            hv = hv * _sigmoid(o_ref[0, :, h * A_DV:(h + 1) * A_DV])
            h_ref[0, :, h * A_DV:(h + 1) * A_DV] = hv.astype(h_ref.dtype)

            b_last = jnp.sum(lf_row, axis=1, keepdims=True)
            w_col = b_last - b_col + li_col
            m_new = jnp.maximum(b_last + m_prev, jnp.max(w_col, axis=0, keepdims=True))
            g_c = jnp.exp(b_last + m_prev - m_new)
            ws_col = jnp.exp(w_col - m_new)
            wsv = (ws_col * v_h.astype(F32)).astype(BF16)
            upd.append(_dot(kt2, wsv))
            gcs.append(g_c)
            wss.append(ws_col)
            m_s[h] = jnp.broadcast_to(m_new, (1, LANES))

        ct_s[p] = jnp.where(lo_row, gcs[0], gcs[1]) * ct2 + jnp.where(lo_row, upd[0], upd[1])
        ws2 = jnp.where(lo_lane, wss[0], wss[1])
        n_s[p] = (jnp.where(lo_lane, gcs[0], gcs[1]) * n2
                  + jnp.sum(ws2 * k2f, axis=0, keepdims=True))

    @pl.when(j == pl.num_programs(1) - 1)
    def _():
        c_out[0] = ct_s[...]
        n_out[0] = n_s[...]
        m_out[0] = m_s[...]


def _mlstm(q, k, v, o, gates, gates_t, b_col, b_row, g_head, ct0, n0, m0, chunk):
    bsz, t, _ = q.shape
    nc = t // chunk
    np_ = A_HEADS // 2
    tok = lambda b, j: (b, j, 0)
    st4 = lambda b, j: (b, 0, 0, 0)
    return pl.pallas_call(
        functools.partial(_mlstm_kernel, chunk=chunk),
        grid=(bsz, nc),
        in_specs=[pl.BlockSpec((1, chunk, q.shape[2]), tok), pl.BlockSpec((1, chunk, k.shape[2]), tok),
                  pl.BlockSpec((1, chunk, v.shape[2]), tok), pl.BlockSpec((1, chunk, o.shape[2]), tok),
                  pl.BlockSpec((1, chunk, 2 * A_HEADS), tok),
                  pl.BlockSpec((1, 2 * A_HEADS, chunk), lambda b, j: (b, 0, j)),
                  _resident(b_col.shape), _resident(b_row.shape), _resident(g_head.shape),
                  pl.BlockSpec((1, np_, LANES, A_DV), st4), pl.BlockSpec((1, np_, 1, LANES), st4),
                  pl.BlockSpec((1, A_HEADS, 1, LANES), st4)],
        out_specs=[pl.BlockSpec((1, chunk, v.shape[2]), tok),
                   pl.BlockSpec((1, np_, LANES, A_DV), st4), pl.BlockSpec((1, np_, 1, LANES), st4),
                   pl.BlockSpec((1, A_HEADS, 1, LANES), st4)],
        out_shape=[jax.ShapeDtypeStruct((bsz, t, v.shape[2]), BF16),
                   jax.ShapeDtypeStruct((bsz, np_, LANES, A_DV), F32),
                   jax.ShapeDtypeStruct((bsz, np_, 1, LANES), F32),
                   jax.ShapeDtypeStruct((bsz, A_HEADS, 1, LANES), F32)],
        scratch_shapes=[pltpu.VMEM((np_, LANES, A_DV), F32), pltpu.VMEM((np_, 1, LANES), F32),
                        pltpu.VMEM((A_HEADS, 1, LANES), F32)],
        compiler_params=_params("parallel", "arbitrary"),
        name="mlstm",
    )(q, k, v, o, gates, gates_t, b_col, b_row, g_head, ct0, n0, m0)


def _mix_ffn_kernel(a_ref, res_ref, wo_ref, g_ref, wg_ref, wu_ref, wd_ref, out_ref, xn_s, acc_s):
    x1 = res_ref[...] + _dot(a_ref[...], wo_ref[...])
    xn_s[...] = (_rms_hat(x1) * g_ref[...]).astype(BF16)
    acc_s[...] = x1

    def body(c, carry):
        xn = xn_s[...]
        gate = _dot(xn, wg_ref[c])
        up = _dot(xn, wu_ref[c])
        act = (gate * _sigmoid(gate) * up).astype(BF16)
        acc_s[...] += _dot(act, wd_ref[c])
        return carry

    lax.fori_loop(0, wg_ref.shape[0], body, 0)
    out_ref[...] = acc_s[...]


def _mix_ffn(a, res, w_o, g, wg, wu, wd, tm):
    m, d = res.shape
    row = lambda i: (i, 0)
    return pl.pallas_call(
        _mix_ffn_kernel,
        grid=(m // tm,),
        in_specs=[pl.BlockSpec((tm, a.shape[1]), row), pl.BlockSpec((tm, d), row),
                  _resident(w_o.shape), _resident(g.shape), _resident(wg.shape),
                  _resident(wu.shape), _resident(wd.shape)],
        out_specs=pl.BlockSpec((tm, d), row),
        out_shape=jax.ShapeDtypeStruct((m, d), F32),
        scratch_shapes=[pltpu.VMEM((tm, d), BF16), pltpu.VMEM((tm, d), F32)],
        compiler_params=_params("parallel"),
        name="mix_ffn",
    )(a, res, w_o, g, wg, wu, wd)


def _fox_proj_kernel(x_ref, gkv_ref, gmix_ref, wkv_ref, wfh_ref, wfl_ref, wqg_ref, bf_ref, gk_ref, gq_ref,
                     k_ref, v_ref, logf_ref, kb_ref, vb_ref, q_ref, gate_ref):
    d = x_ref.shape[1]
    xhat = _rms_hat(x_ref[...])
    sh, sl = _split2(xhat * gkv_ref[...])
    hq = (xhat * gmix_ref[...]).astype(BF16)
    lo_lane = lax.broadcasted_iota(jnp.int32, (1, LANES), 1) < HALF

    f = _dot3(sh, sl, wfh_ref[...], wfl_ref[...])[:, 0:B_HEADS]
    logf_ref[...] = _log_sigmoid(f + bf_ref[...])

    for c in range(d // MXU_N):
        kraw = _dot(sh, wkv_ref[:, c * MXU_N:(c + 1) * MXU_N])
        qraw = _dot(hq, wqg_ref[:, c * MXU_N:(c + 1) * MXU_N])
        for u in range(MXU_N // LANES):
            us = slice(u * LANES, (u + 1) * LANES)
            cs = slice(c * MXU_N + u * LANES, c * MXU_N + (u + 1) * LANES)
            kn = _head_norm_group(kraw[:, us], gk_ref[:, cs], lo_lane)
            k_ref[:, cs] = kn
            kb_ref[:, cs] = kn.astype(BF16)
            qn = _head_norm_group(qraw[:, us], gq_ref[:, cs], lo_lane)
            q_ref[:, cs] = (qn * (LOG2E * B_HD ** -0.5)).astype(BF16)
    v = _dot(sh, wkv_ref[:, d:2 * d])
    v_ref[...] = v
    vb_ref[...] = v.astype(BF16)
    gate_ref[...] = _dot(hq, wqg_ref[:, d:2 * d])


def _fox_proj(x2d, g_kv, g_mix, w_kv, wf_hi, wf_lo, w_qg, b_f, gk_t, gq_t, tm):
    m, d = x2d.shape
    row = lambda i: (i, 0)
    big = pl.BlockSpec((tm, d), row)
    return pl.pallas_call(
        _fox_proj_kernel,
        grid=(m // tm,),
        in_specs=[big] + [_resident(a.shape) for a in (g_kv, g_mix, w_kv, wf_hi, wf_lo, w_qg, b_f, gk_t, gq_t)],
        out_specs=[big, big, pl.BlockSpec((tm, B_HEADS), row), big, big, big, big],
        out_shape=[jax.ShapeDtypeStruct((m, d), F32), jax.ShapeDtypeStruct((m, d), F32),
                   jax.ShapeDtypeStruct((m, B_HEADS), F32),
                   jax.ShapeDtypeStruct((m, d), BF16), jax.ShapeDtypeStruct((m, d), BF16),
                   jax.ShapeDtypeStruct((m, d), BF16), jax.ShapeDtypeStruct((m, d), F32)],
        compiler_params=_params("parallel"),
        name="fox_proj",
    )(x2d, g_kv, g_mix, w_kv, wf_hi, wf_lo, w_qg, b_f, gk_t, gq_t)


def _cumsum_kernel(x_ref, init_ref, out_ref, *rest):
    aug_ref = rest[0] if len(rest) == 2 else None
    carry_s = rest[-1]

    @pl.when(pl.program_id(1) == 0)
    def _():
        carry_s[...] = init_ref[0]

    tb = x_ref.shape[1]
    tri = (lax.broadcasted_iota(jnp.int32, (tb, tb), 1)
           <= lax.broadcasted_iota(jnp.int32, (tb, tb), 0)).astype(BF16)
    hi, mid, lo = _split3(x_ref[0])
    cum = _dot(tri, hi) + _dot(tri, mid) + _dot(tri, lo) + carry_s[...]
    out_ref[0] = cum
    carry_s[...] = cum[tb - 1:tb, :]

    if aug_ref is not None:
        h_idx = lax.broadcasted_iota(jnp.int32, (B_HEADS, aug_ref.shape[2]), 0)
        c_idx = lax.broadcasted_iota(jnp.int32, (B_HEADS, aug_ref.shape[2]), 1)
        base = LANES * (h_idx // 2) + N_BIAS_PIECES * (h_idx % 2)
        aug = None
        for i, piece in enumerate(_split3(cum * (-LOG2E))):
            term = _dot(piece, (c_idx == base + i).astype(BF16))
            aug = term if aug is None else aug + term
        aug_ref[0] = aug.astype(BF16)


def _cumsum(x, init, tb, with_aug=False):
    bsz, t, h = x.shape
    blk = pl.BlockSpec((1, tb, h), lambda b, j: (b, j, 0))
    out_specs, out_shape = [blk], [jax.ShapeDtypeStruct((bsz, t, h), F32)]
    if with_aug:
        width = LANES * (h // 2)
        out_specs.append(pl.BlockSpec((1, tb, width), lambda b, j: (b, j, 0)))
        out_shape.append(jax.ShapeDtypeStruct((bsz, t, width), BF16))
    return pl.pallas_call(
        _cumsum_kernel,
        grid=(bsz, t // tb),
        in_specs=[blk, pl.BlockSpec((1, 1, h), lambda b, j: (b, 0, 0))],
        out_specs=out_specs,
        out_shape=out_shape,
        scratch_shapes=[pltpu.VMEM((1, h), F32)],
        compiler_params=_params("parallel", "arbitrary"),
        name="cumsum",
    )(x, init)


def _attn_pair_step(q2, k2, v2, bias, mask, m_s, acc_s, p, lo_lane):
    for hh in range(2):
        _attn_head_update(2 * p + hh, *_attn_head_scores(2 * p + hh, q2, k2, bias, mask, m_s, lo_lane),
                          v2, m_s, acc_s, lo_lane)


def _attn_head_scores(h, q2, k2, bias, mask, m_s, lo_lane):
    hh = h % 2
    tq = q2.shape[0]
    half = lo_lane if hh == 0 else jnp.logical_not(lo_lane)
    qm = jnp.where(half, q2, jnp.zeros_like(q2))
    if bias[0] == "lanes":
        lane2 = lax.broadcasted_iota(jnp.int32, (tq, LANES), 1)
        pick = jnp.logical_and(lane2 >= N_BIAS_PIECES * hh, lane2 < N_BIAS_PIECES * (hh + 1))
        s = _dot_nt(jnp.concatenate([qm, pick.astype(BF16)], axis=1),
                    jnp.concatenate([k2, bias[1]], axis=1))
    else:
        s = _dot_nt(qm, k2) - bias[1 + hh]
    cols = [s[:, c * LANES:(c + 1) * LANES] for c in range(s.shape[1] // LANES)] if s.shape[1] >= LANES else [s]
    if mask is not None:
        cols = [jnp.where(mask[:, c * LANES:c * LANES + col.shape[1]], col, NEG_BIG)
                for c, col in enumerate(cols)]
    top = functools.reduce(jnp.maximum, cols)
    m_old = m_s[h]
    m_new = jnp.maximum(m_old, jnp.max(top, axis=1, keepdims=True))
    return cols, m_old, m_new


def _attn_head_update(h, cols, m_old, m_new, v2, m_s, acc_s, lo_lane):
    half = lo_lane if h % 2 == 0 else jnp.logical_not(lo_lane)
    alpha = jnp.exp2(m_old - m_new)
    pr = [jnp.exp2(col - m_new[:, 0:col.shape[1]]).astype(BF16) for col in cols]
    pr = pr[0] if len(pr) == 1 else jnp.concatenate(pr, axis=1)
    v_ones = jnp.where(half, v2, jnp.ones_like(v2))
    acc_s[h] = alpha * acc_s[h] + _dot(pr, v_ones)
    m_s[h] = m_new


def _attn_init(m_s, acc_s):
    m_s[...] = jnp.full(m_s.shape, NEG_BIG, F32)
    acc_s[...] = jnp.zeros(acc_s.shape, F32)


def _attn_finish(out_ref, gate_ref, acc_s, lo_lane):
    for p in range(B_HEADS // 2):
        a0, a1 = acc_s[2 * p], acc_s[2 * p + 1]
        num = jnp.where(lo_lane, a0, a1)
        den = pltpu.roll(jnp.where(lo_lane, a1, a0), HALF, axis=1)
        gate = gate_ref[0, :, p * LANES:(p + 1) * LANES]
        out_ref[0, :, p * LANES:(p + 1) * LANES] = (num / den * _sigmoid(gate)).astype(out_ref.dtype)


def _prompt_attn_kernel(q_ref, k_ref, ca_ref, v_ref, gate_ref, out_ref, m_s, acc_s, *, tq, tk):
    i = pl.program_id(1)
    j = pl.program_id(2)
    lo_lane = lax.broadcasted_iota(jnp.int32, (1, LANES), 1) < HALF

    @pl.when(j == 0)
    def _():
        _attn_init(m_s, acc_s)

    def run(masked):
        mask = None
        if masked:
            t_pos = i * tq + lax.broadcasted_iota(jnp.int32, (tq, tk), 0)
            s_pos = j * tk + lax.broadcasted_iota(jnp.int32, (tq, tk), 1)
            mask = s_pos <= t_pos
        def scores(h):
            cs = slice((h // 2) * LANES, (h // 2 + 1) * LANES)
            return _attn_head_scores(h, q_ref[0, :, cs], k_ref[0, :, cs], ("lanes", ca_ref[0, :, cs]),
                                     mask, m_s, lo_lane)

        pending = scores(0)
        for h in range(B_HEADS):
            upcoming = scores(h + 1) if h + 1 < B_HEADS else None
            cs = slice((h // 2) * LANES, (h // 2 + 1) * LANES)
            _attn_head_update(h, *pending, v_ref[0, :, cs], m_s, acc_s, lo_lane)
            pending = upcoming

    last = ((i + 1) * tq - 1) // tk
    first_diag = (i * tq) // tk

    @pl.when(j < first_diag)
    def _():
        run(False)

    @pl.when(jnp.logical_and(j >= first_diag, j <= last))
    def _():
        run(True)

    @pl.when(j == last)
    def _():
        _attn_finish(out_ref, gate_ref, acc_s, lo_lane)


def _prompt_attn(q, kb, cum_aug, vb, gate, tq, tk):
    bsz, t, d = q.shape
    nq, nk = t // tq, t // tk
    last = lambda i: ((i + 1) * tq - 1) // tk
    qmap = lambda b, i, j: (b, i, 0)
    kvmap = lambda b, i, j: (b, jnp.minimum(j, last(i)), 0)
    return pl.pallas_call(
        functools.partial(_prompt_attn_kernel, tq=tq, tk=tk),
        grid=(bsz, nq, nk),
        in_specs=[pl.BlockSpec((1, tq, d), qmap), pl.BlockSpec((1, tk, d), kvmap),
                  pl.BlockSpec((1, tk, d), kvmap), pl.BlockSpec((1, tk, d), kvmap),
                  pl.BlockSpec((1, tq, d), qmap)],
        out_specs=pl.BlockSpec((1, tq, d), qmap),
        out_shape=jax.ShapeDtypeStruct((bsz, t, d), BF16),
        scratch_shapes=[pltpu.VMEM((B_HEADS, tq, LANES), F32), pltpu.VMEM((B_HEADS, tq, LANES), F32)],
        compiler_params=_params("parallel", "parallel", "arbitrary"),
        name="prompt_attn",
    )(q, kb, cum_aug, vb, gate)


def _sample_attn_kernel(q_ref, ck_ref, cv_ref, cct_ref, nk_ref, nv_ref, nct_ref, gate_ref, out_ref,
                        m_s, acc_s):
    j = pl.program_id(1)
    tq = q_ref.shape[1]
    lo_lane = lax.broadcasted_iota(jnp.int32, (1, LANES), 1) < HALF

    @pl.when(j == 0)
    def _():
        _attn_init(m_s, acc_s)

    def rows(ct_ref, p):
        return ("rows", ct_ref[0, 2 * p:2 * p + 1, :] * LOG2E, ct_ref[0, 2 * p + 1:2 * p + 2, :] * LOG2E)

    for p in range(B_HEADS // 2):
        cs = slice(p * LANES, (p + 1) * LANES)
        _attn_pair_step(q_ref[0, :, cs], ck_ref[0, :, cs].astype(BF16), cv_ref[0, :, cs].astype(BF16),
                        rows(cct_ref, p), None, m_s, acc_s, p, lo_lane)

    @pl.when(j == pl.num_programs(1) - 1)
    def _():
        mask = (lax.broadcasted_iota(jnp.int32, (tq, tq), 1)
                <= lax.broadcasted_iota(jnp.int32, (tq, tq), 0))
        for p in range(B_HEADS // 2):
            cs = slice(p * LANES, (p + 1) * LANES)
            _attn_pair_step(q_ref[0, :, cs], nk_ref[0, :, cs], nv_ref[0, :, cs],
                            rows(nct_ref, p), mask, m_s, acc_s, p, lo_lane)
        _attn_finish(out_ref, gate_ref, acc_s, lo_lane)


def _sample_attn(q, cache_k, cache_v, cache_cum_t, kb, vb, new_cum_t, gate, tk):
    bsz, tq, d = q.shape
    past = cache_k.shape[1]
    one = lambda b, j: (b, 0, 0)
    blk = lambda b, j: (b, j, 0)
    return pl.pallas_call(
        _sample_attn_kernel,
        grid=(bsz, past // tk),
        in_specs=[pl.BlockSpec((1, tq, d), one), pl.BlockSpec((1, tk, d), blk), pl.BlockSpec((1, tk, d), blk),
                  pl.BlockSpec((1, B_HEADS, tk), lambda b, j: (b, 0, j)),
                  pl.BlockSpec((1, tq, d), one), pl.BlockSpec((1, tq, d), one),
                  pl.BlockSpec((1, B_HEADS, tq), one), pl.BlockSpec((1, tq, d), one)],
        out_specs=pl.BlockSpec((1, tq, d), one),
        out_shape=jax.ShapeDtypeStruct((bsz, tq, d), BF16),
        scratch_shapes=[pltpu.VMEM((B_HEADS, tq, LANES), F32), pltpu.VMEM((B_HEADS, tq, LANES), F32)],
        compiler_params=_params("parallel", "arbitrary"),
        name="sample_attn",
    )(q, cache_k, cache_v, cache_cum_t, kb, vb, new_cum_t, gate)


def _pad_cols(w, n):
    return jnp.pad(w, ((0, 0), (0, n - w.shape[1])))


def _prep_weights(g_mix, g_ffn, w_in_a, b_i_a, b_f_a, g_head_a, w_out_a, g_kv, w_kvf, b_f_b, g_k,
                  w_qg_b, g_q_b, w_o_b, w_gu, w_down):
    d = g_mix.shape[1]
    main_w = 2 * A_HEADS * A_DQK + 2 * A_HEADS * A_DV
    w_in = w_in_a[0]
    wg = _pad_cols(w_in[:, main_w:], LANES)
    wg_hi, wg_lo = _split2(wg)
    wf = _pad_cols(w_kvf[:, 2 * d:], LANES)
    wf_hi, wf_lo = _split2(wf)
    gates_b = jnp.concatenate([b_i_a[0], b_f_a[0]])
    d_ff = w_down.shape[1]
    nch = d_ff // FF_CHUNK

    def ffn(layer):
        w = w_gu[layer].astype(BF16)
        wgate = w[:, :d_ff].reshape(d, nch, FF_CHUNK).transpose(1, 0, 2)
        wup = w[:, d_ff:].reshape(d, nch, FF_CHUNK).transpose(1, 0, 2)
        wdn = w_down[layer].astype(BF16).reshape(nch, FF_CHUNK, d)
        return wgate, wup, wdn

    return dict(
        g_mix0=g_mix[0][None], g_mix1=g_mix[1][None], g_ffn0=g_ffn[0][None], g_ffn1=g_ffn[1][None],
        w_in=w_in[:, :main_w].astype(BF16), wg_hi=wg_hi, wg_lo=wg_lo,
        gb_col=gates_b[None, :], gb_row=gates_b[:, None], g_head=g_head_a[0][None],
        w_out=w_out_a[0].astype(BF16),
        g_kv=g_kv[None], w_kv=w_kvf[:, :2 * d].astype(BF16), wf_hi=wf_hi, wf_lo=wf_lo, b_f=b_f_b[None],
        gk_t=jnp.tile(g_k, B_HEADS)[None], gq_t=jnp.tile(g_q_b[0], B_HEADS)[None],
        w_qg=w_qg_b[0].astype(BF16), w_o=w_o_b[0].astype(BF16),
        ffn0=ffn(0), ffn1=ffn(1),
    )


def _state_to_pairs(c, n, m):
    bsz = c.shape[0]
    np_ = A_HEADS // 2
    ct = c.reshape(bsz, np_, 2, A_DV, A_DQK).transpose(0, 1, 2, 4, 3).reshape(bsz, np_, 2 * A_DQK, A_DV)
    n2 = n.reshape(bsz, np_, 1, 2 * A_DQK)
    m2 = jnp.broadcast_to(m[:, :, None, None], (bsz, A_HEADS, 1, LANES))
    return ct.astype(F32), n2.astype(F32), m2.astype(F32)


def _state_from_pairs(ct, n2, m2):
    bsz = ct.shape[0]
    np_ = A_HEADS // 2
    c = ct.reshape(bsz, np_, 2, A_DQK, A_DV).transpose(0, 1, 2, 4, 3).reshape(bsz, 1, A_HEADS, A_DV, A_DQK)
    n = n2.reshape(bsz, 1, A_HEADS, A_DQK)
    m = m2[:, :, 0, 0].reshape(bsz, 1, A_HEADS)
    return c, n, m


def _layer_a(x, c0, n0, m0, w, chunk, tm):
    bsz, t, d = x.shape
    x2d = x.reshape(bsz * t, d)
    q, k, v, o, gates = _in_proj(x2d, w["g_mix0"], w["w_in"], w["wg_hi"], w["wg_lo"], tm)
    r3 = lambda z: z.reshape(bsz, t, z.shape[1])
    gates3 = r3(gates)
    ct0, n20, m20 = _state_to_pairs(c0, n0, m0)
    hn, ct, n2, m2 = _mlstm(r3(q), r3(k), r3(v), r3(o), gates3, gates3.transpose(0, 2, 1),
                            w["gb_col"], w["gb_row"], w["g_head"], ct0, n20, m20, chunk)
    x2 = _mix_ffn(hn.reshape(bsz * t, -1), x2d, w["w_out"], w["g_ffn0"], *w["ffn0"], tm)
    return x2, _state_from_pairs(ct, n2, m2)


def _fox_inputs(x2, w, tm):
    return _fox_proj(x2, w["g_kv"], w["g_mix1"], w["w_kv"], w["wf_hi"], w["wf_lo"], w["w_qg"],
                     w["b_f"], w["gk_t"], w["gq_t"], tm)


def _prompt_trunk(x, w):
    bsz, t, d = x.shape
    tm = 512
    zeros = lambda *s: jnp.zeros(s, F32)
    x2, (c, n, m) = _layer_a(x, zeros(bsz, A_HEADS, A_DV, A_DQK), zeros(bsz, A_HEADS, A_DQK),
                             zeros(bsz, A_HEADS), w, min(t, 128), tm)
    k, v, logf, kb, vb, q, gate = _fox_inputs(x2, w, tm)
    r3 = lambda z: z.reshape(bsz, t, z.shape[1])
    logf3 = r3(logf)
    _, cum_aug = _cumsum(logf3, zeros(bsz, 1, B_HEADS), min(t, 512), with_aug=True)
    attn = _prompt_attn(r3(q), r3(kb), cum_aug, r3(vb), r3(gate), min(t, 512), min(t, 512))
    y = _mix_ffn(attn.reshape(bsz * t, d), x2, w["w_o"], w["g_ffn1"], *w["ffn1"], tm)
    return (y.reshape(bsz, t, d), k.reshape(bsz, t, B_HEADS, B_HD), v.reshape(bsz, t, B_HEADS, B_HD),
            logf3, c, n, m)


def _sample_trunk(x, c0, n0, m0, cache_k, cache_v, cache_logf, w):
    bsz, t, d = x.shape
    past = cache_k.shape[1]
    tm = min(bsz * t, 512)
    x2, (c, n, m) = _layer_a(x, c0[:, 0], n0[:, 0], m0[:, 0], w, t, tm)
    k, v, logf, kb, vb, q, gate = _fox_inputs(x2, w, tm)
    r3 = lambda z: z.reshape(bsz, t, z.shape[1])
    logf3 = r3(logf)
    cache_cum, = _cumsum(cache_logf, jnp.zeros((bsz, 1, B_HEADS), F32), min(past, 512))
    new_cum, = _cumsum(logf3, cache_cum[:, past - 1:past, :], t)
    attn = _sample_attn(r3(q), cache_k.reshape(bsz, past, d), cache_v.reshape(bsz, past, d),
                        cache_cum.transpose(0, 2, 1), r3(kb), r3(vb), new_cum.transpose(0, 2, 1),
                        r3(gate), min(past, 512))
    y = _mix_ffn(attn.reshape(bsz * t, d), x2, w["w_o"], w["g_ffn1"], *w["ffn1"], tm)
    return (y.reshape(bsz, t, d), k.reshape(bsz, t, B_HEADS, B_HD), v.reshape(bsz, t, B_HEADS, B_HD),
            logf3, c, n, m)


def kernel(x_prompt, x_sample, cache_k, cache_v, cache_logf, state_c, state_n, state_m, g_mix, g_ffn, w_in_a, b_i_a, b_f_a, g_head_a, w_out_a, g_kv, w_kvf, b_f_b, g_k, w_qg_b, g_q_b, w_o_b, w_gu, w_down):
    w = _prep_weights(g_mix, g_ffn, w_in_a, b_i_a, b_f_a, g_head_a, w_out_a, g_kv, w_kvf, b_f_b, g_k,
                      w_qg_b, g_q_b, w_o_b, w_gu, w_down)
    y_p, k_p, v_p, f_p, c_p, n_p, m_p = _prompt_trunk(x_prompt, w)
    y_s, k_s, v_s, f_s, c_s, n_s, m_s = _sample_trunk(x_sample, state_c, state_n, state_m,
                                                      cache_k, cache_v, cache_logf, w)
    return (y_p, y_s, k_p, v_p, f_p, c_p, n_p, m_p, k_s, v_s, f_s, c_s, n_s, m_s)
```

```python
import functools

import jax
import jax.numpy as jnp
from jax import lax
from jax.experimental import pallas as pl
from jax.experimental.pallas import tpu as pltpu

F32 = jnp.float32
BF16 = jnp.bfloat16

EPS = 1e-6
GATE_CAP = 15.0
A_HEADS = 8
A_DQK = 64
A_DV = 128
B_HEADS = 16
B_HD = 64
LANES = 128
HALF = 64
MXU_N = 256
FF_CHUNK = MXU_N
NEG_BIG = -1e30
LOG2E = 1.4426950408889634
N_BIAS_PIECES = 3
VMEM_LIMIT = 56 * 1024 * 1024


def _params(*sem):
    return pltpu.CompilerParams(dimension_semantics=sem, vmem_limit_bytes=VMEM_LIMIT)


def _resident(shape):
    nd = len(shape)
    return pl.BlockSpec(shape, lambda *_: (0,) * nd, pipeline_mode=pl.Buffered(1))


def _dot(a, b):
    return jnp.dot(a, b, preferred_element_type=F32)


def _dot_nt(a, b):
    return lax.dot_general(a, b, (((1,), (1,)), ((), ())), preferred_element_type=F32)


def _split2(x):
    hi = x.astype(BF16)
    lo = (x - hi.astype(F32)).astype(BF16)
    return hi, lo


def _split3(x):
    hi = x.astype(BF16)
    r = x - hi.astype(F32)
    mid = r.astype(BF16)
    lo = (r - mid.astype(F32)).astype(BF16)
    return hi, mid, lo


def _dot3(xh, xl, wh, wl):
    return _dot(xh, wh) + _dot(xh, wl) + _dot(xl, wh)


def _log_sigmoid(x):
    return jnp.minimum(x, 0.0) - jnp.log(1.0 + jnp.exp(-jnp.abs(x)))


def _sigmoid(x):
    return 1.0 / (1.0 + jnp.exp(-x))


def _soft_cap(z):
    return GATE_CAP * jnp.tanh(z / GATE_CAP)


def _rms_hat(x):
    return x * lax.rsqrt(jnp.mean(x * x, axis=-1, keepdims=True) + EPS)


def _pair_select(lo_mask, a, b):
    return jnp.where(lo_mask, a, b)


def _head_norm_group(xc, gc, lo_mask):
    sq = xc * xc
    s_lo = jnp.sum(jnp.where(lo_mask, sq, 0.0), axis=-1, keepdims=True)
    s_hi = jnp.sum(jnp.where(lo_mask, 0.0, sq), axis=-1, keepdims=True)
    ms = jnp.where(lo_mask, s_lo, s_hi) * (1.0 / B_HD)
    return xc * lax.rsqrt(ms + EPS) * gc


def _in_proj_kernel(x_ref, g_ref, w_ref, wgh_ref, wgl_ref, q_ref, k_ref, v_ref, o_ref, gates_ref):
    xn = _rms_hat(x_ref[...]) * g_ref[...]
    xh, xl = _split2(xn)
    qk_w = A_HEADS * A_DQK
    v_w = A_HEADS * A_DV
    q_ref[...] = _dot(xh, w_ref[:, 0:qk_w]).astype(BF16)
    k_ref[...] = (_dot(xh, w_ref[:, qk_w:2 * qk_w]) * (A_DQK ** -0.5)).astype(BF16)
    v_ref[...] = _dot(xh, w_ref[:, 2 * qk_w:2 * qk_w + v_w]).astype(BF16)
    o_ref[...] = _dot(xh, w_ref[:, 2 * qk_w + v_w:2 * qk_w + 2 * v_w])
    gz = _dot3(xh, xl, wgh_ref[...], wgl_ref[...])
    gates_ref[...] = gz[:, 0:2 * A_HEADS]


def _in_proj(x2d, g, w_main, wg_hi, wg_lo, tm):
    m, d = x2d.shape
    qk_w = A_HEADS * A_DQK
    v_w = A_HEADS * A_DV
    row = lambda i: (i, 0)
    return pl.pallas_call(
        _in_proj_kernel,
        grid=(m // tm,),
        in_specs=[pl.BlockSpec((tm, d), row), _resident(g.shape), _resident(w_main.shape),
                  _resident(wg_hi.shape), _resident(wg_lo.shape)],
        out_specs=[pl.BlockSpec((tm, qk_w), row), pl.BlockSpec((tm, qk_w), row),
                   pl.BlockSpec((tm, v_w), row), pl.BlockSpec((tm, v_w), row),
                   pl.BlockSpec((tm, 2 * A_HEADS), row)],
        out_shape=[jax.ShapeDtypeStruct((m, qk_w), BF16), jax.ShapeDtypeStruct((m, qk_w), BF16),
                   jax.ShapeDtypeStruct((m, v_w), BF16), jax.ShapeDtypeStruct((m, v_w), F32),
                   jax.ShapeDtypeStruct((m, 2 * A_HEADS), F32)],
        compiler_params=_params("parallel"),
        name="in_proj",
    )(x2d, g, w_main, wg_hi, wg_lo)


def _mlstm_kernel(q_ref, k_ref, v_ref, o_ref, gc_ref, gr_ref, bc_ref, br_ref, gh_ref,
                  c0_ref, n0_ref, m0_ref,
                  h_ref, c_out, n_out, m_out, ct_s, n_s, m_s, *, chunk):
    j = pl.program_id(1)
    L = chunk

    @pl.when(j == 0)
    def _():
        ct_s[...] = c0_ref[0]
        n_s[...] = n0_ref[0]
        m_s[...] = m0_ref[0]

    cap_c = _soft_cap(gc_ref[0] + bc_ref[...])
    cap_r = _soft_cap(gr_ref[0] + br_ref[...])
    li_c = cap_c[:, 0:A_HEADS]
    lf_c = _log_sigmoid(cap_c[:, A_HEADS:2 * A_HEADS])
    li_r = cap_r[0:A_HEADS, :]
    lf_r = _log_sigmoid(cap_r[A_HEADS:2 * A_HEADS, :])

    t_idx = lax.broadcasted_iota(jnp.int32, (L, L), 0)
    s_idx = lax.broadcasted_iota(jnp.int32, (L, L), 1)
    causal = s_idx <= t_idx
    lane = lax.broadcasted_iota(jnp.int32, (1, LANES), 1)
    lo_lane = lane < HALF
    lo_row = lax.broadcasted_iota(jnp.int32, (LANES, 1), 0) < HALF
    eye = (lax.broadcasted_iota(jnp.int32, (LANES, LANES), 0)
           == lax.broadcasted_iota(jnp.int32, (LANES, LANES), 1)).astype(BF16)

    def head_stages(h, q2, k2, kt2, ct2b, qn, done):
        half = lo_lane if h % 2 == 0 else jnp.logical_not(lo_lane)
        li_col = li_c[:, h:h + 1]
        lf_col = lf_c[:, h:h + 1]
        li_row = li_r[h:h + 1, :]
        lf_row = lf_r[h:h + 1, :]
        m_prev = m_s[h][:, 0:1]
        v_h = v_ref[0, :, h * A_DV:(h + 1) * A_DV]
        qm = jnp.where(half, q2, jnp.zeros_like(q2))
        qk = _dot_nt(qm, k2)
        qc = _dot(qm, ct2b)
        qn_sum = jnp.sum(jnp.where(half, qn, 0.0), axis=1, keepdims=True)
        b_col = jnp.sum(jnp.where(causal, lf_row, 0.0), axis=1, keepdims=True)
        b_row = jnp.sum(jnp.where(t_idx <= s_idx, lf_col, 0.0), axis=0, keepdims=True)
        b_last = jnp.sum(lf_row, axis=1, keepdims=True)
        yield
        dmat = jnp.where(causal, b_col - b_row + li_row, NEG_BIG)
        inter = b_col + m_prev
        m_t = jnp.maximum(inter, jnp.max(dmat, axis=1, keepdims=True))
        w_col = b_last - b_col + li_col
        m_new = jnp.maximum(b_last + m_prev, jnp.max(w_col, axis=0, keepdims=True))
        yield
        g = jnp.exp(inter - m_t)
        dec = jnp.where(causal, jnp.exp(dmat - m_t), 0.0)
        floor = jnp.exp(-m_t)
        g_c = jnp.exp(b_last + m_prev - m_new)
        ws_col = jnp.exp(w_col - m_new)
        yield
        s = qk * dec
        wsv = (ws_col * v_h.astype(F32)).astype(BF16)
        upd = _dot(kt2, wsv)
        num = g * qc + _dot(s.astype(BF16), v_h)
        den = g * qn_sum + jnp.sum(s, axis=1, keepdims=True)
        yield
        hv = num / jnp.maximum(jnp.abs(den), floor)
        ms = jnp.mean(hv * hv, axis=-1, keepdims=True)
        yield
        hv = hv * lax.rsqrt(ms + EPS)
        hv = hv * gh_ref[:, h * A_DV:(h + 1) * A_DV]
        hv = hv * _sigmoid(o_ref[0, :, h * A_DV:(h + 1) * A_DV])
        h_ref[0, :, h * A_DV:(h + 1) * A_DV] = hv.astype(h_ref.dtype)
        m_s[h] = jnp.broadcast_to(m_new, (1, LANES))
        done[h] = (upd, g_c, ws_col)

    pairs = []
    done = {}
    gens = []
    for p in range(A_HEADS // 2):
        q2 = q_ref[0, :, p * LANES:(p + 1) * LANES]
        k2 = k_ref[0, :, p * LANES:(p + 1) * LANES]
        kt2 = _dot_nt(eye, k2).astype(BF16)
        ct2 = ct_s[p]
        n2 = n_s[p]
        pairs.append((k2, ct2, n2))
        for hh in range(2):
            gens.append(head_stages(2 * p + hh, q2, k2, kt2, ct2.astype(BF16), q2.astype(F32) * n2, done))
    while gens:
        alive = []
        for gen in gens:
            try:
                next(gen)
                alive.append(gen)
            except StopIteration:
                pass
        gens = alive

    for p, (k2, ct2, n2) in enumerate(pairs):
        (upd0, gc0, ws0), (upd1, gc1, ws1) = done[2 * p], done[2 * p + 1]
        ct_s[p] = jnp.where(lo_row, gc0, gc1) * ct2 + jnp.where(lo_row, upd0, upd1)
        ws2 = jnp.where(lo_lane, ws0, ws1)
        n_s[p] = (jnp.where(lo_lane, gc0, gc1) * n2
                  + jnp.sum(ws2 * k2.astype(F32), axis=0, keepdims=True))

    @pl.when(j == pl.num_programs(1) - 1)
    def _():
        c_out[0] = ct_s[...]
        n_out[0] = n_s[...]
        m_out[0] = m_s[...]


def _mlstm(q, k, v, o, gates, gates_t, b_col, b_row, g_head, ct0, n0, m0, chunk):
    bsz, t, _ = q.shape
    nc = t // chunk
    np_ = A_HEADS // 2
    tok = lambda b, j: (b, j, 0)
    st4 = lambda b, j: (b, 0, 0, 0)
    return pl.pallas_call(
        functools.partial(_mlstm_kernel, chunk=chunk),
        grid=(bsz, nc),
        in_specs=[pl.BlockSpec((1, chunk, q.shape[2]), tok), pl.BlockSpec((1, chunk, k.shape[2]), tok),
                  pl.BlockSpec((1, chunk, v.shape[2]), tok), pl.BlockSpec((1, chunk, o.shape[2]), tok),
                  pl.BlockSpec((1, chunk, 2 * A_HEADS), tok),
                  pl.BlockSpec((1, 2 * A_HEADS, chunk), lambda b, j: (b, 0, j)),
                  _resident(b_col.shape), _resident(b_row.shape), _resident(g_head.shape),
                  pl.BlockSpec((1, np_, LANES, A_DV), st4), pl.BlockSpec((1, np_, 1, LANES), st4),
                  pl.BlockSpec((1, A_HEADS, 1, LANES), st4)],
        out_specs=[pl.BlockSpec((1, chunk, v.shape[2]), tok),
                   pl.BlockSpec((1, np_, LANES, A_DV), st4), pl.BlockSpec((1, np_, 1, LANES), st4),
                   pl.BlockSpec((1, A_HEADS, 1, LANES), st4)],
        out_shape=[jax.ShapeDtypeStruct((bsz, t, v.shape[2]), BF16),
                   jax.ShapeDtypeStruct((bsz, np_, LANES, A_DV), F32),
                   jax.ShapeDtypeStruct((bsz, np_, 1, LANES), F32),
                   jax.ShapeDtypeStruct((bsz, A_HEADS, 1, LANES), F32)],
        scratch_shapes=[pltpu.VMEM((np_, LANES, A_DV), F32), pltpu.VMEM((np_, 1, LANES), F32),
                        pltpu.VMEM((A_HEADS, 1, LANES), F32)],
        compiler_params=_params("parallel", "arbitrary"),
        name="mlstm",
    )(q, k, v, o, gates, gates_t, b_col, b_row, g_head, ct0, n0, m0)


def _mix_ffn_kernel(a_ref, res_ref, wo_ref, g_ref, wg_ref, wu_ref, wd_ref, out_ref, xn_s, acc_s):
    x1 = res_ref[...] + _dot(a_ref[...], wo_ref[...])
    xn_s[...] = (_rms_hat(x1) * g_ref[...]).astype(BF16)
    acc_s[...] = x1

    def body(c, carry):
        xn = xn_s[...]
        gate = _dot(xn, wg_ref[c])
        up = _dot(xn, wu_ref[c])
        act = (gate * _sigmoid(gate) * up).astype(BF16)
        acc_s[...] += _dot(act, wd_ref[c])
        return carry

    lax.fori_loop(0, wg_ref.shape[0], body, 0)
    out_ref[...] = acc_s[...]


def _mix_ffn(a, res, w_o, g, wg, wu, wd, tm):
    m, d = res.shape
    row = lambda i: (i, 0)
    return pl.pallas_call(
        _mix_ffn_kernel,
        grid=(m // tm,),
        in_specs=[pl.BlockSpec((tm, a.shape[1]), row), pl.BlockSpec((tm, d), row),
                  _resident(w_o.shape), _resident(g.shape), _resident(wg.shape),
                  _resident(wu.shape), _resident(wd.shape)],
        out_specs=pl.BlockSpec((tm, d), row),
        out_shape=jax.ShapeDtypeStruct((m, d), F32),
        scratch_shapes=[pltpu.VMEM((tm, d), BF16), pltpu.VMEM((tm, d), F32)],
        compiler_params=_params("parallel"),
        name="mix_ffn",
    )(a, res, w_o, g, wg, wu, wd)


def _fox_proj_kernel(x_ref, gkv_ref, gmix_ref, wkv_ref, wfh_ref, wfl_ref, wqg_ref, bf_ref, gk_ref, gq_ref,
                     k_ref, v_ref, logf_ref, kb_ref, vb_ref, q_ref, gate_ref):
    d = x_ref.shape[1]
    xhat = _rms_hat(x_ref[...])
    sh, sl = _split2(xhat * gkv_ref[...])
    hq = (xhat * gmix_ref[...]).astype(BF16)
    lo_lane = lax.broadcasted_iota(jnp.int32, (1, LANES), 1) < HALF

    f = _dot3(sh, sl, wfh_ref[...], wfl_ref[...])[:, 0:B_HEADS]
    logf_ref[...] = _log_sigmoid(f + bf_ref[...])

    for c in range(d // MXU_N):
        kraw = _dot(sh, wkv_ref[:, c * MXU_N:(c + 1) * MXU_N])
        qraw = _dot(hq, wqg_ref[:, c * MXU_N:(c + 1) * MXU_N])
        for u in range(MXU_N // LANES):
            us = slice(u * LANES, (u + 1) * LANES)
            cs = slice(c * MXU_N + u * LANES, c * MXU_N + (u + 1) * LANES)
            kn = _head_norm_group(kraw[:, us], gk_ref[:, cs], lo_lane)
            k_ref[:, cs] = kn
            kb_ref[:, cs] = kn.astype(BF16)
            qn = _head_norm_group(qraw[:, us], gq_ref[:, cs], lo_lane)
            q_ref[:, cs] = (qn * (LOG2E * B_HD ** -0.5)).astype(BF16)
    v = _dot(sh, wkv_ref[:, d:2 * d])
    v_ref[...] = v
    vb_ref[...] = v.astype(BF16)
    gate_ref[...] = _dot(hq, wqg_ref[:, d:2 * d])


def _fox_proj(x2d, g_kv, g_mix, w_kv, wf_hi, wf_lo, w_qg, b_f, gk_t, gq_t, tm):
    m, d = x2d.shape
    row = lambda i: (i, 0)
    big = pl.BlockSpec((tm, d), row)
    return pl.pallas_call(
        _fox_proj_kernel,
        grid=(m // tm,),
        in_specs=[big] + [_resident(a.shape) for a in (g_kv, g_mix, w_kv, wf_hi, wf_lo, w_qg, b_f, gk_t, gq_t)],
        out_specs=[big, big, pl.BlockSpec((tm, B_HEADS), row), big, big, big, big],
        out_shape=[jax.ShapeDtypeStruct((m, d), F32), jax.ShapeDtypeStruct((m, d), F32),
                   jax.ShapeDtypeStruct((m, B_HEADS), F32),
                   jax.ShapeDtypeStruct((m, d), BF16), jax.ShapeDtypeStruct((m, d), BF16),
                   jax.ShapeDtypeStruct((m, d), BF16), jax.ShapeDtypeStruct((m, d), F32)],
        compiler_params=_params("parallel"),
        name="fox_proj",
    )(x2d, g_kv, g_mix, w_kv, wf_hi, wf_lo, w_qg, b_f, gk_t, gq_t)


def _cum_bias_kernel(x_ref, aug_ref, carry_s):
    @pl.when(pl.program_id(1) == 0)
    def _():
        carry_s[...] = jnp.zeros(carry_s.shape, F32)

    tb = x_ref.shape[1]
    tri = (lax.broadcasted_iota(jnp.int32, (tb, tb), 1)
           <= lax.broadcasted_iota(jnp.int32, (tb, tb), 0)).astype(BF16)
    hi, mid, lo = _split3(x_ref[0])
    cum = _dot(tri, hi) + _dot(tri, mid) + _dot(tri, lo) + carry_s[...]
    carry_s[...] = cum[tb - 1:tb, :]

    h_idx = lax.broadcasted_iota(jnp.int32, (B_HEADS, aug_ref.shape[2]), 0)
    c_idx = lax.broadcasted_iota(jnp.int32, (B_HEADS, aug_ref.shape[2]), 1)
    base = LANES * (h_idx // 2) + N_BIAS_PIECES * (h_idx % 2)
    aug = None
    for i, piece in enumerate(_split3(cum * (-LOG2E))):
        term = _dot(piece, (c_idx == base + i).astype(BF16))
        aug = term if aug is None else aug + term
    aug_ref[0] = aug.astype(BF16)


def _cum_bias(logf, tb):
    bsz, t, h = logf.shape
    width = LANES * (h // 2)
    return pl.pallas_call(
        _cum_bias_kernel,
        grid=(bsz, t // tb),
        in_specs=[pl.BlockSpec((1, tb, h), lambda b, j: (b, j, 0))],
        out_specs=pl.BlockSpec((1, tb, width), lambda b, j: (b, j, 0)),
        out_shape=jax.ShapeDtypeStruct((bsz, t, width), BF16),
        scratch_shapes=[pltpu.VMEM((1, h), F32)],
        compiler_params=_params("parallel", "arbitrary"),
        name="cum_bias",
    )(logf)


def _attn_head_scores(h, q2, k2, ca2, mask, m_s, lo_lane):
    hh = h % 2
    tq = q2.shape[0]
    half = lo_lane if hh == 0 else jnp.logical_not(lo_lane)
    qm = jnp.where(half, q2, jnp.zeros_like(q2))
    lane2 = lax.broadcasted_iota(jnp.int32, (tq, LANES), 1)
    pick = jnp.logical_and(lane2 >= N_BIAS_PIECES * hh, lane2 < N_BIAS_PIECES * (hh + 1))
    s = _dot_nt(jnp.concatenate([qm, pick.astype(BF16)], axis=1),
                jnp.concatenate([k2, ca2], axis=1))
    cols = [s[:, c * LANES:(c + 1) * LANES] for c in range(s.shape[1] // LANES)]
    if mask is not None:
        cols = [jnp.where(mask[:, c * LANES:(c + 1) * LANES], col, NEG_BIG) for c, col in enumerate(cols)]
    top = functools.reduce(jnp.maximum, cols)
    m_old = m_s[h]
    m_new = jnp.maximum(m_old, jnp.max(top, axis=1, keepdims=True))
    return cols, m_old, m_new


def _attn_head_update(h, cols, m_old, m_new, v2, m_s, acc_s, lo_lane):
    half = lo_lane if h % 2 == 0 else jnp.logical_not(lo_lane)
    alpha = jnp.exp2(m_old - m_new)
    pr = jnp.concatenate([jnp.exp2(col - m_new).astype(BF16) for col in cols], axis=1)
    v_ones = jnp.where(half, v2, jnp.ones_like(v2))
    acc_s[h] = alpha * acc_s[h] + _dot(pr, v_ones)
    m_s[h] = m_new


def _attn_init(m_s, acc_s):
    m_s[...] = jnp.full(m_s.shape, NEG_BIG, F32)
    acc_s[...] = jnp.zeros(acc_s.shape, F32)


def _attn_finish(out_ref, gate_ref, acc_s, lo_lane):
    for p in range(B_HEADS // 2):
        a0, a1 = acc_s[2 * p], acc_s[2 * p + 1]
        num = jnp.where(lo_lane, a0, a1)
        den = pltpu.roll(jnp.where(lo_lane, a1, a0), HALF, axis=1)
        gate = gate_ref[0, :, p * LANES:(p + 1) * LANES]
        out_ref[0, :, p * LANES:(p + 1) * LANES] = (num / den * _sigmoid(gate)).astype(out_ref.dtype)


def _prompt_attn_kernel(q_ref, k_ref, ca_ref, v_ref, gate_ref, out_ref, m_s, acc_s, *, tq, tk):
    i = pl.program_id(1)
    j = pl.program_id(2)
    lo_lane = lax.broadcasted_iota(jnp.int32, (1, LANES), 1) < HALF

    @pl.when(j == 0)
    def _():
        _attn_init(m_s, acc_s)

    def run(masked):
        mask = None
        if masked:
            t_pos = i * tq + lax.broadcasted_iota(jnp.int32, (tq, tk), 0)
            s_pos = j * tk + lax.broadcasted_iota(jnp.int32, (tq, tk), 1)
            mask = s_pos <= t_pos
        def scores(h):
            cs = slice((h // 2) * LANES, (h // 2 + 1) * LANES)
            return _attn_head_scores(h, q_ref[0, :, cs], k_ref[0, :, cs], ca_ref[0, :, cs],
                                     mask, m_s, lo_lane)

        pending = scores(0)
        for h in range(B_HEADS):
            upcoming = scores(h + 1) if h + 1 < B_HEADS else None
            cs = slice((h // 2) * LANES, (h // 2 + 1) * LANES)
            _attn_head_update(h, *pending, v_ref[0, :, cs], m_s, acc_s, lo_lane)
            pending = upcoming

    last = ((i + 1) * tq - 1) // tk
    first_diag = (i * tq) // tk

    @pl.when(j < first_diag)
    def _():
        run(False)

    @pl.when(jnp.logical_and(j >= first_diag, j <= last))
    def _():
        run(True)

    @pl.when(j == last)
    def _():
        _attn_finish(out_ref, gate_ref, acc_s, lo_lane)


def _prompt_attn(q, kb, cum_aug, vb, gate, tq, tk):
    bsz, t, d = q.shape
    nq, nk = t // tq, t // tk
    last = lambda i: ((i + 1) * tq - 1) // tk
    qmap = lambda b, i, j: (b, i, 0)
    kvmap = lambda b, i, j: (b, jnp.minimum(j, last(i)), 0)
    return pl.pallas_call(
        functools.partial(_prompt_attn_kernel, tq=tq, tk=tk),
        grid=(bsz, nq, nk),
        in_specs=[pl.BlockSpec((1, tq, d), qmap), pl.BlockSpec((1, tk, d), kvmap),
                  pl.BlockSpec((1, tk, d), kvmap), pl.BlockSpec((1, tk, d), kvmap),
                  pl.BlockSpec((1, tq, d), qmap)],
        out_specs=pl.BlockSpec((1, tq, d), qmap),
        out_shape=jax.ShapeDtypeStruct((bsz, t, d), BF16),
        scratch_shapes=[pltpu.VMEM((B_HEADS, tq, LANES), F32), pltpu.VMEM((B_HEADS, tq, LANES), F32)],
        compiler_params=_params("parallel", "parallel", "arbitrary"),
        name="prompt_attn",
    )(q, kb, cum_aug, vb, gate)


def _group_rows(x, tq):
    return jnp.concatenate([jnp.broadcast_to(x[r:r + 1, :], (tq, x.shape[1])) for r in range(x.shape[0])], axis=0)


def _lane_cumsum(x):
    r, n = x.shape
    nb = n // MXU_N
    stacked = jnp.concatenate([x[:, c * MXU_N:(c + 1) * MXU_N] for c in range(nb)], axis=0)
    upper = (lax.broadcasted_iota(jnp.int32, (MXU_N, MXU_N), 0)
             <= lax.broadcasted_iota(jnp.int32, (MXU_N, MXU_N), 1)).astype(BF16)
    within = sum(_dot(piece, upper) for piece in _split3(stacked))
    totals = jnp.broadcast_to(within[:, MXU_N - 1:MXU_N], (nb * r, LANES))
    ri = lax.broadcasted_iota(jnp.int32, (nb * r, nb * r), 0)
    ci = lax.broadcasted_iota(jnp.int32, (nb * r, nb * r), 1)
    earlier = jnp.logical_and(ci // r < ri // r, ci % r == ri % r).astype(BF16)
    offs = sum(_dot(earlier, piece) for piece in _split3(totals))
    full = within + jnp.concatenate([offs] * (MXU_N // LANES), axis=1)
    return [full[c * r:(c + 1) * r, :] for c in range(nb)]


def _sample_attn_kernel(q_ref, kt_ref, vt_ref, lf_ref, nk_ref, nv_ref, nlf_ref, gate_ref, out_ref,
                        cum_s, m_s, l_s, acc_s, *, tk):
    j = pl.program_id(1)
    tq = q_ref.shape[1]
    n_groups = acc_s.shape[0]
    hg = B_HEADS // n_groups
    gw = hg * B_HD
    rows = hg * tq
    row_head = lax.broadcasted_iota(jnp.int32, (rows, gw), 0) // tq
    lane_head = lax.broadcasted_iota(jnp.int32, (rows, gw), 1) // B_HD
    own = row_head == lane_head

    @pl.when(j == 0)
    def _():
        m_s[...] = jnp.full(m_s.shape, NEG_BIG, F32)
        l_s[...] = jnp.zeros(l_s.shape, F32)
        acc_s[...] = jnp.zeros(acc_s.shape, F32)
        blocks = _lane_cumsum(lf_ref[0])
        per_step = tk // MXU_N
        for c, blk in enumerate(blocks):
            cum_s[c // per_step, :, (c % per_step) * MXU_N:(c % per_step + 1) * MXU_N] = blk * LOG2E

    def q_block_diag(g):
        qg = q_ref[0, :, g * gw:(g + 1) * gw]
        qrep = jnp.concatenate([qg] * hg, axis=0)
        return jnp.where(own, qrep, jnp.zeros_like(qrep))

    def online_update(g, cols, pv_fn):
        top = functools.reduce(jnp.maximum, cols)
        m_old = m_s[g]
        m_new = jnp.maximum(m_old, jnp.max(top, axis=1, keepdims=True))
        alpha = jnp.exp2(m_old - m_new)
        probs = [jnp.exp2(col - m_new[:, 0:col.shape[1]]) for col in cols]
        if cols[0].shape[1] == LANES:
            psum = functools.reduce(jnp.add, probs)
        else:
            lane0 = lax.broadcasted_iota(jnp.int32, (rows, LANES), 1) == 0
            psum = jnp.where(lane0, jnp.sum(probs[0], axis=1, keepdims=True), 0.0)
        l_s[g] = alpha * l_s[g] + psum
        p = probs[0] if len(probs) == 1 else jnp.concatenate(probs, axis=1)
        acc_s[g] = jnp.concatenate([alpha] * (gw // LANES), axis=1) * acc_s[g] + pv_fn(p.astype(BF16))
        m_s[g] = m_new

    cum_blk = cum_s[j]
    for g in range(n_groups):
        kt = kt_ref[0, g * gw:(g + 1) * gw, :].astype(BF16)
        s = _dot(q_block_diag(g), kt) - _group_rows(cum_blk[g * hg:(g + 1) * hg, :], tq)
        cols = [s[:, c * LANES:(c + 1) * LANES] for c in range(tk // LANES)]
        vt = vt_ref[0, g * gw:(g + 1) * gw, :].astype(BF16)
        online_update(g, cols, lambda p, vt=vt: _dot_nt(p, vt))

    @pl.when(j == pl.num_programs(1) - 1)
    def _():
        t_q = lax.broadcasted_iota(jnp.int32, (rows, tq), 0) % tq
        t_k = lax.broadcasted_iota(jnp.int32, (rows, tq), 1)
        causal = t_k <= t_q
        upper = (lax.broadcasted_iota(jnp.int32, (tq, tq), 0)
                 <= lax.broadcasted_iota(jnp.int32, (tq, tq), 1)).astype(BF16)
        new_cum = sum(_dot(piece, upper) for piece in _split3(nlf_ref[0]))
        last = cum_s[pl.num_programs(1) - 1][:, tk - 1:tk]
        new_cum = new_cum * LOG2E + last
        lane_grp = lax.broadcasted_iota(jnp.int32, (tq, gw), 1) // B_HD
        for g in range(n_groups):
            kn = nk_ref[0, :, g * gw:(g + 1) * gw]
            s = _dot_nt(q_block_diag(g), kn) - _group_rows(new_cum[g * hg:(g + 1) * hg, :], tq)
            s = jnp.where(causal, s, NEG_BIG)
            vn = nv_ref[0, :, g * gw:(g + 1) * gw]
            online_update(g, [s], lambda p, vn=vn: _dot(p, vn))
            o = acc_s[g] / jnp.sum(l_s[g], axis=1, keepdims=True)
            picked = functools.reduce(
                jnp.add, [jnp.where(lane_grp == r, o[r * tq:(r + 1) * tq, :], 0.0) for r in range(hg)])
            gate = gate_ref[0, :, g * gw:(g + 1) * gw]
            out_ref[0, :, g * gw:(g + 1) * gw] = (picked * _sigmoid(gate)).astype(out_ref.dtype)


def _sample_attn(q, cache_kt, cache_vt, cache_logf_t, kb, vb, new_logf_t, gate, tk):
    bsz, tq, d = q.shape
    past = cache_kt.shape[2]
    n_groups = d // MXU_N
    one = lambda b, j: (b, 0, 0)
    blk = lambda b, j: (b, 0, j)
    return pl.pallas_call(
        functools.partial(_sample_attn_kernel, tk=tk),
        grid=(bsz, past // tk),
        in_specs=[pl.BlockSpec((1, tq, d), one), pl.BlockSpec((1, d, tk), blk), pl.BlockSpec((1, d, tk), blk),
                  pl.BlockSpec((1, B_HEADS, past), one),
                  pl.BlockSpec((1, tq, d), one), pl.BlockSpec((1, tq, d), one),
                  pl.BlockSpec((1, B_HEADS, tq), one), pl.BlockSpec((1, tq, d), one)],
        out_specs=pl.BlockSpec((1, tq, d), one),
        out_shape=jax.ShapeDtypeStruct((bsz, tq, d), BF16),
        scratch_shapes=[pltpu.VMEM((past // tk, B_HEADS, tk), F32),
                        pltpu.VMEM((n_groups, MXU_N // B_HD * tq, LANES), F32),
                        pltpu.VMEM((n_groups, MXU_N // B_HD * tq, LANES), F32),
                        pltpu.VMEM((n_groups, MXU_N // B_HD * tq, MXU_N), F32)],
        compiler_params=_params("parallel", "arbitrary"),
        name="sample_attn",
    )(q, cache_kt, cache_vt, cache_logf_t, kb, vb, new_logf_t, gate)


def _pad_cols(w, n):
    return jnp.pad(w, ((0, 0), (0, n - w.shape[1])))


def _prep_weights(g_mix, g_ffn, w_in_a, b_i_a, b_f_a, g_head_a, w_out_a, g_kv, w_kvf, b_f_b, g_k,
                  w_qg_b, g_q_b, w_o_b, w_gu, w_down):
    d = g_mix.shape[1]
    main_w = 2 * A_HEADS * A_DQK + 2 * A_HEADS * A_DV
    w_in = w_in_a[0]
    wg = _pad_cols(w_in[:, main_w:], LANES)
    wg_hi, wg_lo = _split2(wg)
    wf = _pad_cols(w_kvf[:, 2 * d:], LANES)
    wf_hi, wf_lo = _split2(wf)
    gates_b = jnp.concatenate([b_i_a[0], b_f_a[0]])
    d_ff = w_down.shape[1]
    nch = d_ff // FF_CHUNK

    def ffn(layer):
        w = w_gu[layer].astype(BF16)
        wgate = w[:, :d_ff].reshape(d, nch, FF_CHUNK).transpose(1, 0, 2)
        wup = w[:, d_ff:].reshape(d, nch, FF_CHUNK).transpose(1, 0, 2)
        wdn = w_down[layer].astype(BF16).reshape(nch, FF_CHUNK, d)
        return wgate, wup, wdn

    return dict(
        g_mix0=g_mix[0][None], g_mix1=g_mix[1][None], g_ffn0=g_ffn[0][None], g_ffn1=g_ffn[1][None],
        w_in=w_in[:, :main_w].astype(BF16), wg_hi=wg_hi, wg_lo=wg_lo,
        gb_col=gates_b[None, :], gb_row=gates_b[:, None], g_head=g_head_a[0][None],
        w_out=w_out_a[0].astype(BF16),
        g_kv=g_kv[None], w_kv=w_kvf[:, :2 * d].astype(BF16), wf_hi=wf_hi, wf_lo=wf_lo, b_f=b_f_b[None],
        gk_t=jnp.tile(g_k, B_HEADS)[None], gq_t=jnp.tile(g_q_b[0], B_HEADS)[None],
        w_qg=w_qg_b[0].astype(BF16), w_o=w_o_b[0].astype(BF16),
        ffn0=ffn(0), ffn1=ffn(1),
    )


def _state_to_pairs(c, n, m):
    bsz = c.shape[0]
    np_ = A_HEADS // 2
    ct = c.reshape(bsz, np_, 2, A_DV, A_DQK).transpose(0, 1, 2, 4, 3).reshape(bsz, np_, 2 * A_DQK, A_DV)
    n2 = n.reshape(bsz, np_, 1, 2 * A_DQK)
    m2 = jnp.broadcast_to(m[:, :, None, None], (bsz, A_HEADS, 1, LANES))
    return ct.astype(F32), n2.astype(F32), m2.astype(F32)


def _state_from_pairs(ct, n2, m2):
    bsz = ct.shape[0]
    np_ = A_HEADS // 2
    c = ct.reshape(bsz, np_, 2, A_DQK, A_DV).transpose(0, 1, 2, 4, 3).reshape(bsz, 1, A_HEADS, A_DV, A_DQK)
    n = n2.reshape(bsz, 1, A_HEADS, A_DQK)
    m = m2[:, :, 0, 0].reshape(bsz, 1, A_HEADS)
    return c, n, m


def _layer_a(x, c0, n0, m0, w, chunk, tm):
    bsz, t, d = x.shape
    x2d = x.reshape(bsz * t, d)
    q, k, v, o, gates = _in_proj(x2d, w["g_mix0"], w["w_in"], w["wg_hi"], w["wg_lo"], tm)
    r3 = lambda z: z.reshape(bsz, t, z.shape[1])
    gates3 = r3(gates)
    ct0, n20, m20 = _state_to_pairs(c0, n0, m0)
    hn, ct, n2, m2 = _mlstm(r3(q), r3(k), r3(v), r3(o), gates3, gates3.transpose(0, 2, 1),
                            w["gb_col"], w["gb_row"], w["g_head"], ct0, n20, m20, chunk)
    x2 = _mix_ffn(hn.reshape(bsz * t, -1), x2d, w["w_out"], w["g_ffn0"], *w["ffn0"], tm)
    return x2, _state_from_pairs(ct, n2, m2)


def _fox_inputs(x2, w, tm):
    return _fox_proj(x2, w["g_kv"], w["g_mix1"], w["w_kv"], w["wf_hi"], w["wf_lo"], w["w_qg"],
                     w["b_f"], w["gk_t"], w["gq_t"], tm)


def _prompt_trunk(x, w):
    bsz, t, d = x.shape
    tm = 512
    zeros = lambda *s: jnp.zeros(s, F32)
    x2, (c, n, m) = _layer_a(x, zeros(bsz, A_HEADS, A_DV, A_DQK), zeros(bsz, A_HEADS, A_DQK),
                             zeros(bsz, A_HEADS), w, min(t, 128), tm)
    k, v, logf, kb, vb, q, gate = _fox_inputs(x2, w, tm)
    r3 = lambda z: z.reshape(bsz, t, z.shape[1])
    logf3 = r3(logf)
    cum_aug = _cum_bias(logf3, min(t, 512))
    attn = _prompt_attn(r3(q), r3(kb), cum_aug, r3(vb), r3(gate), min(t, 512), min(t, 512))
    y = _mix_ffn(attn.reshape(bsz * t, d), x2, w["w_o"], w["g_ffn1"], *w["ffn1"], tm)
    return (y.reshape(bsz, t, d), k.reshape(bsz, t, B_HEADS, B_HD), v.reshape(bsz, t, B_HEADS, B_HD),
            logf3, c, n, m)


def _sample_trunk(x, c0, n0, m0, cache_k, cache_v, cache_logf, w):
    bsz, t, d = x.shape
    past = cache_k.shape[1]
    tm = min(bsz * t, 512)
    x2, (c, n, m) = _layer_a(x, c0[:, 0], n0[:, 0], m0[:, 0], w, t, tm)
    k, v, logf, kb, vb, q, gate = _fox_inputs(x2, w, tm)
    r3 = lambda z: z.reshape(bsz, t, z.shape[1])
    logf3 = r3(logf)
    kt = cache_k.transpose(0, 2, 3, 1).reshape(bsz, d, past)
    vt = cache_v.transpose(0, 2, 3, 1).reshape(bsz, d, past)
    attn = _sample_attn(r3(q), kt, vt, cache_logf.transpose(0, 2, 1), r3(kb), r3(vb),
                        logf3.transpose(0, 2, 1), r3(gate), min(past, 1024))
    y = _mix_ffn(attn.reshape(bsz * t, d), x2, w["w_o"], w["g_ffn1"], *w["ffn1"], tm)
    return (y.reshape(bsz, t, d), k.reshape(bsz, t, B_HEADS, B_HD), v.reshape(bsz, t, B_HEADS, B_HD),
            logf3, c, n, m)


def kernel(x_prompt, x_sample, cache_k, cache_v, cache_logf, state_c, state_n, state_m, g_mix, g_ffn, w_in_a, b_i_a, b_f_a, g_head_a, w_out_a, g_kv, w_kvf, b_f_b, g_k, w_qg_b, g_q_b, w_o_b, w_gu, w_down):
    w = _prep_weights(g_mix, g_ffn, w_in_a, b_i_a, b_f_a, g_head_a, w_out_a, g_kv, w_kvf, b_f_b, g_k,
                      w_qg_b, g_q_b, w_o_b, w_gu, w_down)
    y_p, k_p, v_p, f_p, c_p, n_p, m_p = _prompt_trunk(x_prompt, w)
    y_s, k_s, v_s, f_s, c_s, n_s, m_s = _sample_trunk(x_sample, state_c, state_n, state_m,
                                                      cache_k, cache_v, cache_logf, w)
    return (y_p, y_s, k_p, v_p, f_p, c_p, n_p, m_p, k_s, v_s, f_s, c_s, n_s, m_s)
```

```python
import functools

import jax
import jax.numpy as jnp
from jax import lax
from jax.experimental import pallas as pl
from jax.experimental.pallas import tpu as pltpu

F32 = jnp.float32
BF16 = jnp.bfloat16

EPS = 1e-6
GATE_CAP = 15.0
A_HEADS = 8
A_DQK = 64
A_DV = 128
B_HEADS = 16
B_HD = 64
LANES = 128
HALF = 64
MXU_N = 256
FF_CHUNK = MXU_N
NEG_BIG = -1e30
LOG2E = 1.4426950408889634
N_BIAS_PIECES = 3
VMEM_LIMIT = 56 * 1024 * 1024


def _params(*sem):
    return pltpu.CompilerParams(dimension_semantics=sem, vmem_limit_bytes=VMEM_LIMIT)


def _resident(shape):
    nd = len(shape)
    return pl.BlockSpec(shape, lambda *_: (0,) * nd, pipeline_mode=pl.Buffered(1))


def _dot(a, b):
    return jnp.dot(a, b, preferred_element_type=F32)


def _dot_nt(a, b):
    return lax.dot_general(a, b, (((1,), (1,)), ((), ())), preferred_element_type=F32)


def _split2(x):
    hi = x.astype(BF16)
    lo = (x - hi.astype(F32)).astype(BF16)
    return hi, lo


def _split3(x):
    hi = x.astype(BF16)
    r = x - hi.astype(F32)
    mid = r.astype(BF16)
    lo = (r - mid.astype(F32)).astype(BF16)
    return hi, mid, lo


def _dot3(xh, xl, wh, wl):
    return _dot(xh, wh) + _dot(xh, wl) + _dot(xl, wh)


def _log_sigmoid(x):
    return jnp.minimum(x, 0.0) - jnp.log(1.0 + jnp.exp(-jnp.abs(x)))


def _sigmoid(x):
    return 1.0 / (1.0 + jnp.exp(-x))


def _soft_cap(z):
    return GATE_CAP * jnp.tanh(z / GATE_CAP)


def _rms_hat(x):
    return x * lax.rsqrt(jnp.mean(x * x, axis=-1, keepdims=True) + EPS)


def _pair_select(lo_mask, a, b):
    return jnp.where(lo_mask, a, b)


def _head_norm_group(xc, gc, lo_mask):
    sq = xc * xc
    s_lo = jnp.sum(jnp.where(lo_mask, sq, 0.0), axis=-1, keepdims=True)
    s_hi = jnp.sum(jnp.where(lo_mask, 0.0, sq), axis=-1, keepdims=True)
    ms = jnp.where(lo_mask, s_lo, s_hi) * (1.0 / B_HD)
    return xc * lax.rsqrt(ms + EPS) * gc


def _in_proj_kernel(x_ref, g_ref, w_ref, wgh_ref, wgl_ref, q_ref, k_ref, v_ref, o_ref, gates_ref):
    xn = _rms_hat(x_ref[...]) * g_ref[...]
    xh, xl = _split2(xn)
    qk_w = A_HEADS * A_DQK
    v_w = A_HEADS * A_DV
    q_ref[...] = _dot(xh, w_ref[:, 0:qk_w]).astype(BF16)
    k_ref[...] = (_dot(xh, w_ref[:, qk_w:2 * qk_w]) * (A_DQK ** -0.5)).astype(BF16)
    v_ref[...] = _dot(xh, w_ref[:, 2 * qk_w:2 * qk_w + v_w]).astype(BF16)
    o_ref[...] = _dot(xh, w_ref[:, 2 * qk_w + v_w:2 * qk_w + 2 * v_w])
    gz = _dot3(xh, xl, wgh_ref[...], wgl_ref[...])
    gates_ref[...] = gz[:, 0:2 * A_HEADS]


def _in_proj(x2d, g, w_main, wg_hi, wg_lo, tm):
    m, d = x2d.shape
    qk_w = A_HEADS * A_DQK
    v_w = A_HEADS * A_DV
    row = lambda i: (i, 0)
    return pl.pallas_call(
        _in_proj_kernel,
        grid=(m // tm,),
        in_specs=[pl.BlockSpec((tm, d), row), _resident(g.shape), _resident(w_main.shape),
                  _resident(wg_hi.shape), _resident(wg_lo.shape)],
        out_specs=[pl.BlockSpec((tm, qk_w), row), pl.BlockSpec((tm, qk_w), row),
                   pl.BlockSpec((tm, v_w), row), pl.BlockSpec((tm, v_w), row),
                   pl.BlockSpec((tm, 2 * A_HEADS), row)],
        out_shape=[jax.ShapeDtypeStruct((m, qk_w), BF16), jax.ShapeDtypeStruct((m, qk_w), BF16),
                   jax.ShapeDtypeStruct((m, v_w), BF16), jax.ShapeDtypeStruct((m, v_w), F32),
                   jax.ShapeDtypeStruct((m, 2 * A_HEADS), F32)],
        compiler_params=_params("parallel"),
        name="in_proj",
    )(x2d, g, w_main, wg_hi, wg_lo)


def _mlstm_kernel(q_ref, k_ref, v_ref, o_ref, gc_ref, gr_ref, bc_ref, br_ref, gh_ref,
                  c0_ref, n0_ref, m0_ref,
                  h_ref, c_out, n_out, m_out, ct_s, n_s, m_s, *, chunk):
    j = pl.program_id(1)
    L = chunk

    @pl.when(j == 0)
    def _():
        ct_s[...] = c0_ref[0]
        n_s[...] = n0_ref[0]
        m_s[...] = m0_ref[0]

    cap_c = _soft_cap(gc_ref[0] + bc_ref[...])
    cap_r = _soft_cap(gr_ref[0] + br_ref[...])
    li_c = cap_c[:, 0:A_HEADS]
    lf_c = _log_sigmoid(cap_c[:, A_HEADS:2 * A_HEADS])
    li_r = cap_r[0:A_HEADS, :]
    lf_r = _log_sigmoid(cap_r[A_HEADS:2 * A_HEADS, :])

    t_idx = lax.broadcasted_iota(jnp.int32, (L, L), 0)
    s_idx = lax.broadcasted_iota(jnp.int32, (L, L), 1)
    causal = s_idx <= t_idx
    lane = lax.broadcasted_iota(jnp.int32, (1, LANES), 1)
    lo_lane = lane < HALF
    eye = (lax.broadcasted_iota(jnp.int32, (LANES, LANES), 0)
           == lax.broadcasted_iota(jnp.int32, (LANES, LANES), 1)).astype(BF16)

    def rep(col):
        return jnp.broadcast_to(col, (col.shape[0], LANES))

    place = (lax.broadcasted_iota(jnp.int32, (2 * A_HEADS, 2 * A_HEADS * LANES), 0)
             == lax.broadcasted_iota(jnp.int32, (2 * A_HEADS, 2 * A_HEADS * LANES), 1) // LANES).astype(BF16)
    cols_rep = sum(_dot(piece, place) for piece in _split3(jnp.concatenate([li_c, lf_c], axis=1)))
    lo_row = lax.broadcasted_iota(jnp.int32, (LANES, LANES), 0) < HALF

    def head_stages(h, q2, k2, kt2, ct2b, qn, done):
        half = lo_lane if h % 2 == 0 else jnp.logical_not(lo_lane)
        li_col = cols_rep[:, h * LANES:(h + 1) * LANES]
        lf_col = cols_rep[:, (A_HEADS + h) * LANES:(A_HEADS + h + 1) * LANES]
        li_row = li_r[h:h + 1, :]
        lf_row = lf_r[h:h + 1, :]
        m_prev = m_s[h]
        v_h = v_ref[0, :, h * A_DV:(h + 1) * A_DV]
        qm = jnp.where(half, q2, jnp.zeros_like(q2))
        qk = _dot_nt(qm, k2)
        qc = _dot(qm, ct2b)
        qn_sum = rep(jnp.sum(jnp.where(half, qn, 0.0), axis=1, keepdims=True))
        b_col = rep(jnp.sum(jnp.where(causal, lf_row, 0.0), axis=1, keepdims=True))
        b_row = jnp.sum(jnp.where(t_idx <= s_idx, lf_col[:, :L], 0.0), axis=0, keepdims=True)
        b_last = rep(jnp.sum(lf_row, axis=1, keepdims=True))
        yield
        dmat = jnp.where(causal, b_col[:, :L] - b_row + li_row, NEG_BIG)
        inter = b_col + m_prev
        m_t = jnp.maximum(inter, rep(jnp.max(dmat, axis=1, keepdims=True)))
        w_col = b_last - b_col + li_col
        m_new = jnp.maximum(b_last + m_prev, jnp.max(w_col, axis=0, keepdims=True))
        yield
        g = jnp.exp(inter - m_t)
        dec = jnp.where(causal, jnp.exp(dmat - m_t[:, :L]), 0.0)
        floor = jnp.exp(-m_t)
        g_c = jnp.exp(b_last + m_prev - m_new)
        ws = jnp.exp(w_col - m_new)
        yield
        s = qk * dec
        wsv = (ws * v_h.astype(F32)).astype(BF16)
        upd = _dot(kt2, wsv)
        num = g * qc + _dot(s.astype(BF16), v_h)
        den = g * qn_sum + rep(jnp.sum(s, axis=1, keepdims=True))
        yield
        hv = num / jnp.maximum(jnp.abs(den), floor)
        ms = rep(jnp.mean(hv * hv, axis=-1, keepdims=True))
        yield
        hv = hv * lax.rsqrt(ms + EPS)
        hv = hv * gh_ref[:, h * A_DV:(h + 1) * A_DV]
        hv = hv * _sigmoid(o_ref[0, :, h * A_DV:(h + 1) * A_DV])
        h_ref[0, :, h * A_DV:(h + 1) * A_DV] = hv.astype(h_ref.dtype)
        m_s[h] = m_new
        done[h] = (upd, g_c, ws)

    pairs = []
    done = {}
    gens = []
    for p in range(A_HEADS // 2):
        q2 = q_ref[0, :, p * LANES:(p + 1) * LANES]
        k2 = k_ref[0, :, p * LANES:(p + 1) * LANES]
        kt2 = _dot_nt(eye, k2).astype(BF16)
        ct2 = ct_s[p]
        n2 = n_s[p]
        pairs.append((k2, ct2, n2))
        for hh in range(2):
            gens.append(head_stages(2 * p + hh, q2, k2, kt2, ct2.astype(BF16), q2.astype(F32) * n2, done))
    while gens:
        alive = []
        for gen in gens:
            try:
                next(gen)
                alive.append(gen)
            except StopIteration:
                pass
        gens = alive

    for p, (k2, ct2, n2) in enumerate(pairs):
        (upd0, gc0, ws0), (upd1, gc1, ws1) = done[2 * p], done[2 * p + 1]
        ct_s[p] = jnp.where(lo_row, gc0, gc1) * ct2 + jnp.where(lo_row, upd0, upd1)
        ws2 = jnp.where(lo_lane, ws0, ws1)
        n_s[p] = (jnp.where(lo_lane, gc0, gc1) * n2
                  + jnp.sum(ws2 * k2.astype(F32), axis=0, keepdims=True))

    @pl.when(j == pl.num_programs(1) - 1)
    def _():
        c_out[0] = ct_s[...]
        n_out[0] = n_s[...]
        m_out[0] = m_s[...]


def _mlstm(q, k, v, o, gates, gates_t, b_col, b_row, g_head, ct0, n0, m0, chunk):
    bsz, t, _ = q.shape
    nc = t // chunk
    np_ = A_HEADS // 2
    tok = lambda b, j: (b, j, 0)
    st4 = lambda b, j: (b, 0, 0, 0)
    return pl.pallas_call(
        functools.partial(_mlstm_kernel, chunk=chunk),
        grid=(bsz, nc),
        in_specs=[pl.BlockSpec((1, chunk, q.shape[2]), tok), pl.BlockSpec((1, chunk, k.shape[2]), tok),
                  pl.BlockSpec((1, chunk, v.shape[2]), tok), pl.BlockSpec((1, chunk, o.shape[2]), tok),
                  pl.BlockSpec((1, chunk, 2 * A_HEADS), tok),
                  pl.BlockSpec((1, 2 * A_HEADS, chunk), lambda b, j: (b, 0, j)),
                  _resident(b_col.shape), _resident(b_row.shape), _resident(g_head.shape),
                  pl.BlockSpec((1, np_, LANES, A_DV), st4), pl.BlockSpec((1, np_, 1, LANES), st4),
                  pl.BlockSpec((1, A_HEADS, 1, LANES), st4)],
        out_specs=[pl.BlockSpec((1, chunk, v.shape[2]), tok),
                   pl.BlockSpec((1, np_, LANES, A_DV), st4), pl.BlockSpec((1, np_, 1, LANES), st4),
                   pl.BlockSpec((1, A_HEADS, 1, LANES), st4)],
        out_shape=[jax.ShapeDtypeStruct((bsz, t, v.shape[2]), BF16),
                   jax.ShapeDtypeStruct((bsz, np_, LANES, A_DV), F32),
                   jax.ShapeDtypeStruct((bsz, np_, 1, LANES), F32),
                   jax.ShapeDtypeStruct((bsz, A_HEADS, 1, LANES), F32)],
        scratch_shapes=[pltpu.VMEM((np_, LANES, A_DV), F32), pltpu.VMEM((np_, 1, LANES), F32),
                        pltpu.VMEM((A_HEADS, 1, LANES), F32)],
        compiler_params=_params("parallel", "arbitrary"),
        name="mlstm",
    )(q, k, v, o, gates, gates_t, b_col, b_row, g_head, ct0, n0, m0)


def _mix_ffn_kernel(a_ref, res_ref, wo_ref, g_ref, wg_ref, wu_ref, wd_ref, out_ref, xn_s, acc_s):
    x1 = res_ref[...] + _dot(a_ref[...], wo_ref[...])
    xn_s[...] = (_rms_hat(x1) * g_ref[...]).astype(BF16)
    acc_s[...] = x1

    for c in range(wg_ref.shape[0]):
        xn = xn_s[...]
        gate = _dot(xn, wg_ref[c])
        up = _dot(xn, wu_ref[c])
        act = (gate * _sigmoid(gate) * up).astype(BF16)
        acc_s[...] += _dot(act, wd_ref[c])
    out_ref[...] = acc_s[...]


def _mix_ffn(a, res, w_o, g, wg, wu, wd, tm):
    m, d = res.shape
    row = lambda i: (i, 0)
    return pl.pallas_call(
        _mix_ffn_kernel,
        grid=(m // tm,),
        in_specs=[pl.BlockSpec((tm, a.shape[1]), row), pl.BlockSpec((tm, d), row),
                  _resident(w_o.shape), _resident(g.shape), _resident(wg.shape),
                  _resident(wu.shape), _resident(wd.shape)],
        out_specs=pl.BlockSpec((tm, d), row),
        out_shape=jax.ShapeDtypeStruct((m, d), F32),
        scratch_shapes=[pltpu.VMEM((tm, d), BF16), pltpu.VMEM((tm, d), F32)],
        compiler_params=_params("parallel"),
        name="mix_ffn",
    )(a, res, w_o, g, wg, wu, wd)


def _fox_proj_kernel(x_ref, gkv_ref, gmix_ref, wkv_ref, wfh_ref, wfl_ref, wqg_ref, bf_ref, gk_ref, gq_ref,
                     k_ref, v_ref, logf_ref, kb_ref, vb_ref, q_ref, gate_ref):
    d = x_ref.shape[1]
    xhat = _rms_hat(x_ref[...])
    sh, sl = _split2(xhat * gkv_ref[...])
    hq = (xhat * gmix_ref[...]).astype(BF16)
    lo_lane = lax.broadcasted_iota(jnp.int32, (1, LANES), 1) < HALF

    f = _dot3(sh, sl, wfh_ref[...], wfl_ref[...])[:, 0:B_HEADS]
    logf_ref[...] = _log_sigmoid(f + bf_ref[...])

    for c in range(d // MXU_N):
        kraw = _dot(sh, wkv_ref[:, c * MXU_N:(c + 1) * MXU_N])
        qraw = _dot(hq, wqg_ref[:, c * MXU_N:(c + 1) * MXU_N])
        for u in range(MXU_N // LANES):
            us = slice(u * LANES, (u + 1) * LANES)
            cs = slice(c * MXU_N + u * LANES, c * MXU_N + (u + 1) * LANES)
            kn = _head_norm_group(kraw[:, us], gk_ref[:, cs], lo_lane)
            if k_ref.ndim == 3:
                k_ref[0, cs, :] = kn.T
            else:
                k_ref[:, cs] = kn
            kb_ref[:, cs] = kn.astype(BF16)
            qn = _head_norm_group(qraw[:, us], gq_ref[:, cs], lo_lane)
            q_ref[:, cs] = (qn * (LOG2E * B_HD ** -0.5)).astype(BF16)
    for c in range(d // MXU_N):
        cs = slice(c * MXU_N, (c + 1) * MXU_N)
        v = _dot(sh, wkv_ref[:, d + c * MXU_N:d + (c + 1) * MXU_N])
        if v_ref.ndim == 3:
            v_ref[0, cs, :] = v.T
        else:
            v_ref[:, cs] = v
        vb_ref[:, cs] = v.astype(BF16)
    gate_ref[...] = _dot(hq, wqg_ref[:, d:2 * d])


def _fox_proj(x2d, g_kv, g_mix, w_kv, wf_hi, wf_lo, w_qg, b_f, gk_t, gq_t, tm, seq_t=None):
    m, d = x2d.shape
    row = lambda i: (i, 0)
    big = pl.BlockSpec((tm, d), row)
    if seq_t is None:
        kv_spec, kv_shape = big, jax.ShapeDtypeStruct((m, d), F32)
    else:
        per_seq = seq_t // tm
        kv_spec = pl.BlockSpec((1, d, tm), lambda i: (i // per_seq, 0, i % per_seq))
        kv_shape = jax.ShapeDtypeStruct((m // seq_t, d, seq_t), F32)
    return pl.pallas_call(
        _fox_proj_kernel,
        grid=(m // tm,),
        in_specs=[big] + [_resident(a.shape) for a in (g_kv, g_mix, w_kv, wf_hi, wf_lo, w_qg, b_f, gk_t, gq_t)],
        out_specs=[kv_spec, kv_spec, pl.BlockSpec((tm, B_HEADS), row), big, big, big, big],
        out_shape=[kv_shape, kv_shape,
                   jax.ShapeDtypeStruct((m, B_HEADS), F32),
                   jax.ShapeDtypeStruct((m, d), BF16), jax.ShapeDtypeStruct((m, d), BF16),
                   jax.ShapeDtypeStruct((m, d), BF16), jax.ShapeDtypeStruct((m, d), F32)],
        compiler_params=_params("parallel"),
        name="fox_proj",
    )(x2d, g_kv, g_mix, w_kv, wf_hi, wf_lo, w_qg, b_f, gk_t, gq_t)


def _cum_bias_kernel(x_ref, aug_ref, carry_s):
    @pl.when(pl.program_id(1) == 0)
    def _():
        carry_s[...] = jnp.zeros(carry_s.shape, F32)

    tb = x_ref.shape[1]
    tri = (lax.broadcasted_iota(jnp.int32, (tb, tb), 1)
           <= lax.broadcasted_iota(jnp.int32, (tb, tb), 0)).astype(BF16)
    hi, mid, lo = _split3(x_ref[0])
    cum = _dot(tri, hi) + _dot(tri, mid) + _dot(tri, lo) + carry_s[...]
    carry_s[...] = cum[tb - 1:tb, :]

    h_idx = lax.broadcasted_iota(jnp.int32, (B_HEADS, aug_ref.shape[2]), 0)
    c_idx = lax.broadcasted_iota(jnp.int32, (B_HEADS, aug_ref.shape[2]), 1)
    base = LANES * (h_idx // 2) + N_BIAS_PIECES * (h_idx % 2)
    aug = None
    for i, piece in enumerate(_split3(cum * (-LOG2E))):
        term = _dot(piece, (c_idx == base + i).astype(BF16))
        aug = term if aug is None else aug + term
    aug_ref[0] = aug.astype(BF16)


def _cum_bias(logf, tb):
    bsz, t, h = logf.shape
    width = LANES * (h // 2)
    return pl.pallas_call(
        _cum_bias_kernel,
        grid=(bsz, t // tb),
        in_specs=[pl.BlockSpec((1, tb, h), lambda b, j: (b, j, 0))],
        out_specs=pl.BlockSpec((1, tb, width), lambda b, j: (b, j, 0)),
        out_shape=jax.ShapeDtypeStruct((bsz, t, width), BF16),
        scratch_shapes=[pltpu.VMEM((1, h), F32)],
        compiler_params=_params("parallel", "arbitrary"),
        name="cum_bias",
    )(logf)


def _attn_head_scores(h, q2, k2, ca2, mask, m_s, lo_lane):
    hh = h % 2
    tq = q2.shape[0]
    half = lo_lane if hh == 0 else jnp.logical_not(lo_lane)
    qm = jnp.where(half, q2, jnp.zeros_like(q2))
    lane2 = lax.broadcasted_iota(jnp.int32, (tq, LANES), 1)
    pick = jnp.logical_and(lane2 >= N_BIAS_PIECES * hh, lane2 < N_BIAS_PIECES * (hh + 1))
    s = _dot_nt(jnp.concatenate([qm, pick.astype(BF16)], axis=1),
                jnp.concatenate([k2, ca2], axis=1))
    cols = [s[:, c * LANES:(c + 1) * LANES] for c in range(s.shape[1] // LANES)]
    if mask is not None:
        cols = [jnp.where(mask[:, c * LANES:(c + 1) * LANES], col, NEG_BIG) for c, col in enumerate(cols)]
    top = functools.reduce(jnp.maximum, cols)
    m_old = m_s[h]
    m_new = jnp.maximum(m_old, jnp.max(top, axis=1, keepdims=True))
    return cols, m_old, m_new


def _attn_head_update(h, cols, m_old, m_new, v2, m_s, acc_s, lo_lane):
    half = lo_lane if h % 2 == 0 else jnp.logical_not(lo_lane)
    alpha = jnp.exp2(m_old - m_new)
    pr = jnp.concatenate([jnp.exp2(col - m_new).astype(BF16) for col in cols], axis=1)
    v_ones = jnp.where(half, v2, jnp.ones_like(v2))
    acc_s[h] = alpha * acc_s[h] + _dot(pr, v_ones)
    m_s[h] = m_new


def _attn_init(m_s, acc_s):
    m_s[...] = jnp.full(m_s.shape, NEG_BIG, F32)
    acc_s[...] = jnp.zeros(acc_s.shape, F32)


def _attn_finish(out_ref, gate_ref, acc_s, lo_lane):
    for p in range(B_HEADS // 2):
        a0, a1 = acc_s[2 * p], acc_s[2 * p + 1]
        num = jnp.where(lo_lane, a0, a1)
        den = pltpu.roll(jnp.where(lo_lane, a1, a0), HALF, axis=1)
        gate = gate_ref[0, :, p * LANES:(p + 1) * LANES]
        out_ref[0, :, p * LANES:(p + 1) * LANES] = (num / den * _sigmoid(gate)).astype(out_ref.dtype)


def _prompt_attn_kernel(q_ref, k_ref, ca_ref, v_ref, gate_ref, out_ref, m_s, acc_s, *, tq, tk):
    i = pl.program_id(1)
    j = pl.program_id(2)
    lo_lane = lax.broadcasted_iota(jnp.int32, (1, LANES), 1) < HALF

    @pl.when(j == 0)
    def _():
        _attn_init(m_s, acc_s)

    def run(masked):
        mask = None
        if masked:
            t_pos = i * tq + lax.broadcasted_iota(jnp.int32, (tq, tk), 0)
            s_pos = j * tk + lax.broadcasted_iota(jnp.int32, (tq, tk), 1)
            mask = s_pos <= t_pos
        def scores(h):
            cs = slice((h // 2) * LANES, (h // 2 + 1) * LANES)
            return _attn_head_scores(h, q_ref[0, :, cs], k_ref[0, :, cs], ca_ref[0, :, cs],
                                     mask, m_s, lo_lane)

        pending = scores(0)
        for h in range(B_HEADS):
            upcoming = scores(h + 1) if h + 1 < B_HEADS else None
            cs = slice((h // 2) * LANES, (h // 2 + 1) * LANES)
            _attn_head_update(h, *pending, v_ref[0, :, cs], m_s, acc_s, lo_lane)
            pending = upcoming

    last = ((i + 1) * tq - 1) // tk
    first_diag = (i * tq) // tk

    @pl.when(j < first_diag)
    def _():
        run(False)

    @pl.when(jnp.logical_and(j >= first_diag, j <= last))
    def _():
        run(True)

    @pl.when(j == last)
    def _():
        _attn_finish(out_ref, gate_ref, acc_s, lo_lane)


def _prompt_attn(q, kb, cum_aug, vb, gate, tq, tk):
    bsz, t, d = q.shape
    nq, nk = t // tq, t // tk
    last = lambda i: ((i + 1) * tq - 1) // tk
    qmap = lambda b, i, j: (b, i, 0)
    kvmap = lambda b, i, j: (b, jnp.minimum(j, last(i)), 0)
    return pl.pallas_call(
        functools.partial(_prompt_attn_kernel, tq=tq, tk=tk),
        grid=(bsz, nq, nk),
        in_specs=[pl.BlockSpec((1, tq, d), qmap), pl.BlockSpec((1, tk, d), kvmap),
                  pl.BlockSpec((1, tk, d), kvmap), pl.BlockSpec((1, tk, d), kvmap),
                  pl.BlockSpec((1, tq, d), qmap)],
        out_specs=pl.BlockSpec((1, tq, d), qmap),
        out_shape=jax.ShapeDtypeStruct((bsz, t, d), BF16),
        scratch_shapes=[pltpu.VMEM((B_HEADS, tq, LANES), F32), pltpu.VMEM((B_HEADS, tq, LANES), F32)],
        compiler_params=_params("parallel", "parallel", "arbitrary"),
        name="prompt_attn",
    )(q, kb, cum_aug, vb, gate)


def _group_rows(x, tq):
    return jnp.concatenate([jnp.broadcast_to(x[r:r + 1, :], (tq, x.shape[1])) for r in range(x.shape[0])], axis=0)


def _lane_cumsum(x):
    r, n = x.shape
    nb = n // MXU_N
    stacked = jnp.concatenate([x[:, c * MXU_N:(c + 1) * MXU_N] for c in range(nb)], axis=0)
    upper = (lax.broadcasted_iota(jnp.int32, (MXU_N, MXU_N), 0)
             <= lax.broadcasted_iota(jnp.int32, (MXU_N, MXU_N), 1)).astype(BF16)
    within = sum(_dot(piece, upper) for piece in _split3(stacked))
    totals = jnp.broadcast_to(within[:, MXU_N - 1:MXU_N], (nb * r, LANES))
    ri = lax.broadcasted_iota(jnp.int32, (nb * r, nb * r), 0)
    ci = lax.broadcasted_iota(jnp.int32, (nb * r, nb * r), 1)
    earlier = jnp.logical_and(ci // r < ri // r, ci % r == ri % r).astype(BF16)
    offs = sum(_dot(earlier, piece) for piece in _split3(totals))
    full = within + jnp.concatenate([offs] * (MXU_N // LANES), axis=1)
    return [full[c * r:(c + 1) * r, :] for c in range(nb)]


def _sample_attn_kernel(q_ref, kt_ref, vt_ref, lf_ref, nk_ref, nv_ref, nlf_ref, gate_ref, out_ref,
                        cum_s, m_s, l_s, acc_s, *, tk):
    j = pl.program_id(1)
    tq = q_ref.shape[1]
    n_groups = acc_s.shape[0]
    hg = B_HEADS // n_groups
    gw = hg * B_HD
    rows = hg * tq
    row_head = lax.broadcasted_iota(jnp.int32, (rows, gw), 0) // tq
    lane_head = lax.broadcasted_iota(jnp.int32, (rows, gw), 1) // B_HD
    own = row_head == lane_head

    @pl.when(j == 0)
    def _():
        m_s[...] = jnp.full(m_s.shape, NEG_BIG, F32)
        l_s[...] = jnp.zeros(l_s.shape, F32)
        acc_s[...] = jnp.zeros(acc_s.shape, F32)
        blocks = _lane_cumsum(lf_ref[0])
        per_step = tk // MXU_N
        for c, blk in enumerate(blocks):
            cum_s[c // per_step, :, (c % per_step) * MXU_N:(c % per_step + 1) * MXU_N] = blk * LOG2E

    def q_block_diag(g):
        qg = q_ref[0, :, g * gw:(g + 1) * gw]
        qrep = jnp.concatenate([qg] * hg, axis=0)
        return jnp.where(own, qrep, jnp.zeros_like(qrep))

    def online_update(g, cols, pv_fn):
        top = functools.reduce(jnp.maximum, cols)
        m_old = m_s[g]
        m_new = jnp.maximum(m_old, jnp.max(top, axis=1, keepdims=True))
        alpha = jnp.exp2(m_old - m_new)
        probs = [jnp.exp2(col - m_new[:, 0:col.shape[1]]) for col in cols]
        if cols[0].shape[1] == LANES:
            psum = functools.reduce(jnp.add, probs)
        else:
            lane0 = lax.broadcasted_iota(jnp.int32, (rows, LANES), 1) == 0
            psum = jnp.where(lane0, jnp.sum(probs[0], axis=1, keepdims=True), 0.0)
        l_s[g] = alpha * l_s[g] + psum
        p = probs[0] if len(probs) == 1 else jnp.concatenate(probs, axis=1)
        acc_s[g] = jnp.concatenate([alpha] * (gw // LANES), axis=1) * acc_s[g] + pv_fn(p.astype(BF16))
        m_s[g] = m_new

    cum_blk = cum_s[j]
    for g in range(n_groups):
        kt = kt_ref[0, g * gw:(g + 1) * gw, :].astype(BF16)
        s = _dot(q_block_diag(g), kt) - _group_rows(cum_blk[g * hg:(g + 1) * hg, :], tq)
        cols = [s[:, c * LANES:(c + 1) * LANES] for c in range(tk // LANES)]
        vt = vt_ref[0, g * gw:(g + 1) * gw, :].astype(BF16)
        online_update(g, cols, lambda p, vt=vt: _dot_nt(p, vt))

    @pl.when(j == pl.num_programs(1) - 1)
    def _():
        t_q = lax.broadcasted_iota(jnp.int32, (rows, tq), 0) % tq
        t_k = lax.broadcasted_iota(jnp.int32, (rows, tq), 1)
        causal = t_k <= t_q
        upper = (lax.broadcasted_iota(jnp.int32, (tq, tq), 0)
                 <= lax.broadcasted_iota(jnp.int32, (tq, tq), 1)).astype(BF16)
        new_cum = sum(_dot(piece, upper) for piece in _split3(nlf_ref[0]))
        last = cum_s[pl.num_programs(1) - 1][:, tk - 1:tk]
        new_cum = new_cum * LOG2E + last
        lane_grp = lax.broadcasted_iota(jnp.int32, (tq, gw), 1) // B_HD
        for g in range(n_groups):
            kn = nk_ref[0, :, g * gw:(g + 1) * gw]
            s = _dot_nt(q_block_diag(g), kn) - _group_rows(new_cum[g * hg:(g + 1) * hg, :], tq)
            s = jnp.where(causal, s, NEG_BIG)
            vn = nv_ref[0, :, g * gw:(g + 1) * gw]
            online_update(g, [s], lambda p, vn=vn: _dot(p, vn))
            o = acc_s[g] / jnp.sum(l_s[g], axis=1, keepdims=True)
            picked = functools.reduce(
                jnp.add, [jnp.where(lane_grp == r, o[r * tq:(r + 1) * tq, :], 0.0) for r in range(hg)])
            gate = gate_ref[0, :, g * gw:(g + 1) * gw]
            out_ref[0, :, g * gw:(g + 1) * gw] = (picked * _sigmoid(gate)).astype(out_ref.dtype)


def _sample_attn(q, cache_kt, cache_vt, cache_logf_t, kb, vb, new_logf_t, gate, tk):
    bsz, tq, d = q.shape
    past = cache_kt.shape[2]
    n_groups = d // MXU_N
    one = lambda b, j: (b, 0, 0)
    blk = lambda b, j: (b, 0, j)
    return pl.pallas_call(
        functools.partial(_sample_attn_kernel, tk=tk),
        grid=(bsz, past // tk),
        in_specs=[pl.BlockSpec((1, tq, d), one), pl.BlockSpec((1, d, tk), blk), pl.BlockSpec((1, d, tk), blk),
                  pl.BlockSpec((1, B_HEADS, past), one),
                  pl.BlockSpec((1, tq, d), one), pl.BlockSpec((1, tq, d), one),
                  pl.BlockSpec((1, B_HEADS, tq), one), pl.BlockSpec((1, tq, d), one)],
        out_specs=pl.BlockSpec((1, tq, d), one),
        out_shape=jax.ShapeDtypeStruct((bsz, tq, d), BF16),
        scratch_shapes=[pltpu.VMEM((past // tk, B_HEADS, tk), F32),
                        pltpu.VMEM((n_groups, MXU_N // B_HD * tq, LANES), F32),
                        pltpu.VMEM((n_groups, MXU_N // B_HD * tq, LANES), F32),
                        pltpu.VMEM((n_groups, MXU_N // B_HD * tq, MXU_N), F32)],
        compiler_params=_params("parallel", "arbitrary"),
        name="sample_attn",
    )(q, cache_kt, cache_vt, cache_logf_t, kb, vb, new_logf_t, gate)


def _pad_cols(w, n):
    return jnp.pad(w, ((0, 0), (0, n - w.shape[1])))


def _prep_weights(g_mix, g_ffn, w_in_a, b_i_a, b_f_a, g_head_a, w_out_a, g_kv, w_kvf, b_f_b, g_k,
                  w_qg_b, g_q_b, w_o_b, w_gu, w_down):
    d = g_mix.shape[1]
    main_w = 2 * A_HEADS * A_DQK + 2 * A_HEADS * A_DV
    w_in = w_in_a[0]
    wg = _pad_cols(w_in[:, main_w:], LANES)
    wg_hi, wg_lo = _split2(wg)
    wf = _pad_cols(w_kvf[:, 2 * d:], LANES)
    wf_hi, wf_lo = _split2(wf)
    gates_b = jnp.concatenate([b_i_a[0], b_f_a[0]])
    d_ff = w_down.shape[1]
    nch = d_ff // FF_CHUNK

    def ffn(layer):
        w = w_gu[layer].astype(BF16)
        wgate = w[:, :d_ff].reshape(d, nch, FF_CHUNK).transpose(1, 0, 2)
        wup = w[:, d_ff:].reshape(d, nch, FF_CHUNK).transpose(1, 0, 2)
        wdn = w_down[layer].astype(BF16).reshape(nch, FF_CHUNK, d)
        return wgate, wup, wdn

    return dict(
        g_mix0=g_mix[0][None], g_mix1=g_mix[1][None], g_ffn0=g_ffn[0][None], g_ffn1=g_ffn[1][None],
        w_in=w_in[:, :main_w].astype(BF16), wg_hi=wg_hi, wg_lo=wg_lo,
        gb_col=gates_b[None, :], gb_row=gates_b[:, None], g_head=g_head_a[0][None],
        w_out=w_out_a[0].astype(BF16),
        g_kv=g_kv[None], w_kv=w_kvf[:, :2 * d].astype(BF16), wf_hi=wf_hi, wf_lo=wf_lo, b_f=b_f_b[None],
        gk_t=jnp.tile(g_k, B_HEADS)[None], gq_t=jnp.tile(g_q_b[0], B_HEADS)[None],
        w_qg=w_qg_b[0].astype(BF16), w_o=w_o_b[0].astype(BF16),
        ffn0=ffn(0), ffn1=ffn(1),
    )


def _state_to_pairs(c, n, m):
    bsz = c.shape[0]
    np_ = A_HEADS // 2
    ct = c.reshape(bsz, np_, 2, A_DV, A_DQK).transpose(0, 1, 2, 4, 3).reshape(bsz, np_, 2 * A_DQK, A_DV)
    n2 = n.reshape(bsz, np_, 1, 2 * A_DQK)
    m2 = jnp.broadcast_to(m[:, :, None, None], (bsz, A_HEADS, 1, LANES))
    return ct.astype(F32), n2.astype(F32), m2.astype(F32)


def _state_from_pairs(ct, n2, m2):
    bsz = ct.shape[0]
    np_ = A_HEADS // 2
    c = ct.reshape(bsz, np_, 2, A_DQK, A_DV).transpose(0, 1, 2, 4, 3).reshape(bsz, 1, A_HEADS, A_DV, A_DQK)
    n = n2.reshape(bsz, 1, A_HEADS, A_DQK)
    m = m2[:, :, 0, 0].reshape(bsz, 1, A_HEADS)
    return c, n, m


def _layer_a(x, c0, n0, m0, w, chunk, tm):
    bsz, t, d = x.shape
    x2d = x.reshape(bsz * t, d)
    q, k, v, o, gates = _in_proj(x2d, w["g_mix0"], w["w_in"], w["wg_hi"], w["wg_lo"], tm)
    r3 = lambda z: z.reshape(bsz, t, z.shape[1])
    gates3 = r3(gates)
    ct0, n20, m20 = _state_to_pairs(c0, n0, m0)
    hn, ct, n2, m2 = _mlstm(r3(q), r3(k), r3(v), r3(o), gates3, gates3.transpose(0, 2, 1),
                            w["gb_col"], w["gb_row"], w["g_head"], ct0, n20, m20, chunk)
    x2 = _mix_ffn(hn.reshape(bsz * t, -1), x2d, w["w_out"], w["g_ffn0"], *w["ffn0"], tm)
    return x2, _state_from_pairs(ct, n2, m2)


def _fox_inputs(x2, w, tm, seq_t=None):
    return _fox_proj(x2, w["g_kv"], w["g_mix1"], w["w_kv"], w["wf_hi"], w["wf_lo"], w["w_qg"],
                     w["b_f"], w["gk_t"], w["gq_t"], tm, seq_t)


def _prompt_trunk(x, w):
    bsz, t, d = x.shape
    tm = 512
    zeros = lambda *s: jnp.zeros(s, F32)
    x2, (c, n, m) = _layer_a(x, zeros(bsz, A_HEADS, A_DV, A_DQK), zeros(bsz, A_HEADS, A_DQK),
                             zeros(bsz, A_HEADS), w, min(t, 128), tm)
    kt, vt, logf, kb, vb, q, gate = _fox_inputs(x2, w, tm, seq_t=t)
    r3 = lambda z: z.reshape(bsz, t, z.shape[1])
    logf3 = r3(logf)
    cum_aug = _cum_bias(logf3, min(t, 512))
    attn = _prompt_attn(r3(q), r3(kb), cum_aug, r3(vb), r3(gate), min(t, 512), min(t, 512))
    y = _mix_ffn(attn.reshape(bsz * t, d), x2, w["w_o"], w["g_ffn1"], *w["ffn1"], tm)
    to_out = lambda z: z.reshape(bsz, B_HEADS, B_HD, t).transpose(0, 3, 1, 2)
    return (y.reshape(bsz, t, d), to_out(kt), to_out(vt), logf3, c, n, m)


def _sample_trunk(x, c0, n0, m0, cache_k, cache_v, cache_logf, w):
    bsz, t, d = x.shape
    past = cache_k.shape[1]
    tm = min(bsz * t, 512)
    x2, (c, n, m) = _layer_a(x, c0[:, 0], n0[:, 0], m0[:, 0], w, t, tm)
    k, v, logf, kb, vb, q, gate = _fox_inputs(x2, w, tm)
    r3 = lambda z: z.reshape(bsz, t, z.shape[1])
    logf3 = r3(logf)
    kt = cache_k.transpose(0, 2, 3, 1).reshape(bsz, d, past)
    vt = cache_v.transpose(0, 2, 3, 1).reshape(bsz, d, past)
    attn = _sample_attn(r3(q), kt, vt, cache_logf.transpose(0, 2, 1), r3(kb), r3(vb),
                        logf3.transpose(0, 2, 1), r3(gate), min(past, 1024))
    y = _mix_ffn(attn.reshape(bsz * t, d), x2, w["w_o"], w["g_ffn1"], *w["ffn1"], tm)
    return (y.reshape(bsz, t, d), k.reshape(bsz, t, B_HEADS, B_HD), v.reshape(bsz, t, B_HEADS, B_HD),
            logf3, c, n, m)


def kernel(x_prompt, x_sample, cache_k, cache_v, cache_logf, state_c, state_n, state_m, g_mix, g_ffn, w_in_a, b_i_a, b_f_a, g_head_a, w_out_a, g_kv, w_kvf, b_f_b, g_k, w_qg_b, g_q_b, w_o_b, w_gu, w_down):
    w = _prep_weights(g_mix, g_ffn, w_in_a, b_i_a, b_f_a, g_head_a, w_out_a, g_kv, w_kvf, b_f_b, g_k,
                      w_qg_b, g_q_b, w_o_b, w_gu, w_down)
    y_p, k_p, v_p, f_p, c_p, n_p, m_p = _prompt_trunk(x_prompt, w)
    y_s, k_s, v_s, f_s, c_s, n_s, m_s = _sample_trunk(x_sample, state_c, state_n, state_m,
                                                      cache_k, cache_v, cache_logf, w)
    return (y_p, y_s, k_p, v_p, f_p, c_p, n_p, m_p, k_s, v_s, f_s, c_s, n_s, m_s)
```

```python
import functools

import jax
import jax.numpy as jnp
from jax import lax
from jax.experimental import pallas as pl
from jax.experimental.pallas import tpu as pltpu

F32 = jnp.float32
BF16 = jnp.bfloat16

EPS = 1e-6
GATE_CAP = 15.0
A_HEADS = 8
A_DQK = 64
A_DV = 128
B_HEADS = 16
B_HD = 64
LANES = 128
HALF = 64
MXU_N = 256
FF_CHUNK = MXU_N
NEG_BIG = -1e30
LOG2E = 1.4426950408889634
N_BIAS_PIECES = 3
VMEM_LIMIT = 56 * 1024 * 1024


def _params(*sem):
    return pltpu.CompilerParams(dimension_semantics=sem, vmem_limit_bytes=VMEM_LIMIT)


def _resident(shape):
    nd = len(shape)
    return pl.BlockSpec(shape, lambda *_: (0,) * nd, pipeline_mode=pl.Buffered(1))


def _dot(a, b):
    return jnp.dot(a, b, preferred_element_type=F32)


def _dot_nt(a, b):
    return lax.dot_general(a, b, (((1,), (1,)), ((), ())), preferred_element_type=F32)


def _split2(x):
    hi = x.astype(BF16)
    lo = (x - hi.astype(F32)).astype(BF16)
    return hi, lo


def _split3(x):
    hi = x.astype(BF16)
    r = x - hi.astype(F32)
    mid = r.astype(BF16)
    lo = (r - mid.astype(F32)).astype(BF16)
    return hi, mid, lo


def _dot3(xh, xl, wh, wl):
    return _dot(xh, wh) + _dot(xh, wl) + _dot(xl, wh)


def _log_sigmoid(x):
    return jnp.minimum(x, 0.0) - jnp.log(1.0 + jnp.exp(-jnp.abs(x)))


def _sigmoid(x):
    return 1.0 / (1.0 + jnp.exp(-x))


def _soft_cap(z):
    return GATE_CAP * jnp.tanh(z / GATE_CAP)


def _rms_hat(x):
    return x * lax.rsqrt(jnp.mean(x * x, axis=-1, keepdims=True) + EPS)


def _pair_select(lo_mask, a, b):
    return jnp.where(lo_mask, a, b)


def _head_norm_group(xc, gc, lo_mask):
    sq = xc * xc
    s_lo = jnp.sum(jnp.where(lo_mask, sq, 0.0), axis=-1, keepdims=True)
    s_hi = jnp.sum(jnp.where(lo_mask, 0.0, sq), axis=-1, keepdims=True)
    ms = jnp.where(lo_mask, s_lo, s_hi) * (1.0 / B_HD)
    return xc * lax.rsqrt(ms + EPS) * gc


def _in_proj_kernel(x_ref, g_ref, w_ref, wgh_ref, wgl_ref, q_ref, k_ref, v_ref, o_ref, gates_ref):
    xn = _rms_hat(x_ref[...]) * g_ref[...]
    xh, xl = _split2(xn)
    qk_w = A_HEADS * A_DQK
    v_w = A_HEADS * A_DV
    q_ref[...] = _dot(xh, w_ref[:, 0:qk_w]).astype(BF16)
    k_ref[...] = (_dot(xh, w_ref[:, qk_w:2 * qk_w]) * (A_DQK ** -0.5)).astype(BF16)
    v_ref[...] = _dot(xh, w_ref[:, 2 * qk_w:2 * qk_w + v_w]).astype(BF16)
    o_ref[...] = _dot(xh, w_ref[:, 2 * qk_w + v_w:2 * qk_w + 2 * v_w])
    gz = _dot3(xh, xl, wgh_ref[...], wgl_ref[...])
    gates_ref[...] = gz[:, 0:2 * A_HEADS]


def _in_proj(x2d, g, w_main, wg_hi, wg_lo, tm):
    m, d = x2d.shape
    qk_w = A_HEADS * A_DQK
    v_w = A_HEADS * A_DV
    row = lambda i: (i, 0)
    return pl.pallas_call(
        _in_proj_kernel,
        grid=(m // tm,),
        in_specs=[pl.BlockSpec((tm, d), row), _resident(g.shape), _resident(w_main.shape),
                  _resident(wg_hi.shape), _resident(wg_lo.shape)],
        out_specs=[pl.BlockSpec((tm, qk_w), row), pl.BlockSpec((tm, qk_w), row),
                   pl.BlockSpec((tm, v_w), row), pl.BlockSpec((tm, v_w), row),
                   pl.BlockSpec((tm, 2 * A_HEADS), row)],
        out_shape=[jax.ShapeDtypeStruct((m, qk_w), BF16), jax.ShapeDtypeStruct((m, qk_w), BF16),
                   jax.ShapeDtypeStruct((m, v_w), BF16), jax.ShapeDtypeStruct((m, v_w), F32),
                   jax.ShapeDtypeStruct((m, 2 * A_HEADS), F32)],
        compiler_params=_params("parallel"),
        name="in_proj",
    )(x2d, g, w_main, wg_hi, wg_lo)


def _mlstm_kernel(q_ref, k_ref, v_ref, o_ref, gc_ref, gr_ref, bc_ref, br_ref, gh_ref,
                  c0_ref, n0_ref, m0_ref,
                  h_ref, c_out, n_out, m_out, ct_s, n_s, m_s, *, chunk):
    j = pl.program_id(1)
    L = chunk

    @pl.when(j == 0)
    def _():
        ct_s[...] = c0_ref[0]
        n_s[...] = n0_ref[0]
        m_s[...] = m0_ref[0]

    cap_c = _soft_cap(gc_ref[0] + bc_ref[...])
    cap_r = _soft_cap(gr_ref[0] + br_ref[...])
    li_c = cap_c[:, 0:A_HEADS]
    lf_c = _log_sigmoid(cap_c[:, A_HEADS:2 * A_HEADS])
    li_r = cap_r[0:A_HEADS, :]
    lf_r = _log_sigmoid(cap_r[A_HEADS:2 * A_HEADS, :])

    t_idx = lax.broadcasted_iota(jnp.int32, (L, L), 0)
    s_idx = lax.broadcasted_iota(jnp.int32, (L, L), 1)
    causal = s_idx <= t_idx
    lane = lax.broadcasted_iota(jnp.int32, (1, LANES), 1)
    lo_lane = lane < HALF
    eye = (lax.broadcasted_iota(jnp.int32, (LANES, LANES), 0)
           == lax.broadcasted_iota(jnp.int32, (LANES, LANES), 1)).astype(BF16)

    def rep(col):
        return jnp.broadcast_to(col, (col.shape[0], LANES))

    place = (lax.broadcasted_iota(jnp.int32, (2 * A_HEADS, 2 * A_HEADS * LANES), 0)
             == lax.broadcasted_iota(jnp.int32, (2 * A_HEADS, 2 * A_HEADS * LANES), 1) // LANES).astype(BF16)
    cols_rep = sum(_dot(piece, place) for piece in _split3(jnp.concatenate([li_c, lf_c], axis=1)))
    lo_row = lax.broadcasted_iota(jnp.int32, (LANES, LANES), 0) < HALF

    def head_stages(h, q2, k2, kt2, ct2b, qn, done):
        half = lo_lane if h % 2 == 0 else jnp.logical_not(lo_lane)
        li_col = cols_rep[:, h * LANES:(h + 1) * LANES]
        lf_col = cols_rep[:, (A_HEADS + h) * LANES:(A_HEADS + h + 1) * LANES]
        li_row = li_r[h:h + 1, :]
        lf_row = lf_r[h:h + 1, :]
        m_prev = m_s[h]
        v_h = v_ref[0, :, h * A_DV:(h + 1) * A_DV]
        qm = jnp.where(half, q2, jnp.zeros_like(q2))
        qk = _dot_nt(qm, k2)
        qc = _dot(qm, ct2b)
        qn_sum = rep(jnp.sum(jnp.where(half, qn, 0.0), axis=1, keepdims=True))
        b_col = rep(jnp.sum(jnp.where(causal, lf_row, 0.0), axis=1, keepdims=True))
        b_row = jnp.sum(jnp.where(t_idx <= s_idx, lf_col[:, :L], 0.0), axis=0, keepdims=True)
        b_last = rep(jnp.sum(lf_row, axis=1, keepdims=True))
        yield
        dmat = jnp.where(causal, b_col[:, :L] - b_row + li_row, NEG_BIG)
        inter = b_col + m_prev
        m_t = jnp.maximum(inter, rep(jnp.max(dmat, axis=1, keepdims=True)))
        w_col = b_last - b_col + li_col
        m_new = jnp.maximum(b_last + m_prev, jnp.max(w_col, axis=0, keepdims=True))
        yield
        g = jnp.exp(inter - m_t)
        dec = jnp.where(causal, jnp.exp(dmat - m_t[:, :L]), 0.0)
        floor = jnp.exp(-m_t)
        g_c = jnp.exp(b_last + m_prev - m_new)
        ws = jnp.exp(w_col - m_new)
        yield
        s = qk * dec
        wsv = (ws * v_h.astype(F32)).astype(BF16)
        upd = _dot(kt2, wsv)
        num = g * qc + _dot(s.astype(BF16), v_h)
        den = g * qn_sum + rep(jnp.sum(s, axis=1, keepdims=True))
        yield
        hv = num / jnp.maximum(jnp.abs(den), floor)
        ms = rep(jnp.mean(hv * hv, axis=-1, keepdims=True))
        yield
        hv = hv * lax.rsqrt(ms + EPS)
        hv = hv * gh_ref[:, h * A_DV:(h + 1) * A_DV]
        hv = hv * _sigmoid(o_ref[0, :, h * A_DV:(h + 1) * A_DV])
        h_ref[0, :, h * A_DV:(h + 1) * A_DV] = hv.astype(h_ref.dtype)
        m_s[h] = m_new
        done[h] = (upd, g_c, ws)

    pairs = []
    done = {}
    gens = []
    for p in range(A_HEADS // 2):
        q2 = q_ref[0, :, p * LANES:(p + 1) * LANES]
        k2 = k_ref[0, :, p * LANES:(p + 1) * LANES]
        kt2 = _dot_nt(eye, k2).astype(BF16)
        ct2 = ct_s[p]
        n2 = n_s[p]
        pairs.append((k2, ct2, n2))
        for hh in range(2):
            gens.append(head_stages(2 * p + hh, q2, k2, kt2, ct2.astype(BF16), q2.astype(F32) * n2, done))
    while gens:
        alive = []
        for gen in gens:
            try:
                next(gen)
                alive.append(gen)
            except StopIteration:
                pass
        gens = alive

    for p, (k2, ct2, n2) in enumerate(pairs):
        (upd0, gc0, ws0), (upd1, gc1, ws1) = done[2 * p], done[2 * p + 1]
        ct_s[p] = jnp.where(lo_row, gc0, gc1) * ct2 + jnp.where(lo_row, upd0, upd1)
        ws2 = jnp.where(lo_lane, ws0, ws1)
        n_s[p] = (jnp.where(lo_lane, gc0, gc1) * n2
                  + jnp.sum(ws2 * k2.astype(F32), axis=0, keepdims=True))

    @pl.when(j == pl.num_programs(1) - 1)
    def _():
        c_out[0] = ct_s[...]
        n_out[0] = n_s[...]
        m_out[0] = m_s[...]


def _mlstm(q, k, v, o, gates, gates_t, b_col, b_row, g_head, ct0, n0, m0, chunk):
    bsz, t, _ = q.shape
    nc = t // chunk
    np_ = A_HEADS // 2
    tok = lambda b, j: (b, j, 0)
    st4 = lambda b, j: (b, 0, 0, 0)
    return pl.pallas_call(
        functools.partial(_mlstm_kernel, chunk=chunk),
        grid=(bsz, nc),
        in_specs=[pl.BlockSpec((1, chunk, q.shape[2]), tok), pl.BlockSpec((1, chunk, k.shape[2]), tok),
                  pl.BlockSpec((1, chunk, v.shape[2]), tok), pl.BlockSpec((1, chunk, o.shape[2]), tok),
                  pl.BlockSpec((1, chunk, 2 * A_HEADS), tok),
                  pl.BlockSpec((1, 2 * A_HEADS, chunk), lambda b, j: (b, 0, j)),
                  _resident(b_col.shape), _resident(b_row.shape), _resident(g_head.shape),
                  pl.BlockSpec((1, np_, LANES, A_DV), st4), pl.BlockSpec((1, np_, 1, LANES), st4),
                  pl.BlockSpec((1, A_HEADS, 1, LANES), st4)],
        out_specs=[pl.BlockSpec((1, chunk, v.shape[2]), tok),
                   pl.BlockSpec((1, np_, LANES, A_DV), st4), pl.BlockSpec((1, np_, 1, LANES), st4),
                   pl.BlockSpec((1, A_HEADS, 1, LANES), st4)],
        out_shape=[jax.ShapeDtypeStruct((bsz, t, v.shape[2]), BF16),
                   jax.ShapeDtypeStruct((bsz, np_, LANES, A_DV), F32),
                   jax.ShapeDtypeStruct((bsz, np_, 1, LANES), F32),
                   jax.ShapeDtypeStruct((bsz, A_HEADS, 1, LANES), F32)],
        scratch_shapes=[pltpu.VMEM((np_, LANES, A_DV), F32), pltpu.VMEM((np_, 1, LANES), F32),
                        pltpu.VMEM((A_HEADS, 1, LANES), F32)],
        compiler_params=_params("parallel", "arbitrary"),
        name="mlstm",
    )(q, k, v, o, gates, gates_t, b_col, b_row, g_head, ct0, n0, m0)


def _mix_ffn_kernel(a_ref, res_ref, wo_ref, g_ref, wgu_ref, wd_ref, out_ref, xn_s, acc_s):
    x1 = res_ref[...] + _dot(a_ref[...], wo_ref[...])
    xn_s[...] = (_rms_hat(x1) * g_ref[...]).astype(BF16)
    acc_s[...] = x1

    d_ff = wd_ref.shape[0]
    for c in range(d_ff // FF_CHUNK):
        lo, hi = c * FF_CHUNK, (c + 1) * FF_CHUNK
        xn = xn_s[...]
        gate = _dot(xn, wgu_ref[:, lo:hi])
        up = _dot(xn, wgu_ref[:, d_ff + lo:d_ff + hi])
        act = (gate * _sigmoid(gate) * up).astype(BF16)
        acc_s[...] += _dot(act, wd_ref[lo:hi, :])
    out_ref[...] = acc_s[...]


def _mix_ffn(a, res, w_o, g, wgu, wd, tm):
    m, d = res.shape
    row = lambda i: (i, 0)
    return pl.pallas_call(
        _mix_ffn_kernel,
        grid=(m // tm,),
        in_specs=[pl.BlockSpec((tm, a.shape[1]), row), pl.BlockSpec((tm, d), row),
                  _resident(w_o.shape), _resident(g.shape), _resident(wgu.shape), _resident(wd.shape)],
        out_specs=pl.BlockSpec((tm, d), row),
        out_shape=jax.ShapeDtypeStruct((m, d), F32),
        scratch_shapes=[pltpu.VMEM((tm, d), BF16), pltpu.VMEM((tm, d), F32)],
        compiler_params=_params("parallel"),
        name="mix_ffn",
    )(a, res, w_o, g, wgu, wd)


def _fox_proj_kernel(x_ref, gkv_ref, gmix_ref, wkv_ref, wfh_ref, wfl_ref, wqg_ref, bf_ref, gk_ref, gq_ref,
                     k_ref, v_ref, logf_ref, kb_ref, vb_ref, q_ref, gate_ref):
    d = x_ref.shape[1]
    xhat = _rms_hat(x_ref[...])
    sh, sl = _split2(xhat * gkv_ref[...])
    hq = (xhat * gmix_ref[...]).astype(BF16)
    lo_lane = lax.broadcasted_iota(jnp.int32, (1, LANES), 1) < HALF

    f = _dot3(sh, sl, wfh_ref[...], wfl_ref[...])[:, 0:B_HEADS]
    logf_ref[...] = _log_sigmoid(f + bf_ref[...])

    for c in range(d // MXU_N):
        kraw = _dot(sh, wkv_ref[:, c * MXU_N:(c + 1) * MXU_N])
        qraw = _dot(hq, wqg_ref[:, c * MXU_N:(c + 1) * MXU_N])
        for u in range(MXU_N // LANES):
            us = slice(u * LANES, (u + 1) * LANES)
            cs = slice(c * MXU_N + u * LANES, c * MXU_N + (u + 1) * LANES)
            kn = _head_norm_group(kraw[:, us], gk_ref[:, cs], lo_lane)
            if k_ref.ndim == 3:
                k_ref[0, cs, :] = kn.T
            else:
                k_ref[:, cs] = kn
            kb_ref[:, cs] = kn.astype(BF16)
            qn = _head_norm_group(qraw[:, us], gq_ref[:, cs], lo_lane)
            q_ref[:, cs] = (qn * (LOG2E * B_HD ** -0.5)).astype(BF16)
    for c in range(d // MXU_N):
        cs = slice(c * MXU_N, (c + 1) * MXU_N)
        v = _dot(sh, wkv_ref[:, d + c * MXU_N:d + (c + 1) * MXU_N])
        if v_ref.ndim == 3:
            v_ref[0, cs, :] = v.T
        else:
            v_ref[:, cs] = v
        vb_ref[:, cs] = v.astype(BF16)
    gate_ref[...] = _dot(hq, wqg_ref[:, d:2 * d])


def _fox_proj(x2d, g_kv, g_mix, w_kv, wf_hi, wf_lo, w_qg, b_f, gk_t, gq_t, tm, seq_t=None):
    m, d = x2d.shape
    row = lambda i: (i, 0)
    big = pl.BlockSpec((tm, d), row)
    if seq_t is None:
        kv_spec, kv_shape = big, jax.ShapeDtypeStruct((m, d), F32)
    else:
        per_seq = seq_t // tm
        kv_spec = pl.BlockSpec((1, d, tm), lambda i: (i // per_seq, 0, i % per_seq))
        kv_shape = jax.ShapeDtypeStruct((m // seq_t, d, seq_t), F32)
    return pl.pallas_call(
        _fox_proj_kernel,
        grid=(m // tm,),
        in_specs=[big] + [_resident(a.shape) for a in (g_kv, g_mix, w_kv, wf_hi, wf_lo, w_qg, b_f, gk_t, gq_t)],
        out_specs=[kv_spec, kv_spec, pl.BlockSpec((tm, B_HEADS), row), big, big, big, big],
        out_shape=[kv_shape, kv_shape,
                   jax.ShapeDtypeStruct((m, B_HEADS), F32),
                   jax.ShapeDtypeStruct((m, d), BF16), jax.ShapeDtypeStruct((m, d), BF16),
                   jax.ShapeDtypeStruct((m, d), BF16), jax.ShapeDtypeStruct((m, d), F32)],
        compiler_params=_params("parallel"),
        name="fox_proj",
    )(x2d, g_kv, g_mix, w_kv, wf_hi, wf_lo, w_qg, b_f, gk_t, gq_t)


def _cum_bias_kernel(x_ref, aug_ref, carry_s):
    @pl.when(pl.program_id(1) == 0)
    def _():
        carry_s[...] = jnp.zeros(carry_s.shape, F32)

    tb = x_ref.shape[1]
    tri = (lax.broadcasted_iota(jnp.int32, (tb, tb), 1)
           <= lax.broadcasted_iota(jnp.int32, (tb, tb), 0)).astype(BF16)
    hi, mid, lo = _split3(x_ref[0])
    cum = _dot(tri, hi) + _dot(tri, mid) + _dot(tri, lo) + carry_s[...]
    carry_s[...] = cum[tb - 1:tb, :]

    h_idx = lax.broadcasted_iota(jnp.int32, (B_HEADS, aug_ref.shape[2]), 0)
    c_idx = lax.broadcasted_iota(jnp.int32, (B_HEADS, aug_ref.shape[2]), 1)
    base = N_BIAS_PIECES * h_idx
    aug = None
    for i, piece in enumerate(_split3(cum * (-LOG2E))):
        term = _dot(piece, (c_idx == base + i).astype(BF16))
        aug = term if aug is None else aug + term
    aug_ref[0] = aug.astype(BF16)


def _cum_bias(logf, tb):
    bsz, t, h = logf.shape
    width = LANES
    assert N_BIAS_PIECES * h <= width
    return pl.pallas_call(
        _cum_bias_kernel,
        grid=(bsz, t // tb),
        in_specs=[pl.BlockSpec((1, tb, h), lambda b, j: (b, j, 0))],
        out_specs=pl.BlockSpec((1, tb, width), lambda b, j: (b, j, 0)),
        out_shape=jax.ShapeDtypeStruct((bsz, t, width), BF16),
        scratch_shapes=[pltpu.VMEM((1, h), F32)],
        compiler_params=_params("parallel", "arbitrary"),
        name="cum_bias",
    )(logf)


def _attn_head_scores(h, q2, k2, ca, mask, m_s, lo_lane):
    tq = q2.shape[0]
    half = lo_lane if h % 2 == 0 else jnp.logical_not(lo_lane)
    qm = jnp.where(half, q2, jnp.zeros_like(q2))
    lane2 = lax.broadcasted_iota(jnp.int32, (tq, LANES), 1)
    pick = jnp.logical_and(lane2 >= N_BIAS_PIECES * h, lane2 < N_BIAS_PIECES * (h + 1))
    s = _dot_nt(jnp.concatenate([qm, pick.astype(BF16)], axis=1),
                jnp.concatenate([k2, ca], axis=1))
    cols = [s[:, c * LANES:(c + 1) * LANES] for c in range(s.shape[1] // LANES)]
    if mask is not None:
        cols = [jnp.where(mask[:, c * LANES:(c + 1) * LANES], col, NEG_BIG) for c, col in enumerate(cols)]
    top = functools.reduce(jnp.maximum, cols)
    m_old = m_s[h]
    m_new = jnp.maximum(m_old, jnp.max(top, axis=1, keepdims=True))
    return cols, m_old, m_new


def _attn_head_update(h, cols, m_old, m_new, v2, m_s, acc_s, lo_lane):
    half = lo_lane if h % 2 == 0 else jnp.logical_not(lo_lane)
    alpha = jnp.exp2(m_old - m_new)
    pr = jnp.concatenate([jnp.exp2(col - m_new).astype(BF16) for col in cols], axis=1)
    v_ones = jnp.where(half, v2, jnp.ones_like(v2))
    acc_s[h] = alpha * acc_s[h] + _dot(pr, v_ones)
    m_s[h] = m_new


def _attn_init(m_s, acc_s):
    m_s[...] = jnp.full(m_s.shape, NEG_BIG, F32)
    acc_s[...] = jnp.zeros(acc_s.shape, F32)


def _attn_finish(out_ref, gate_ref, acc_s, lo_lane):
    for p in range(B_HEADS // 2):
        a0, a1 = acc_s[2 * p], acc_s[2 * p + 1]
        num = jnp.where(lo_lane, a0, a1)
        den = pltpu.roll(jnp.where(lo_lane, a1, a0), HALF, axis=1)
        gate = gate_ref[0, :, p * LANES:(p + 1) * LANES]
        out_ref[0, :, p * LANES:(p + 1) * LANES] = (num / den * _sigmoid(gate)).astype(out_ref.dtype)


def _fold_block(r, c, nq):
    first = c <= r
    return jnp.where(first, r, nq - 1 - r), jnp.where(first, c, c - r - 1)


def _prompt_attn_kernel(q_ref, k_ref, ca_ref, v_ref, gate_ref, out_ref, m_s, acc_s, *, blk, nq):
    i, j = _fold_block(pl.program_id(1), pl.program_id(2), nq)
    lo_lane = lax.broadcasted_iota(jnp.int32, (1, LANES), 1) < HALF

    @pl.when(j == 0)
    def _():
        _attn_init(m_s, acc_s)

    def run(masked):
        mask = None
        if masked:
            mask = (lax.broadcasted_iota(jnp.int32, (blk, blk), 1)
                    <= lax.broadcasted_iota(jnp.int32, (blk, blk), 0))
        ca = ca_ref[0]

        def scores(h):
            cs = slice((h // 2) * LANES, (h // 2 + 1) * LANES)
            return _attn_head_scores(h, q_ref[0, :, cs], k_ref[0, :, cs], ca, mask, m_s, lo_lane)

        pending = scores(0)
        for h in range(B_HEADS):
            upcoming = scores(h + 1) if h + 1 < B_HEADS else None
            cs = slice((h // 2) * LANES, (h // 2 + 1) * LANES)
            _attn_head_update(h, *pending, v_ref[0, :, cs], m_s, acc_s, lo_lane)
            pending = upcoming

    @pl.when(j < i)
    def _():
        run(False)

    @pl.when(j == i)
    def _():
        run(True)
        _attn_finish(out_ref, gate_ref, acc_s, lo_lane)


def _prompt_attn(q, kb, cum_aug, vb, gate, blk):
    bsz, t, d = q.shape
    nq = t // blk
    rows = max(nq // 2, 1)
    assert nq == 1 or nq % 2 == 0
    qmap = lambda b, r, c: (b, _fold_block(r, c, nq)[0], 0)
    kvmap = lambda b, r, c: (b, _fold_block(r, c, nq)[1], 0)
    return pl.pallas_call(
        functools.partial(_prompt_attn_kernel, blk=blk, nq=nq),
        grid=(bsz, rows, nq + 1 if nq > 1 else 1),
        in_specs=[pl.BlockSpec((1, blk, d), qmap), pl.BlockSpec((1, blk, d), kvmap),
                  pl.BlockSpec((1, blk, cum_aug.shape[2]), kvmap), pl.BlockSpec((1, blk, d), kvmap),
                  pl.BlockSpec((1, blk, d), qmap)],
        out_specs=pl.BlockSpec((1, blk, d), qmap),
        out_shape=jax.ShapeDtypeStruct((bsz, t, d), BF16),
        scratch_shapes=[pltpu.VMEM((B_HEADS, blk, LANES), F32), pltpu.VMEM((B_HEADS, blk, LANES), F32)],
        compiler_params=_params("parallel", "arbitrary", "arbitrary"),
        name="prompt_attn",
    )(q, kb, cum_aug, vb, gate)


def _group_rows(x, tq):
    return jnp.concatenate([jnp.broadcast_to(x[r:r + 1, :], (tq, x.shape[1])) for r in range(x.shape[0])], axis=0)


def _lane_cumsum(x):
    r, n = x.shape
    nb = n // MXU_N
    stacked = jnp.concatenate([x[:, c * MXU_N:(c + 1) * MXU_N] for c in range(nb)], axis=0)
    upper = (lax.broadcasted_iota(jnp.int32, (MXU_N, MXU_N), 0)
             <= lax.broadcasted_iota(jnp.int32, (MXU_N, MXU_N), 1)).astype(BF16)
    within = sum(_dot(piece, upper) for piece in _split3(stacked))
    totals = jnp.broadcast_to(within[:, MXU_N - 1:MXU_N], (nb * r, LANES))
    ri = lax.broadcasted_iota(jnp.int32, (nb * r, nb * r), 0)
    ci = lax.broadcasted_iota(jnp.int32, (nb * r, nb * r), 1)
    earlier = jnp.logical_and(ci // r < ri // r, ci % r == ri % r).astype(BF16)
    offs = sum(_dot(earlier, piece) for piece in _split3(totals))
    full = within + jnp.concatenate([offs] * (MXU_N // LANES), axis=1)
    return [full[c * r:(c + 1) * r, :] for c in range(nb)]


def _sample_attn_kernel(q_ref, kt_ref, vt_ref, lf_ref, nk_ref, nv_ref, nlf_ref, gate_ref, out_ref,
                        cum_s, m_s, l_s, acc_s, *, tk):
    j = pl.program_id(1)
    tq = q_ref.shape[1]
    n_groups = acc_s.shape[0]
    hg = B_HEADS // n_groups
    gw = hg * B_HD
    rows = hg * tq
    row_head = lax.broadcasted_iota(jnp.int32, (rows, gw), 0) // tq
    lane_head = lax.broadcasted_iota(jnp.int32, (rows, gw), 1) // B_HD
    own = row_head == lane_head

    @pl.when(j == 0)
    def _():
        m_s[...] = jnp.full(m_s.shape, NEG_BIG, F32)
        l_s[...] = jnp.zeros(l_s.shape, F32)
        acc_s[...] = jnp.zeros(acc_s.shape, F32)
        blocks = _lane_cumsum(lf_ref[0])
        per_step = tk // MXU_N
        for c, blk in enumerate(blocks):
            cum_s[c // per_step, :, (c % per_step) * MXU_N:(c % per_step + 1) * MXU_N] = blk * LOG2E

    def q_block_diag(g):
        qg = q_ref[0, :, g * gw:(g + 1) * gw]
        qrep = jnp.concatenate([qg] * hg, axis=0)
        return jnp.where(own, qrep, jnp.zeros_like(qrep))

    def online_update(g, cols, pv_fn):
        top = functools.reduce(jnp.maximum, cols)
        m_old = m_s[g]
        m_new = jnp.maximum(m_old, jnp.max(top, axis=1, keepdims=True))
        alpha = jnp.exp2(m_old - m_new)
        probs = [jnp.exp2(col - m_new[:, 0:col.shape[1]]) for col in cols]
        if cols[0].shape[1] == LANES:
            psum = functools.reduce(jnp.add, probs)
        else:
            lane0 = lax.broadcasted_iota(jnp.int32, (rows, LANES), 1) == 0
            psum = jnp.where(lane0, jnp.sum(probs[0], axis=1, keepdims=True), 0.0)
        l_s[g] = alpha * l_s[g] + psum
        p = probs[0] if len(probs) == 1 else jnp.concatenate(probs, axis=1)
        acc_s[g] = jnp.concatenate([alpha] * (gw // LANES), axis=1) * acc_s[g] + pv_fn(p.astype(BF16))
        m_s[g] = m_new

    cum_blk = cum_s[j]
    for g in range(n_groups):
        kt = kt_ref[0, g * gw:(g + 1) * gw, :].astype(BF16)
        s = _dot(q_block_diag(g), kt) - _group_rows(cum_blk[g * hg:(g + 1) * hg, :], tq)
        cols = [s[:, c * LANES:(c + 1) * LANES] for c in range(tk // LANES)]
        vt = vt_ref[0, g * gw:(g + 1) * gw, :].astype(BF16)
        online_update(g, cols, lambda p, vt=vt: _dot_nt(p, vt))

    @pl.when(j == pl.num_programs(1) - 1)
    def _():
        t_q = lax.broadcasted_iota(jnp.int32, (rows, tq), 0) % tq
        t_k = lax.broadcasted_iota(jnp.int32, (rows, tq), 1)
        causal = t_k <= t_q
        upper = (lax.broadcasted_iota(jnp.int32, (tq, tq), 0)
                 <= lax.broadcasted_iota(jnp.int32, (tq, tq), 1)).astype(BF16)
        new_cum = sum(_dot(piece, upper) for piece in _split3(nlf_ref[0]))
        last = cum_s[pl.num_programs(1) - 1][:, tk - 1:tk]
        new_cum = new_cum * LOG2E + last
        lane_grp = lax.broadcasted_iota(jnp.int32, (tq, gw), 1) // B_HD
        for g in range(n_groups):
            kn = nk_ref[0, :, g * gw:(g + 1) * gw]
            s = _dot_nt(q_block_diag(g), kn) - _group_rows(new_cum[g * hg:(g + 1) * hg, :], tq)
            s = jnp.where(causal, s, NEG_BIG)
            vn = nv_ref[0, :, g * gw:(g + 1) * gw]
            online_update(g, [s], lambda p, vn=vn: _dot(p, vn))
            o = acc_s[g] / jnp.sum(l_s[g], axis=1, keepdims=True)
            picked = functools.reduce(
                jnp.add, [jnp.where(lane_grp == r, o[r * tq:(r + 1) * tq, :], 0.0) for r in range(hg)])
            gate = gate_ref[0, :, g * gw:(g + 1) * gw]
            out_ref[0, :, g * gw:(g + 1) * gw] = (picked * _sigmoid(gate)).astype(out_ref.dtype)


def _sample_attn(q, cache_kt, cache_vt, cache_logf_t, kb, vb, new_logf_t, gate, tk):
    bsz, tq, d = q.shape
    past = cache_kt.shape[2]
    n_groups = d // MXU_N
    one = lambda b, j: (b, 0, 0)
    blk = lambda b, j: (b, 0, j)
    return pl.pallas_call(
        functools.partial(_sample_attn_kernel, tk=tk),
        grid=(bsz, past // tk),
        in_specs=[pl.BlockSpec((1, tq, d), one), pl.BlockSpec((1, d, tk), blk), pl.BlockSpec((1, d, tk), blk),
                  pl.BlockSpec((1, B_HEADS, past), one),
                  pl.BlockSpec((1, tq, d), one), pl.BlockSpec((1, tq, d), one),
                  pl.BlockSpec((1, B_HEADS, tq), one), pl.BlockSpec((1, tq, d), one)],
        out_specs=pl.BlockSpec((1, tq, d), one),
        out_shape=jax.ShapeDtypeStruct((bsz, tq, d), BF16),
        scratch_shapes=[pltpu.VMEM((past // tk, B_HEADS, tk), F32),
                        pltpu.VMEM((n_groups, MXU_N // B_HD * tq, LANES), F32),
                        pltpu.VMEM((n_groups, MXU_N // B_HD * tq, LANES), F32),
                        pltpu.VMEM((n_groups, MXU_N // B_HD * tq, MXU_N), F32)],
        compiler_params=_params("parallel", "arbitrary"),
        name="sample_attn",
    )(q, cache_kt, cache_vt, cache_logf_t, kb, vb, new_logf_t, gate)


def _pad_cols(w, n):
    return jnp.pad(w, ((0, 0), (0, n - w.shape[1])))


def _prep_weights(g_mix, g_ffn, w_in_a, b_i_a, b_f_a, g_head_a, w_out_a, g_kv, w_kvf, b_f_b, g_k,
                  w_qg_b, g_q_b, w_o_b, w_gu, w_down):
    d = g_mix.shape[1]
    main_w = 2 * A_HEADS * A_DQK + 2 * A_HEADS * A_DV
    w_in = w_in_a[0]
    wg = _pad_cols(w_in[:, main_w:], LANES)
    wg_hi, wg_lo = _split2(wg)
    wf = _pad_cols(w_kvf[:, 2 * d:], LANES)
    wf_hi, wf_lo = _split2(wf)
    gates_b = jnp.concatenate([b_i_a[0], b_f_a[0]])

    def ffn(layer):
        return w_gu[layer].astype(BF16), w_down[layer].astype(BF16)

    return dict(
        g_mix0=g_mix[0][None], g_mix1=g_mix[1][None], g_ffn0=g_ffn[0][None], g_ffn1=g_ffn[1][None],
        w_in=w_in[:, :main_w].astype(BF16), wg_hi=wg_hi, wg_lo=wg_lo,
        gb_col=gates_b[None, :], gb_row=gates_b[:, None], g_head=g_head_a[0][None],
        w_out=w_out_a[0].astype(BF16),
        g_kv=g_kv[None], w_kv=w_kvf[:, :2 * d].astype(BF16), wf_hi=wf_hi, wf_lo=wf_lo, b_f=b_f_b[None],
        gk_t=jnp.tile(g_k, B_HEADS)[None], gq_t=jnp.tile(g_q_b[0], B_HEADS)[None],
        w_qg=w_qg_b[0].astype(BF16), w_o=w_o_b[0].astype(BF16),
        ffn0=ffn(0), ffn1=ffn(1),
    )


def _state_to_pairs(c, n, m):
    bsz = c.shape[0]
    np_ = A_HEADS // 2
    ct = c.reshape(bsz, np_, 2, A_DV, A_DQK).transpose(0, 1, 2, 4, 3).reshape(bsz, np_, 2 * A_DQK, A_DV)
    n2 = n.reshape(bsz, np_, 1, 2 * A_DQK)
    m2 = jnp.broadcast_to(m[:, :, None, None], (bsz, A_HEADS, 1, LANES))
    return ct.astype(F32), n2.astype(F32), m2.astype(F32)


def _state_from_pairs(ct, n2, m2):
    bsz = ct.shape[0]
    np_ = A_HEADS // 2
    c = ct.reshape(bsz, np_, 2, A_DQK, A_DV).transpose(0, 1, 2, 4, 3).reshape(bsz, 1, A_HEADS, A_DV, A_DQK)
    n = n2.reshape(bsz, 1, A_HEADS, A_DQK)
    m = m2[:, :, 0, 0].reshape(bsz, 1, A_HEADS)
    return c, n, m


def _layer_a(x, c0, n0, m0, w, chunk, tm):
    bsz, t, d = x.shape
    x2d = x.reshape(bsz * t, d)
    q, k, v, o, gates = _in_proj(x2d, w["g_mix0"], w["w_in"], w["wg_hi"], w["wg_lo"], tm)
    r3 = lambda z: z.reshape(bsz, t, z.shape[1])
    gates3 = r3(gates)
    ct0, n20, m20 = _state_to_pairs(c0, n0, m0)
    hn, ct, n2, m2 = _mlstm(r3(q), r3(k), r3(v), r3(o), gates3, gates3.transpose(0, 2, 1),
                            w["gb_col"], w["gb_row"], w["g_head"], ct0, n20, m20, chunk)
    x2 = _mix_ffn(hn.reshape(bsz * t, -1), x2d, w["w_out"], w["g_ffn0"], *w["ffn0"], tm)
    return x2, _state_from_pairs(ct, n2, m2)


def _fox_inputs(x2, w, tm, seq_t=None):
    return _fox_proj(x2, w["g_kv"], w["g_mix1"], w["w_kv"], w["wf_hi"], w["wf_lo"], w["w_qg"],
                     w["b_f"], w["gk_t"], w["gq_t"], tm, seq_t)


def _prompt_trunk(x, w):
    bsz, t, d = x.shape
    tm = 512
    zeros = lambda *s: jnp.zeros(s, F32)
    x2, (c, n, m) = _layer_a(x, zeros(bsz, A_HEADS, A_DV, A_DQK), zeros(bsz, A_HEADS, A_DQK),
                             zeros(bsz, A_HEADS), w, min(t, 128), tm)
    kt, vt, logf, kb, vb, q, gate = _fox_inputs(x2, w, tm, seq_t=t)
    r3 = lambda z: z.reshape(bsz, t, z.shape[1])
    logf3 = r3(logf)
    cum_aug = _cum_bias(logf3, min(t, 512))
    attn = _prompt_attn(r3(q), r3(kb), cum_aug, r3(vb), r3(gate), min(t, 512))
    y = _mix_ffn(attn.reshape(bsz * t, d), x2, w["w_o"], w["g_ffn1"], *w["ffn1"], tm)
    to_out = lambda z: z.reshape(bsz, B_HEADS, B_HD, t).transpose(0, 3, 1, 2)
    return (y.reshape(bsz, t, d), to_out(kt), to_out(vt), logf3, c, n, m)


def _sample_trunk(x, c0, n0, m0, cache_k, cache_v, cache_logf, w):
    bsz, t, d = x.shape
    past = cache_k.shape[1]
    tm = min(bsz * t, 512)
    x2, (c, n, m) = _layer_a(x, c0[:, 0], n0[:, 0], m0[:, 0], w, t, tm)
    k, v, logf, kb, vb, q, gate = _fox_inputs(x2, w, tm)
    r3 = lambda z: z.reshape(bsz, t, z.shape[1])
    logf3 = r3(logf)
    kt = cache_k.transpose(0, 2, 3, 1).reshape(bsz, d, past)
    vt = cache_v.transpose(0, 2, 3, 1).reshape(bsz, d, past)
    attn = _sample_attn(r3(q), kt, vt, cache_logf.transpose(0, 2, 1), r3(kb), r3(vb),
                        logf3.transpose(0, 2, 1), r3(gate), min(past, 1024))
    y = _mix_ffn(attn.reshape(bsz * t, d), x2, w["w_o"], w["g_ffn1"], *w["ffn1"], tm)
    return (y.reshape(bsz, t, d), k.reshape(bsz, t, B_HEADS, B_HD), v.reshape(bsz, t, B_HEADS, B_HD),
            logf3, c, n, m)


def kernel(x_prompt, x_sample, cache_k, cache_v, cache_logf, state_c, state_n, state_m, g_mix, g_ffn, w_in_a, b_i_a, b_f_a, g_head_a, w_out_a, g_kv, w_kvf, b_f_b, g_k, w_qg_b, g_q_b, w_o_b, w_gu, w_down):
    w = _prep_weights(g_mix, g_ffn, w_in_a, b_i_a, b_f_a, g_head_a, w_out_a, g_kv, w_kvf, b_f_b, g_k,
                      w_qg_b, g_q_b, w_o_b, w_gu, w_down)
    y_p, k_p, v_p, f_p, c_p, n_p, m_p = _prompt_trunk(x_prompt, w)
    y_s, k_s, v_s, f_s, c_s, n_s, m_s = _sample_trunk(x_sample, state_c, state_n, state_m,
                                                      cache_k, cache_v, cache_logf, w)
    return (y_p, y_s, k_p, v_p, f_p, c_p, n_p, m_p, k_s, v_s, f_s, c_s, n_s, m_s)
```

```python
import functools

import jax
import jax.numpy as jnp
from jax import lax
from jax.experimental import pallas as pl
from jax.experimental.pallas import tpu as pltpu

F32 = jnp.float32
BF16 = jnp.bfloat16

EPS = 1e-6
GATE_CAP = 15.0
A_HEADS = 8
A_DQK = 64
A_DV = 128
B_HEADS = 16
B_HD = 64
LANES = 128
HALF = 64
MXU_N = 256
FF_CHUNK = MXU_N
NEG_BIG = -1e30
LOG2E = 1.4426950408889634
N_BIAS_PIECES = 3
VMEM_LIMIT = 56 * 1024 * 1024


def _params(*sem):
    return pltpu.CompilerParams(dimension_semantics=sem, vmem_limit_bytes=VMEM_LIMIT)


def _resident(shape):
    nd = len(shape)
    return pl.BlockSpec(shape, lambda *_: (0,) * nd, pipeline_mode=pl.Buffered(1))


def _dot(a, b):
    return jnp.dot(a, b, preferred_element_type=F32)


def _dot_nt(a, b):
    return lax.dot_general(a, b, (((1,), (1,)), ((), ())), preferred_element_type=F32)


def _split2(x):
    hi = x.astype(BF16)
    lo = (x - hi.astype(F32)).astype(BF16)
    return hi, lo


def _split3(x):
    hi = x.astype(BF16)
    r = x - hi.astype(F32)
    mid = r.astype(BF16)
    lo = (r - mid.astype(F32)).astype(BF16)
    return hi, mid, lo


def _hi_lo_cols(w):
    hi, lo = _split2(w)
    both = jnp.concatenate([hi, lo], axis=1)
    return jnp.pad(both, ((0, 0), (0, LANES - both.shape[1])))


def _log_sigmoid(x):
    return jnp.minimum(x, 0.0) - jnp.log(1.0 + jnp.exp(-jnp.abs(x)))


def _sigmoid(x):
    return 1.0 / (1.0 + jnp.exp(-x))


def _soft_cap(z):
    return GATE_CAP * jnp.tanh(z / GATE_CAP)


def _rms_hat(x):
    return x * lax.rsqrt(jnp.mean(x * x, axis=-1, keepdims=True) + EPS)


def _pair_select(lo_mask, a, b):
    return jnp.where(lo_mask, a, b)


def _head_norm_group(xc, gc, lo_mask):
    sq = xc * xc
    s_lo = jnp.sum(jnp.where(lo_mask, sq, 0.0), axis=-1, keepdims=True)
    s_hi = jnp.sum(jnp.where(lo_mask, 0.0, sq), axis=-1, keepdims=True)
    ms = jnp.where(lo_mask, s_lo, s_hi) * (1.0 / B_HD)
    return xc * lax.rsqrt(ms + EPS) * gc


def _in_proj_kernel(x_ref, g_ref, w_ref, wg2_ref, q_ref, k_ref, v_ref, o_ref, gates_ref):
    xh = (_rms_hat(x_ref[...]) * g_ref[...]).astype(BF16)
    qk_w = A_HEADS * A_DQK
    v_w = A_HEADS * A_DV
    q_ref[...] = _dot(xh, w_ref[:, 0:qk_w]).astype(BF16)
    k_ref[...] = (_dot(xh, w_ref[:, qk_w:2 * qk_w]) * (A_DQK ** -0.5)).astype(BF16)
    v_ref[...] = _dot(xh, w_ref[:, 2 * qk_w:2 * qk_w + v_w]).astype(BF16)
    o_ref[...] = _dot(xh, w_ref[:, 2 * qk_w + v_w:2 * qk_w + 2 * v_w])
    gz = _dot(xh, wg2_ref[...])
    gates_ref[...] = gz[:, 0:2 * A_HEADS] + gz[:, 2 * A_HEADS:4 * A_HEADS]


def _in_proj(x2d, g, w_main, wg2, tm):
    m, d = x2d.shape
    qk_w = A_HEADS * A_DQK
    v_w = A_HEADS * A_DV
    row = lambda i: (i, 0)
    return pl.pallas_call(
        _in_proj_kernel,
        grid=(m // tm,),
        in_specs=[pl.BlockSpec((tm, d), row), _resident(g.shape), _resident(w_main.shape),
                  _resident(wg2.shape)],
        out_specs=[pl.BlockSpec((tm, qk_w), row), pl.BlockSpec((tm, qk_w), row),
                   pl.BlockSpec((tm, v_w), row), pl.BlockSpec((tm, v_w), row),
                   pl.BlockSpec((tm, 2 * A_HEADS), row)],
        out_shape=[jax.ShapeDtypeStruct((m, qk_w), BF16), jax.ShapeDtypeStruct((m, qk_w), BF16),
                   jax.ShapeDtypeStruct((m, v_w), BF16), jax.ShapeDtypeStruct((m, v_w), F32),
                   jax.ShapeDtypeStruct((m, 2 * A_HEADS), F32)],
        compiler_params=_params("parallel"),
        name="in_proj",
    )(x2d, g, w_main, wg2)


def _mlstm_kernel(q_ref, k_ref, v_ref, o_ref, gc_ref, gr_ref, bc_ref, br_ref, gh_ref,
                  c0_ref, n0_ref, m0_ref,
                  h_ref, c_out, n_out, m_out, ct_s, n_s, m_s, *, chunk):
    j = pl.program_id(1)
    L = chunk

    @pl.when(j == 0)
    def _():
        ct_s[...] = c0_ref[0]
        n_s[...] = n0_ref[0]
        m_s[...] = m0_ref[0]

    cap_c = _soft_cap(gc_ref[0] + bc_ref[...])
    cap_r = _soft_cap(gr_ref[0] + br_ref[...])
    li_c = cap_c[:, 0:A_HEADS]
    lf_c = _log_sigmoid(cap_c[:, A_HEADS:2 * A_HEADS])
    li_r = cap_r[0:A_HEADS, :]
    lf_r = _log_sigmoid(cap_r[A_HEADS:2 * A_HEADS, :])

    t_idx = lax.broadcasted_iota(jnp.int32, (L, L), 0)
    s_idx = lax.broadcasted_iota(jnp.int32, (L, L), 1)
    causal = s_idx <= t_idx
    lane = lax.broadcasted_iota(jnp.int32, (1, LANES), 1)
    lo_lane = lane < HALF
    eye = (lax.broadcasted_iota(jnp.int32, (LANES, LANES), 0)
           == lax.broadcasted_iota(jnp.int32, (LANES, LANES), 1)).astype(BF16)

    def rep(col):
        return jnp.broadcast_to(col, (col.shape[0], LANES))

    place = (lax.broadcasted_iota(jnp.int32, (2 * A_HEADS, 2 * A_HEADS * LANES), 0)
             == lax.broadcasted_iota(jnp.int32, (2 * A_HEADS, 2 * A_HEADS * LANES), 1) // LANES).astype(BF16)
    cols_rep = sum(_dot(piece, place) for piece in _split3(jnp.concatenate([li_c, lf_c], axis=1)))
    lo_row = lax.broadcasted_iota(jnp.int32, (LANES, LANES), 0) < HALF

    def head_stages(h, q2, k2, kt2, ct2b, qn, done):
        half = lo_lane if h % 2 == 0 else jnp.logical_not(lo_lane)
        li_col = cols_rep[:, h * LANES:(h + 1) * LANES]
        lf_col = cols_rep[:, (A_HEADS + h) * LANES:(A_HEADS + h + 1) * LANES]
        li_row = li_r[h:h + 1, :]
        lf_row = lf_r[h:h + 1, :]
        m_prev = m_s[h]
        v_h = v_ref[0, :, h * A_DV:(h + 1) * A_DV]
        qm = jnp.where(half, q2, jnp.zeros_like(q2))
        qk = _dot_nt(qm, k2)
        qc = _dot(qm, ct2b)
        qn_sum = rep(jnp.sum(jnp.where(half, qn, 0.0), axis=1, keepdims=True))
        b_col = rep(jnp.sum(jnp.where(causal, lf_row, 0.0), axis=1, keepdims=True))
        b_row = jnp.sum(jnp.where(t_idx <= s_idx, lf_col[:, :L], 0.0), axis=0, keepdims=True)
        b_last = rep(jnp.sum(lf_row, axis=1, keepdims=True))
        yield
        dmat = jnp.where(causal, b_col[:, :L] - b_row + li_row, NEG_BIG)
        inter = b_col + m_prev
        m_t = jnp.maximum(inter, rep(jnp.max(dmat, axis=1, keepdims=True)))
        w_col = b_last - b_col + li_col
        m_new = jnp.maximum(b_last + m_prev, jnp.max(w_col, axis=0, keepdims=True))
        yield
        g = jnp.exp(inter - m_t)
        dec = jnp.where(causal, jnp.exp(dmat - m_t[:, :L]), 0.0)
        floor = jnp.exp(-m_t)
        g_c = jnp.exp(b_last + m_prev - m_new)
        ws = jnp.exp(w_col - m_new)
        yield
        s = qk * dec
        wsv = (ws * v_h.astype(F32)).astype(BF16)
        upd = _dot(kt2, wsv)
        num = g * qc + _dot(s.astype(BF16), v_h)
        den = g * qn_sum + rep(jnp.sum(s, axis=1, keepdims=True))
        yield
        hv = num / jnp.maximum(jnp.abs(den), floor)
        ms = rep(jnp.mean(hv * hv, axis=-1, keepdims=True))
        yield
        hv = hv * lax.rsqrt(ms + EPS)
        hv = hv * gh_ref[:, h * A_DV:(h + 1) * A_DV]
        hv = hv * _sigmoid(o_ref[0, :, h * A_DV:(h + 1) * A_DV])
        h_ref[0, :, h * A_DV:(h + 1) * A_DV] = hv.astype(h_ref.dtype)
        m_s[h] = m_new
        done[h] = (upd, g_c, ws)

    pairs = []
    done = {}
    gens = []
    for p in range(A_HEADS // 2):
        q2 = q_ref[0, :, p * LANES:(p + 1) * LANES]
        k2 = k_ref[0, :, p * LANES:(p + 1) * LANES]
        kt2 = _dot_nt(eye, k2).astype(BF16)
        ct2 = ct_s[p]
        n2 = n_s[p]
        pairs.append((k2, ct2, n2))
        for hh in range(2):
            gens.append(head_stages(2 * p + hh, q2, k2, kt2, ct2.astype(BF16), q2.astype(F32) * n2, done))
    while gens:
        alive = []
        for gen in gens:
            try:
                next(gen)
                alive.append(gen)
            except StopIteration:
                pass
        gens = alive

    for p, (k2, ct2, n2) in enumerate(pairs):
        (upd0, gc0, ws0), (upd1, gc1, ws1) = done[2 * p], done[2 * p + 1]
        ct_s[p] = jnp.where(lo_row, gc0, gc1) * ct2 + jnp.where(lo_row, upd0, upd1)
        ws2 = jnp.where(lo_lane, ws0, ws1)
        n_s[p] = (jnp.where(lo_lane, gc0, gc1) * n2
                  + jnp.sum(ws2 * k2.astype(F32), axis=0, keepdims=True))

    @pl.when(j == pl.num_programs(1) - 1)
    def _():
        c_out[0] = ct_s[...]
        n_out[0] = n_s[...]
        m_out[0] = m_s[...]


def _mlstm(q, k, v, o, gates, gates_t, b_col, b_row, g_head, ct0, n0, m0, chunk):
    bsz, t, _ = q.shape
    nc = t // chunk
    np_ = A_HEADS // 2
    tok = lambda b, j: (b, j, 0)
    st4 = lambda b, j: (b, 0, 0, 0)
    return pl.pallas_call(
        functools.partial(_mlstm_kernel, chunk=chunk),
        grid=(bsz, nc),
        in_specs=[pl.BlockSpec((1, chunk, q.shape[2]), tok), pl.BlockSpec((1, chunk, k.shape[2]), tok),
                  pl.BlockSpec((1, chunk, v.shape[2]), tok), pl.BlockSpec((1, chunk, o.shape[2]), tok),
                  pl.BlockSpec((1, chunk, 2 * A_HEADS), tok),
                  pl.BlockSpec((1, 2 * A_HEADS, chunk), lambda b, j: (b, 0, j)),
                  _resident(b_col.shape), _resident(b_row.shape), _resident(g_head.shape),
                  pl.BlockSpec((1, np_, LANES, A_DV), st4), pl.BlockSpec((1, np_, 1, LANES), st4),
                  pl.BlockSpec((1, A_HEADS, 1, LANES), st4)],
        out_specs=[pl.BlockSpec((1, chunk, v.shape[2]), tok),
                   pl.BlockSpec((1, np_, LANES, A_DV), st4), pl.BlockSpec((1, np_, 1, LANES), st4),
                   pl.BlockSpec((1, A_HEADS, 1, LANES), st4)],
        out_shape=[jax.ShapeDtypeStruct((bsz, t, v.shape[2]), BF16),
                   jax.ShapeDtypeStruct((bsz, np_, LANES, A_DV), F32),
                   jax.ShapeDtypeStruct((bsz, np_, 1, LANES), F32),
                   jax.ShapeDtypeStruct((bsz, A_HEADS, 1, LANES), F32)],
        scratch_shapes=[pltpu.VMEM((np_, LANES, A_DV), F32), pltpu.VMEM((np_, 1, LANES), F32),
                        pltpu.VMEM((A_HEADS, 1, LANES), F32)],
        compiler_params=_params("parallel", "arbitrary"),
        name="mlstm",
    )(q, k, v, o, gates, gates_t, b_col, b_row, g_head, ct0, n0, m0)


def _mix_ffn_kernel(a_ref, res_ref, wo_ref, g_ref, wgu_ref, wd_ref, out_ref, xn_s, acc_s):
    x1 = res_ref[...] + _dot(a_ref[...], wo_ref[...])
    xn_s[...] = (_rms_hat(x1) * g_ref[...]).astype(BF16)
    acc_s[...] = x1

    d_ff = wd_ref.shape[0]
    for c in range(d_ff // FF_CHUNK):
        lo, hi = c * FF_CHUNK, (c + 1) * FF_CHUNK
        xn = xn_s[...]
        gate = _dot(xn, wgu_ref[:, lo:hi])
        up = _dot(xn, wgu_ref[:, d_ff + lo:d_ff + hi])
        act = (gate * _sigmoid(gate) * up).astype(BF16)
        acc_s[...] += _dot(act, wd_ref[lo:hi, :])
    out_ref[...] = acc_s[...]


def _mix_ffn(a, res, w_o, g, wgu, wd, tm):
    m, d = res.shape
    row = lambda i: (i, 0)
    return pl.pallas_call(
        _mix_ffn_kernel,
        grid=(m // tm,),
        in_specs=[pl.BlockSpec((tm, a.shape[1]), row), pl.BlockSpec((tm, d), row),
                  _resident(w_o.shape), _resident(g.shape), _resident(wgu.shape), _resident(wd.shape)],
        out_specs=pl.BlockSpec((tm, d), row),
        out_shape=jax.ShapeDtypeStruct((m, d), F32),
        scratch_shapes=[pltpu.VMEM((tm, d), BF16), pltpu.VMEM((tm, d), F32)],
        compiler_params=_params("parallel"),
        name="mix_ffn",
    )(a, res, w_o, g, wgu, wd)


def _fox_proj_kernel(x_ref, gkv_ref, gmix_ref, wkv_ref, wf2_ref, wqg_ref, bf_ref, gk_ref, gq_ref,
                     k_ref, v_ref, logf_ref, kb_ref, vb_ref, q_ref, gate_ref):
    d = x_ref.shape[1]
    xhat = _rms_hat(x_ref[...])
    sh = (xhat * gkv_ref[...]).astype(BF16)
    hq = (xhat * gmix_ref[...]).astype(BF16)
    lo_lane = lax.broadcasted_iota(jnp.int32, (1, LANES), 1) < HALF

    fz = _dot(sh, wf2_ref[...])
    logf_ref[...] = _log_sigmoid(fz[:, 0:B_HEADS] + fz[:, B_HEADS:2 * B_HEADS] + bf_ref[...])

    for c in range(d // MXU_N):
        kraw = _dot(sh, wkv_ref[:, c * MXU_N:(c + 1) * MXU_N])
        qraw = _dot(hq, wqg_ref[:, c * MXU_N:(c + 1) * MXU_N])
        for u in range(MXU_N // LANES):
            us = slice(u * LANES, (u + 1) * LANES)
            cs = slice(c * MXU_N + u * LANES, c * MXU_N + (u + 1) * LANES)
            kn = _head_norm_group(kraw[:, us], gk_ref[:, cs], lo_lane)
            if k_ref.ndim == 3:
                k_ref[0, cs, :] = kn.T
            else:
                k_ref[:, cs] = kn
            kb_ref[:, cs] = kn.astype(BF16)
            qn = _head_norm_group(qraw[:, us], gq_ref[:, cs], lo_lane)
            q_ref[:, cs] = (qn * (LOG2E * B_HD ** -0.5)).astype(BF16)
    for c in range(d // MXU_N):
        cs = slice(c * MXU_N, (c + 1) * MXU_N)
        v = _dot(sh, wkv_ref[:, d + c * MXU_N:d + (c + 1) * MXU_N])
        if v_ref.ndim == 3:
            v_ref[0, cs, :] = v.T
        else:
            v_ref[:, cs] = v
        vb_ref[:, cs] = v.astype(BF16)
    gate_ref[...] = _dot(hq, wqg_ref[:, d:2 * d])


def _fox_proj(x2d, g_kv, g_mix, w_kv, wf2, w_qg, b_f, gk_t, gq_t, tm, seq_t=None):
    m, d = x2d.shape
    row = lambda i: (i, 0)
    big = pl.BlockSpec((tm, d), row)
    if seq_t is None:
        kv_spec, kv_shape = big, jax.ShapeDtypeStruct((m, d), F32)
    else:
        per_seq = seq_t // tm
        kv_spec = pl.BlockSpec((1, d, tm), lambda i: (i // per_seq, 0, i % per_seq))
        kv_shape = jax.ShapeDtypeStruct((m // seq_t, d, seq_t), F32)
    return pl.pallas_call(
        _fox_proj_kernel,
        grid=(m // tm,),
        in_specs=[big] + [_resident(a.shape) for a in (g_kv, g_mix, w_kv, wf2, w_qg, b_f, gk_t, gq_t)],
        out_specs=[kv_spec, kv_spec, pl.BlockSpec((tm, B_HEADS), row), big, big, big, big],
        out_shape=[kv_shape, kv_shape,
                   jax.ShapeDtypeStruct((m, B_HEADS), F32),
                   jax.ShapeDtypeStruct((m, d), BF16), jax.ShapeDtypeStruct((m, d), BF16),
                   jax.ShapeDtypeStruct((m, d), BF16), jax.ShapeDtypeStruct((m, d), F32)],
        compiler_params=_params("parallel"),
        name="fox_proj",
    )(x2d, g_kv, g_mix, w_kv, wf2, w_qg, b_f, gk_t, gq_t)


def _cum_bias_kernel(x_ref, aug_ref, carry_s):
    @pl.when(pl.program_id(1) == 0)
    def _():
        carry_s[...] = jnp.zeros(carry_s.shape, F32)

    tb = x_ref.shape[1]
    tri = (lax.broadcasted_iota(jnp.int32, (tb, tb), 1)
           <= lax.broadcasted_iota(jnp.int32, (tb, tb), 0)).astype(BF16)
    hi, mid, lo = _split3(x_ref[0])
    cum = _dot(tri, hi) + _dot(tri, mid) + _dot(tri, lo) + carry_s[...]
    carry_s[...] = cum[tb - 1:tb, :]

    h_idx = lax.broadcasted_iota(jnp.int32, (B_HEADS, aug_ref.shape[2]), 0)
    c_idx = lax.broadcasted_iota(jnp.int32, (B_HEADS, aug_ref.shape[2]), 1)
    base = N_BIAS_PIECES * h_idx
    aug = None
    for i, piece in enumerate(_split3(cum * (-LOG2E))):
        term = _dot(piece, (c_idx == base + i).astype(BF16))
        aug = term if aug is None else aug + term
    aug_ref[0] = aug.astype(BF16)


def _cum_bias(logf, tb):
    bsz, t, h = logf.shape
    width = LANES
    assert N_BIAS_PIECES * h <= width
    return pl.pallas_call(
        _cum_bias_kernel,
        grid=(bsz, t // tb),
        in_specs=[pl.BlockSpec((1, tb, h), lambda b, j: (b, j, 0))],
        out_specs=pl.BlockSpec((1, tb, width), lambda b, j: (b, j, 0)),
        out_shape=jax.ShapeDtypeStruct((bsz, t, width), BF16),
        scratch_shapes=[pltpu.VMEM((1, h), F32)],
        compiler_params=_params("parallel", "arbitrary"),
        name="cum_bias",
    )(logf)


def _attn_head_scores(h, q2, k2, ca, mask, m_s, lo_lane):
    tq = q2.shape[0]
    half = lo_lane if h % 2 == 0 else jnp.logical_not(lo_lane)
    qm = jnp.where(half, q2, jnp.zeros_like(q2))
    lane2 = lax.broadcasted_iota(jnp.int32, (tq, LANES), 1)
    pick = jnp.logical_and(lane2 >= N_BIAS_PIECES * h, lane2 < N_BIAS_PIECES * (h + 1))
    s = _dot_nt(jnp.concatenate([qm, pick.astype(BF16)], axis=1),
                jnp.concatenate([k2, ca], axis=1))
    cols = [s[:, c * LANES:(c + 1) * LANES] for c in range(s.shape[1] // LANES)]
    if mask is not None:
        cols = [jnp.where(mask[:, c * LANES:(c + 1) * LANES], col, NEG_BIG) for c, col in enumerate(cols)]
    top = functools.reduce(jnp.maximum, cols)
    m_old = m_s[h]
    m_new = jnp.maximum(m_old, jnp.max(top, axis=1, keepdims=True))
    return cols, m_old, m_new


def _attn_head_update(h, cols, m_old, m_new, v2, m_s, acc_s, lo_lane):
    half = lo_lane if h % 2 == 0 else jnp.logical_not(lo_lane)
    alpha = jnp.exp2(m_old - m_new)
    pr = jnp.concatenate([jnp.exp2(col - m_new).astype(BF16) for col in cols], axis=1)
    v_ones = jnp.where(half, v2, jnp.ones_like(v2))
    acc_s[h] = alpha * acc_s[h] + _dot(pr, v_ones)
    m_s[h] = m_new


def _attn_init(m_s, acc_s):
    m_s[...] = jnp.full(m_s.shape, NEG_BIG, F32)
    acc_s[...] = jnp.zeros(acc_s.shape, F32)


def _attn_finish(out_ref, gate_ref, acc_s, lo_lane):
    for p in range(B_HEADS // 2):
        a0, a1 = acc_s[2 * p], acc_s[2 * p + 1]
        num = jnp.where(lo_lane, a0, a1)
        den = pltpu.roll(jnp.where(lo_lane, a1, a0), HALF, axis=1)
        gate = gate_ref[0, :, p * LANES:(p + 1) * LANES]
        out_ref[0, :, p * LANES:(p + 1) * LANES] = (num / den * _sigmoid(gate)).astype(out_ref.dtype)


def _fold_block(r, c, nq):
    first = c <= r
    return jnp.where(first, r, nq - 1 - r), jnp.where(first, c, c - r - 1)


def _prompt_attn_kernel(q_ref, k_ref, ca_ref, v_ref, gate_ref, out_ref, m_s, acc_s, *, blk, nq):
    i, j = _fold_block(pl.program_id(1), pl.program_id(2), nq)
    lo_lane = lax.broadcasted_iota(jnp.int32, (1, LANES), 1) < HALF

    @pl.when(j == 0)
    def _():
        _attn_init(m_s, acc_s)

    def run(masked):
        mask = None
        if masked:
            mask = (lax.broadcasted_iota(jnp.int32, (blk, blk), 1)
                    <= lax.broadcasted_iota(jnp.int32, (blk, blk), 0))
        ca = ca_ref[0]

        def scores(h):
            cs = slice((h // 2) * LANES, (h // 2 + 1) * LANES)
            return _attn_head_scores(h, q_ref[0, :, cs], k_ref[0, :, cs], ca, mask, m_s, lo_lane)

        pending = scores(0)
        for h in range(B_HEADS):
            upcoming = scores(h + 1) if h + 1 < B_HEADS else None
            cs = slice((h // 2) * LANES, (h // 2 + 1) * LANES)
            _attn_head_update(h, *pending, v_ref[0, :, cs], m_s, acc_s, lo_lane)
            pending = upcoming

    @pl.when(j < i)
    def _():
        run(False)

    @pl.when(j == i)
    def _():
        run(True)
        _attn_finish(out_ref, gate_ref, acc_s, lo_lane)


def _prompt_attn(q, kb, cum_aug, vb, gate, blk):
    bsz, t, d = q.shape
    nq = t // blk
    rows = max(nq // 2, 1)
    assert nq == 1 or nq % 2 == 0
    qmap = lambda b, r, c: (b, _fold_block(r, c, nq)[0], 0)
    kvmap = lambda b, r, c: (b, _fold_block(r, c, nq)[1], 0)
    return pl.pallas_call(
        functools.partial(_prompt_attn_kernel, blk=blk, nq=nq),
        grid=(bsz, rows, nq + 1 if nq > 1 else 1),
        in_specs=[pl.BlockSpec((1, blk, d), qmap), pl.BlockSpec((1, blk, d), kvmap),
                  pl.BlockSpec((1, blk, cum_aug.shape[2]), kvmap), pl.BlockSpec((1, blk, d), kvmap),
                  pl.BlockSpec((1, blk, d), qmap)],
        out_specs=pl.BlockSpec((1, blk, d), qmap),
        out_shape=jax.ShapeDtypeStruct((bsz, t, d), BF16),
        scratch_shapes=[pltpu.VMEM((B_HEADS, blk, LANES), F32), pltpu.VMEM((B_HEADS, blk, LANES), F32)],
        compiler_params=_params("parallel", "arbitrary", "arbitrary"),
        name="prompt_attn",
    )(q, kb, cum_aug, vb, gate)


def _group_rows(x, tq):
    return jnp.concatenate([jnp.broadcast_to(x[r:r + 1, :], (tq, x.shape[1])) for r in range(x.shape[0])], axis=0)


def _lane_cumsum(x):
    r, n = x.shape
    nb = n // MXU_N
    stacked = jnp.concatenate([x[:, c * MXU_N:(c + 1) * MXU_N] for c in range(nb)], axis=0)
    upper = (lax.broadcasted_iota(jnp.int32, (MXU_N, MXU_N), 0)
             <= lax.broadcasted_iota(jnp.int32, (MXU_N, MXU_N), 1)).astype(BF16)
    within = sum(_dot(piece, upper) for piece in _split3(stacked))
    totals = jnp.broadcast_to(within[:, MXU_N - 1:MXU_N], (nb * r, LANES))
    ri = lax.broadcasted_iota(jnp.int32, (nb * r, nb * r), 0)
    ci = lax.broadcasted_iota(jnp.int32, (nb * r, nb * r), 1)
    earlier = jnp.logical_and(ci // r < ri // r, ci % r == ri % r).astype(BF16)
    offs = sum(_dot(earlier, piece) for piece in _split3(totals))
    full = within + jnp.concatenate([offs] * (MXU_N // LANES), axis=1)
    return [full[c * r:(c + 1) * r, :] for c in range(nb)]


def _sample_attn_kernel(q_ref, kt_ref, vt_ref, lf_ref, nk_ref, nv_ref, nlf_ref, gate_ref, out_ref,
                        cum_s, m_s, l_s, acc_s, *, tk):
    j = pl.program_id(1)
    tq = q_ref.shape[1]
    n_groups = acc_s.shape[0]
    hg = B_HEADS // n_groups
    gw = hg * B_HD
    rows = hg * tq
    row_head = lax.broadcasted_iota(jnp.int32, (rows, gw), 0) // tq
    lane_head = lax.broadcasted_iota(jnp.int32, (rows, gw), 1) // B_HD
    own = row_head == lane_head

    @pl.when(j == 0)
    def _():
        m_s[...] = jnp.full(m_s.shape, NEG_BIG, F32)
        l_s[...] = jnp.zeros(l_s.shape, F32)
        acc_s[...] = jnp.zeros(acc_s.shape, F32)
        blocks = _lane_cumsum(lf_ref[0])
        per_step = tk // MXU_N
        for c, blk in enumerate(blocks):
            cum_s[c // per_step, :, (c % per_step) * MXU_N:(c % per_step + 1) * MXU_N] = blk * LOG2E

    def q_block_diag(g):
        qg = q_ref[0, :, g * gw:(g + 1) * gw]
        qrep = jnp.concatenate([qg] * hg, axis=0)
        return jnp.where(own, qrep, jnp.zeros_like(qrep))

    def online_update(g, cols, pv_fn):
        top = functools.reduce(jnp.maximum, cols)
        m_old = m_s[g]
        m_new = jnp.maximum(m_old, jnp.max(top, axis=1, keepdims=True))
        alpha = jnp.exp2(m_old - m_new)
        probs = [jnp.exp2(col - m_new[:, 0:col.shape[1]]) for col in cols]
        if cols[0].shape[1] == LANES:
            psum = functools.reduce(jnp.add, probs)
        else:
            lane0 = lax.broadcasted_iota(jnp.int32, (rows, LANES), 1) == 0
            psum = jnp.where(lane0, jnp.sum(probs[0], axis=1, keepdims=True), 0.0)
        l_s[g] = alpha * l_s[g] + psum
        p = probs[0] if len(probs) == 1 else jnp.concatenate(probs, axis=1)
        acc_s[g] = jnp.concatenate([alpha] * (gw // LANES), axis=1) * acc_s[g] + pv_fn(p.astype(BF16))
        m_s[g] = m_new

    cum_blk = cum_s[j]
    for g in range(n_groups):
        kt = kt_ref[0, g * gw:(g + 1) * gw, :].astype(BF16)
        s = _dot(q_block_diag(g), kt) - _group_rows(cum_blk[g * hg:(g + 1) * hg, :], tq)
        cols = [s[:, c * LANES:(c + 1) * LANES] for c in range(tk // LANES)]
        vt = vt_ref[0, g * gw:(g + 1) * gw, :].astype(BF16)
        online_update(g, cols, lambda p, vt=vt: _dot_nt(p, vt))

    @pl.when(j == pl.num_programs(1) - 1)
    def _():
        t_q = lax.broadcasted_iota(jnp.int32, (rows, tq), 0) % tq
        t_k = lax.broadcasted_iota(jnp.int32, (rows, tq), 1)
        causal = t_k <= t_q
        upper = (lax.broadcasted_iota(jnp.int32, (tq, tq), 0)
                 <= lax.broadcasted_iota(jnp.int32, (tq, tq), 1)).astype(BF16)
        new_cum = sum(_dot(piece, upper) for piece in _split3(nlf_ref[0]))
        last = cum_s[pl.num_programs(1) - 1][:, tk - 1:tk]
        new_cum = new_cum * LOG2E + last
        lane_grp = lax.broadcasted_iota(jnp.int32, (tq, gw), 1) // B_HD
        for g in range(n_groups):
            kn = nk_ref[0, :, g * gw:(g + 1) * gw]
            s = _dot_nt(q_block_diag(g), kn) - _group_rows(new_cum[g * hg:(g + 1) * hg, :], tq)
            s = jnp.where(causal, s, NEG_BIG)
            vn = nv_ref[0, :, g * gw:(g + 1) * gw]
            online_update(g, [s], lambda p, vn=vn: _dot(p, vn))
            o = acc_s[g] / jnp.sum(l_s[g], axis=1, keepdims=True)
            picked = functools.reduce(
                jnp.add, [jnp.where(lane_grp == r, o[r * tq:(r + 1) * tq, :], 0.0) for r in range(hg)])
            gate = gate_ref[0, :, g * gw:(g + 1) * gw]
            out_ref[0, :, g * gw:(g + 1) * gw] = (picked * _sigmoid(gate)).astype(out_ref.dtype)


def _sample_attn(q, cache_kt, cache_vt, cache_logf_t, kb, vb, new_logf_t, gate, tk):
    bsz, tq, d = q.shape
    past = cache_kt.shape[2]
    n_groups = d // MXU_N
    one = lambda b, j: (b, 0, 0)
    blk = lambda b, j: (b, 0, j)
    return pl.pallas_call(
        functools.partial(_sample_attn_kernel, tk=tk),
        grid=(bsz, past // tk),
        in_specs=[pl.BlockSpec((1, tq, d), one), pl.BlockSpec((1, d, tk), blk), pl.BlockSpec((1, d, tk), blk),
                  pl.BlockSpec((1, B_HEADS, past), one),
                  pl.BlockSpec((1, tq, d), one), pl.BlockSpec((1, tq, d), one),
                  pl.BlockSpec((1, B_HEADS, tq), one), pl.BlockSpec((1, tq, d), one)],
        out_specs=pl.BlockSpec((1, tq, d), one),
        out_shape=jax.ShapeDtypeStruct((bsz, tq, d), BF16),
        scratch_shapes=[pltpu.VMEM((past // tk, B_HEADS, tk), F32),
                        pltpu.VMEM((n_groups, MXU_N // B_HD * tq, LANES), F32),
                        pltpu.VMEM((n_groups, MXU_N // B_HD * tq, LANES), F32),
                        pltpu.VMEM((n_groups, MXU_N // B_HD * tq, MXU_N), F32)],
        compiler_params=_params("parallel", "arbitrary"),
        name="sample_attn",
    )(q, cache_kt, cache_vt, cache_logf_t, kb, vb, new_logf_t, gate)


def _prep_weights(g_mix, g_ffn, w_in_a, b_i_a, b_f_a, g_head_a, w_out_a, g_kv, w_kvf, b_f_b, g_k,
                  w_qg_b, g_q_b, w_o_b, w_gu, w_down):
    d = g_mix.shape[1]
    main_w = 2 * A_HEADS * A_DQK + 2 * A_HEADS * A_DV
    w_in = w_in_a[0]
    gates_b = jnp.concatenate([b_i_a[0], b_f_a[0]])

    def ffn(layer):
        return w_gu[layer].astype(BF16), w_down[layer].astype(BF16)

    return dict(
        g_mix0=g_mix[0][None], g_mix1=g_mix[1][None], g_ffn0=g_ffn[0][None], g_ffn1=g_ffn[1][None],
        w_in=w_in[:, :main_w].astype(BF16), wg2=_hi_lo_cols(w_in[:, main_w:]),
        gb_col=gates_b[None, :], gb_row=gates_b[:, None], g_head=g_head_a[0][None],
        w_out=w_out_a[0].astype(BF16),
        g_kv=g_kv[None], w_kv=w_kvf[:, :2 * d].astype(BF16), wf2=_hi_lo_cols(w_kvf[:, 2 * d:]), b_f=b_f_b[None],
        gk_t=jnp.tile(g_k, B_HEADS)[None], gq_t=jnp.tile(g_q_b[0], B_HEADS)[None],
        w_qg=w_qg_b[0].astype(BF16), w_o=w_o_b[0].astype(BF16),
        ffn0=ffn(0), ffn1=ffn(1),
    )


def _state_to_pairs(c, n, m):
    bsz = c.shape[0]
    np_ = A_HEADS // 2
    ct = c.reshape(bsz, np_, 2, A_DV, A_DQK).transpose(0, 1, 2, 4, 3).reshape(bsz, np_, 2 * A_DQK, A_DV)
    n2 = n.reshape(bsz, np_, 1, 2 * A_DQK)
    m2 = jnp.broadcast_to(m[:, :, None, None], (bsz, A_HEADS, 1, LANES))
    return ct.astype(F32), n2.astype(F32), m2.astype(F32)


def _state_from_pairs(ct, n2, m2):
    bsz = ct.shape[0]
    np_ = A_HEADS // 2
    c = ct.reshape(bsz, np_, 2, A_DQK, A_DV).transpose(0, 1, 2, 4, 3).reshape(bsz, 1, A_HEADS, A_DV, A_DQK)
    n = n2.reshape(bsz, 1, A_HEADS, A_DQK)
    m = m2[:, :, 0, 0].reshape(bsz, 1, A_HEADS)
    return c, n, m


def _layer_a(x, c0, n0, m0, w, chunk, tm):
    bsz, t, d = x.shape
    x2d = x.reshape(bsz * t, d)
    q, k, v, o, gates = _in_proj(x2d, w["g_mix0"], w["w_in"], w["wg2"], tm)
    r3 = lambda z: z.reshape(bsz, t, z.shape[1])
    gates3 = r3(gates)
    ct0, n20, m20 = _state_to_pairs(c0, n0, m0)
    hn, ct, n2, m2 = _mlstm(r3(q), r3(k), r3(v), r3(o), gates3, gates3.transpose(0, 2, 1),
                            w["gb_col"], w["gb_row"], w["g_head"], ct0, n20, m20, chunk)
    x2 = _mix_ffn(hn.reshape(bsz * t, -1), x2d, w["w_out"], w["g_ffn0"], *w["ffn0"], tm)
    return x2, _state_from_pairs(ct, n2, m2)


def _fox_inputs(x2, w, tm, seq_t=None):
    return _fox_proj(x2, w["g_kv"], w["g_mix1"], w["w_kv"], w["wf2"], w["w_qg"],
                     w["b_f"], w["gk_t"], w["gq_t"], tm, seq_t)


def _prompt_trunk(x, w):
    bsz, t, d = x.shape
    tm = 512
    zeros = lambda *s: jnp.zeros(s, F32)
    x2, (c, n, m) = _layer_a(x, zeros(bsz, A_HEADS, A_DV, A_DQK), zeros(bsz, A_HEADS, A_DQK),
                             zeros(bsz, A_HEADS), w, min(t, 128), tm)
    kt, vt, logf, kb, vb, q, gate = _fox_inputs(x2, w, tm, seq_t=t)
    r3 = lambda z: z.reshape(bsz, t, z.shape[1])
    logf3 = r3(logf)
    cum_aug = _cum_bias(logf3, min(t, 512))
    attn = _prompt_attn(r3(q), r3(kb), cum_aug, r3(vb), r3(gate), min(t, 512))
    y = _mix_ffn(attn.reshape(bsz * t, d), x2, w["w_o"], w["g_ffn1"], *w["ffn1"], tm)
    to_out = lambda z: z.reshape(bsz, B_HEADS, B_HD, t).transpose(0, 3, 1, 2)
    return (y.reshape(bsz, t, d), to_out(kt), to_out(vt), logf3, c, n, m)


def _sample_trunk(x, c0, n0, m0, cache_k, cache_v, cache_logf, w):
    bsz, t, d = x.shape
    past = cache_k.shape[1]
    tm = min(bsz * t, 512)
    x2, (c, n, m) = _layer_a(x, c0[:, 0], n0[:, 0], m0[:, 0], w, t, tm)
    k, v, logf, kb, vb, q, gate = _fox_inputs(x2, w, tm)
    r3 = lambda z: z.reshape(bsz, t, z.shape[1])
    logf3 = r3(logf)
    kt = cache_k.transpose(0, 2, 3, 1).reshape(bsz, d, past)
    vt = cache_v.transpose(0, 2, 3, 1).reshape(bsz, d, past)
    attn = _sample_attn(r3(q), kt, vt, cache_logf.transpose(0, 2, 1), r3(kb), r3(vb),
                        logf3.transpose(0, 2, 1), r3(gate), min(past, 2048))
    y = _mix_ffn(attn.reshape(bsz * t, d), x2, w["w_o"], w["g_ffn1"], *w["ffn1"], tm)
    return (y.reshape(bsz, t, d), k.reshape(bsz, t, B_HEADS, B_HD), v.reshape(bsz, t, B_HEADS, B_HD),
            logf3, c, n, m)


def kernel(x_prompt, x_sample, cache_k, cache_v, cache_logf, state_c, state_n, state_m, g_mix, g_ffn, w_in_a, b_i_a, b_f_a, g_head_a, w_out_a, g_kv, w_kvf, b_f_b, g_k, w_qg_b, g_q_b, w_o_b, w_gu, w_down):
    w = _prep_weights(g_mix, g_ffn, w_in_a, b_i_a, b_f_a, g_head_a, w_out_a, g_kv, w_kvf, b_f_b, g_k,
                      w_qg_b, g_q_b, w_o_b, w_gu, w_down)
    y_p, k_p, v_p, f_p, c_p, n_p, m_p = _prompt_trunk(x_prompt, w)
    y_s, k_s, v_s, f_s, c_s, n_s, m_s = _sample_trunk(x_sample, state_c, state_n, state_m,
                                                      cache_k, cache_v, cache_logf, w)
    return (y_p, y_s, k_p, v_p, f_p, c_p, n_p, m_p, k_s, v_s, f_s, c_s, n_s, m_s)
```

```python
import functools

import jax
import jax.numpy as jnp
from jax import lax
from jax.experimental import pallas as pl
from jax.experimental.pallas import tpu as pltpu

F32 = jnp.float32
BF16 = jnp.bfloat16

EPS = 1e-6
GATE_CAP = 15.0
A_HEADS = 8
A_DQK = 64
A_DV = 128
B_HEADS = 16
B_HD = 64
LANES = 128
HALF = 64
MXU_N = 256
FF_CHUNK = MXU_N
NEG_BIG = -1e30
LOG2E = 1.4426950408889634
N_BIAS_PIECES = 3
VMEM_LIMIT = 56 * 1024 * 1024


def _params(*sem):
    return pltpu.CompilerParams(dimension_semantics=sem, vmem_limit_bytes=VMEM_LIMIT)


def _resident(shape):
    nd = len(shape)
    return pl.BlockSpec(shape, lambda *_: (0,) * nd, pipeline_mode=pl.Buffered(1))


def _dot(a, b):
    return jnp.dot(a, b, preferred_element_type=F32)


def _dot_nt(a, b):
    return lax.dot_general(a, b, (((1,), (1,)), ((), ())), preferred_element_type=F32)


def _split2(x):
    hi = x.astype(BF16)
    lo = (x - hi.astype(F32)).astype(BF16)
    return hi, lo


def _split3(x):
    hi = x.astype(BF16)
    r = x - hi.astype(F32)
    mid = r.astype(BF16)
    lo = (r - mid.astype(F32)).astype(BF16)
    return hi, mid, lo


def _hi_lo_cols(w):
    hi, lo = _split2(w)
    both = jnp.concatenate([hi, lo], axis=1)
    return jnp.pad(both, ((0, 0), (0, LANES - both.shape[1])))


def _log_sigmoid(x):
    return jnp.minimum(x, 0.0) - jnp.log(1.0 + jnp.exp(-jnp.abs(x)))


def _sigmoid(x):
    return 1.0 / (1.0 + jnp.exp(-x))


def _soft_cap(z):
    return GATE_CAP * jnp.tanh(z / GATE_CAP)


def _rms_hat(x):
    return x * lax.rsqrt(jnp.mean(x * x, axis=-1, keepdims=True) + EPS)


def _pair_select(lo_mask, a, b):
    return jnp.where(lo_mask, a, b)


def _head_norm_group(xc, gc, lo_mask):
    sq = xc * xc
    s_lo = jnp.sum(jnp.where(lo_mask, sq, 0.0), axis=-1, keepdims=True)
    s_hi = jnp.sum(jnp.where(lo_mask, 0.0, sq), axis=-1, keepdims=True)
    ms = jnp.where(lo_mask, s_lo, s_hi) * (1.0 / B_HD)
    return xc * lax.rsqrt(ms + EPS) * gc


def _in_proj_kernel(x_ref, g_ref, w_ref, wg2_ref, q_ref, k_ref, v_ref, o_ref, gates_ref):
    xh = (_rms_hat(x_ref[...]) * g_ref[...]).astype(BF16)
    qk_w = A_HEADS * A_DQK
    v_w = A_HEADS * A_DV
    q_ref[...] = _dot(xh, w_ref[:, 0:qk_w]).astype(BF16)
    k_ref[...] = (_dot(xh, w_ref[:, qk_w:2 * qk_w]) * (A_DQK ** -0.5)).astype(BF16)
    v_ref[...] = _dot(xh, w_ref[:, 2 * qk_w:2 * qk_w + v_w]).astype(BF16)
    o_ref[...] = _dot(xh, w_ref[:, 2 * qk_w + v_w:2 * qk_w + 2 * v_w])
    gz = _dot(xh, wg2_ref[...])
    gates_ref[...] = gz[:, 0:2 * A_HEADS] + gz[:, 2 * A_HEADS:4 * A_HEADS]


def _in_proj(x2d, g, w_main, wg2, tm):
    m, d = x2d.shape
    qk_w = A_HEADS * A_DQK
    v_w = A_HEADS * A_DV
    row = lambda i: (i, 0)
    return pl.pallas_call(
        _in_proj_kernel,
        grid=(m // tm,),
        in_specs=[pl.BlockSpec((tm, d), row), _resident(g.shape), _resident(w_main.shape),
                  _resident(wg2.shape)],
        out_specs=[pl.BlockSpec((tm, qk_w), row), pl.BlockSpec((tm, qk_w), row),
                   pl.BlockSpec((tm, v_w), row), pl.BlockSpec((tm, v_w), row),
                   pl.BlockSpec((tm, 2 * A_HEADS), row)],
        out_shape=[jax.ShapeDtypeStruct((m, qk_w), BF16), jax.ShapeDtypeStruct((m, qk_w), BF16),
                   jax.ShapeDtypeStruct((m, v_w), BF16), jax.ShapeDtypeStruct((m, v_w), F32),
                   jax.ShapeDtypeStruct((m, 2 * A_HEADS), F32)],
        compiler_params=_params("parallel"),
        name="in_proj",
    )(x2d, g, w_main, wg2)


def _mlstm_kernel(q_ref, k_ref, v_ref, o_ref, gc_ref, gr_ref, bc_ref, br_ref, gh_ref,
                  c0_ref, n0_ref, m0_ref,
                  h_ref, c_out, n_out, m_out, ct_s, n_s, m_s, *, chunk):
    j = pl.program_id(1)
    L = chunk
    n_streams = q_ref.shape[0]

    @pl.when(j == 0)
    def _():
        ct_s[...] = c0_ref[...]
        n_s[...] = n0_ref[...]
        m_s[...] = m0_ref[...]

    t_idx = lax.broadcasted_iota(jnp.int32, (L, L), 0)
    s_idx = lax.broadcasted_iota(jnp.int32, (L, L), 1)
    causal = s_idx <= t_idx
    lane = lax.broadcasted_iota(jnp.int32, (1, LANES), 1)
    lo_lane = lane < HALF
    lo_row = lax.broadcasted_iota(jnp.int32, (LANES, LANES), 0) < HALF
    eye = (lax.broadcasted_iota(jnp.int32, (LANES, LANES), 0)
           == lax.broadcasted_iota(jnp.int32, (LANES, LANES), 1)).astype(BF16)
    place = (lax.broadcasted_iota(jnp.int32, (2 * A_HEADS, 2 * A_HEADS * LANES), 0)
             == lax.broadcasted_iota(jnp.int32, (2 * A_HEADS, 2 * A_HEADS * LANES), 1) // LANES).astype(BF16)

    def rep(col):
        return jnp.broadcast_to(col, (col.shape[0], LANES))

    def head_stages(b, h, gates, q2, k2, kt2, ct2b, qn, done):
        cols_rep, li_r, lf_r = gates
        half = lo_lane if h % 2 == 0 else jnp.logical_not(lo_lane)
        li_col = cols_rep[:, h * LANES:(h + 1) * LANES]
        lf_col = cols_rep[:, (A_HEADS + h) * LANES:(A_HEADS + h + 1) * LANES]
        li_row = li_r[h:h + 1, :]
        lf_row = lf_r[h:h + 1, :]
        m_prev = m_s[b, h]
        v_h = v_ref[b, :, h * A_DV:(h + 1) * A_DV]
        qm = jnp.where(half, q2, jnp.zeros_like(q2))
        qk = _dot_nt(qm, k2)
        qc = _dot(qm, ct2b)
        qn_sum = rep(jnp.sum(jnp.where(half, qn, 0.0), axis=1, keepdims=True))
        b_col = rep(jnp.sum(jnp.where(causal, lf_row, 0.0), axis=1, keepdims=True))
        b_row = jnp.sum(jnp.where(t_idx <= s_idx, lf_col[:, :L], 0.0), axis=0, keepdims=True)
        b_last = rep(jnp.sum(lf_row, axis=1, keepdims=True))
        yield
        dmat = jnp.where(causal, b_col[:, :L] - b_row + li_row, NEG_BIG)
        inter = b_col + m_prev
        m_t = jnp.maximum(inter, rep(jnp.max(dmat, axis=1, keepdims=True)))
        w_col = b_last - b_col + li_col
        m_new = jnp.maximum(b_last + m_prev, jnp.max(w_col, axis=0, keepdims=True))
        yield
        g = jnp.exp(inter - m_t)
        dec = jnp.where(causal, jnp.exp(dmat - m_t[:, :L]), 0.0)
        floor = jnp.exp(-m_t)
        g_c = jnp.exp(b_last + m_prev - m_new)
        ws = jnp.exp(w_col - m_new)
        yield
        s = qk * dec
        wsv = (ws * v_h.astype(F32)).astype(BF16)
        upd = _dot(kt2, wsv)
        num = g * qc + _dot(s.astype(BF16), v_h)
        den = g * qn_sum + rep(jnp.sum(s, axis=1, keepdims=True))
        yield
        hv = num / jnp.maximum(jnp.abs(den), floor)
        ms = rep(jnp.mean(hv * hv, axis=-1, keepdims=True))
        yield
        hv = hv * lax.rsqrt(ms + EPS)
        hv = hv * gh_ref[:, h * A_DV:(h + 1) * A_DV]
        hv = hv * _sigmoid(o_ref[b, :, h * A_DV:(h + 1) * A_DV])
        h_ref[b, :, h * A_DV:(h + 1) * A_DV] = hv.astype(h_ref.dtype)
        m_s[b, h] = m_new
        done[b, h] = (upd, g_c, ws)

    pairs = []
    done = {}
    gens = []
    for b in range(n_streams):
        cap_c = _soft_cap(gc_ref[b] + bc_ref[...])
        cap_r = _soft_cap(gr_ref[b] + br_ref[...])
        li_c = cap_c[:, 0:A_HEADS]
        lf_c = _log_sigmoid(cap_c[:, A_HEADS:2 * A_HEADS])
        cols_rep = sum(_dot(piece, place) for piece in _split3(jnp.concatenate([li_c, lf_c], axis=1)))
        gates = (cols_rep, cap_r[0:A_HEADS, :], _log_sigmoid(cap_r[A_HEADS:2 * A_HEADS, :]))
        for p in range(A_HEADS // 2):
            q2 = q_ref[b, :, p * LANES:(p + 1) * LANES]
            k2 = k_ref[b, :, p * LANES:(p + 1) * LANES]
            kt2 = _dot_nt(eye, k2).astype(BF16)
            ct2 = ct_s[b, p]
            n2 = n_s[b, p]
            pairs.append((b, p, k2, ct2, n2))
            for hh in range(2):
                gens.append(head_stages(b, 2 * p + hh, gates, q2, k2, kt2, ct2.astype(BF16),
                                        q2.astype(F32) * n2, done))
    while gens:
        alive = []
        for gen in gens:
            try:
                next(gen)
                alive.append(gen)
            except StopIteration:
                pass
        gens = alive

    for b, p, k2, ct2, n2 in pairs:
        (upd0, gc0, ws0), (upd1, gc1, ws1) = done[b, 2 * p], done[b, 2 * p + 1]
        ct_s[b, p] = jnp.where(lo_row, gc0, gc1) * ct2 + jnp.where(lo_row, upd0, upd1)
        ws2 = jnp.where(lo_lane, ws0, ws1)
        n_s[b, p] = (jnp.where(lo_lane, gc0, gc1) * n2
                     + jnp.sum(ws2 * k2.astype(F32), axis=0, keepdims=True))

    @pl.when(j == pl.num_programs(1) - 1)
    def _():
        c_out[...] = ct_s[...]
        n_out[...] = n_s[...]
        m_out[...] = m_s[...]


def _mlstm(q, k, v, o, gates, gates_t, b_col, b_row, g_head, ct0, n0, m0, chunk, streams):
    bsz, t, _ = q.shape
    nc = t // chunk
    np_ = A_HEADS // 2
    sb = streams
    tok = lambda b, j: (b, j, 0)
    st4 = lambda b, j: (b, 0, 0, 0)
    return pl.pallas_call(
        functools.partial(_mlstm_kernel, chunk=chunk),
        grid=(bsz // sb, nc),
        in_specs=[pl.BlockSpec((sb, chunk, q.shape[2]), tok), pl.BlockSpec((sb, chunk, k.shape[2]), tok),
                  pl.BlockSpec((sb, chunk, v.shape[2]), tok), pl.BlockSpec((sb, chunk, o.shape[2]), tok),
                  pl.BlockSpec((sb, chunk, 2 * A_HEADS), tok),
                  pl.BlockSpec((sb, 2 * A_HEADS, chunk), lambda b, j: (b, 0, j)),
                  _resident(b_col.shape), _resident(b_row.shape), _resident(g_head.shape),
                  pl.BlockSpec((sb, np_, LANES, A_DV), st4), pl.BlockSpec((sb, np_, 1, LANES), st4),
                  pl.BlockSpec((sb, A_HEADS, 1, LANES), st4)],
        out_specs=[pl.BlockSpec((sb, chunk, v.shape[2]), tok),
                   pl.BlockSpec((sb, np_, LANES, A_DV), st4), pl.BlockSpec((sb, np_, 1, LANES), st4),
                   pl.BlockSpec((sb, A_HEADS, 1, LANES), st4)],
        out_shape=[jax.ShapeDtypeStruct((bsz, t, v.shape[2]), BF16),
                   jax.ShapeDtypeStruct((bsz, np_, LANES, A_DV), F32),
                   jax.ShapeDtypeStruct((bsz, np_, 1, LANES), F32),
                   jax.ShapeDtypeStruct((bsz, A_HEADS, 1, LANES), F32)],
        scratch_shapes=[pltpu.VMEM((sb, np_, LANES, A_DV), F32), pltpu.VMEM((sb, np_, 1, LANES), F32),
                        pltpu.VMEM((sb, A_HEADS, 1, LANES), F32)],
        compiler_params=_params("parallel", "arbitrary"),
        name="mlstm",
    )(q, k, v, o, gates, gates_t, b_col, b_row, g_head, ct0, n0, m0)


def _mix_ffn_kernel(a_ref, res_ref, wo_ref, g_ref, wgu_ref, wd_ref, out_ref, xn_s, acc_s):
    x1 = res_ref[...] + _dot(a_ref[...], wo_ref[...])
    xn_s[...] = (_rms_hat(x1) * g_ref[...]).astype(BF16)
    acc_s[...] = x1

    d_ff = wd_ref.shape[0]
    for c in range(d_ff // FF_CHUNK):
        lo, hi = c * FF_CHUNK, (c + 1) * FF_CHUNK
        xn = xn_s[...]
        gate = _dot(xn, wgu_ref[:, lo:hi])
        up = _dot(xn, wgu_ref[:, d_ff + lo:d_ff + hi])
        act = (gate * _sigmoid(gate) * up).astype(BF16)
        acc_s[...] += _dot(act, wd_ref[lo:hi, :])
    out_ref[...] = acc_s[...]


def _mix_ffn(a, res, w_o, g, wgu, wd, tm):
    m, d = res.shape
    row = lambda i: (i, 0)
    return pl.pallas_call(
        _mix_ffn_kernel,
        grid=(m // tm,),
        in_specs=[pl.BlockSpec((tm, a.shape[1]), row), pl.BlockSpec((tm, d), row),
                  _resident(w_o.shape), _resident(g.shape), _resident(wgu.shape), _resident(wd.shape)],
        out_specs=pl.BlockSpec((tm, d), row),
        out_shape=jax.ShapeDtypeStruct((m, d), F32),
        scratch_shapes=[pltpu.VMEM((tm, d), BF16), pltpu.VMEM((tm, d), F32)],
        compiler_params=_params("parallel"),
        name="mix_ffn",
    )(a, res, w_o, g, wgu, wd)


def _fox_proj_kernel(x_ref, gkv_ref, gmix_ref, wkv_ref, wf2_ref, wqg_ref, bf_ref, gk_ref, gq_ref,
                     k_ref, v_ref, logf_ref, kb_ref, vb_ref, q_ref, gate_ref):
    d = x_ref.shape[1]
    xhat = _rms_hat(x_ref[...])
    sh = (xhat * gkv_ref[...]).astype(BF16)
    hq = (xhat * gmix_ref[...]).astype(BF16)
    lo_lane = lax.broadcasted_iota(jnp.int32, (1, LANES), 1) < HALF

    fz = _dot(sh, wf2_ref[...])
    logf_ref[...] = _log_sigmoid(fz[:, 0:B_HEADS] + fz[:, B_HEADS:2 * B_HEADS] + bf_ref[...])

    for c in range(d // MXU_N):
        kraw = _dot(sh, wkv_ref[:, c * MXU_N:(c + 1) * MXU_N])
        qraw = _dot(hq, wqg_ref[:, c * MXU_N:(c + 1) * MXU_N])
        for u in range(MXU_N // LANES):
            us = slice(u * LANES, (u + 1) * LANES)
            cs = slice(c * MXU_N + u * LANES, c * MXU_N + (u + 1) * LANES)
            kn = _head_norm_group(kraw[:, us], gk_ref[:, cs], lo_lane)
            if k_ref.ndim == 3:
                k_ref[0, cs, :] = kn.T
            else:
                k_ref[:, cs] = kn
            kb_ref[:, cs] = kn.astype(BF16)
            qn = _head_norm_group(qraw[:, us], gq_ref[:, cs], lo_lane)
            q_ref[:, cs] = (qn * (LOG2E * B_HD ** -0.5)).astype(BF16)
    for c in range(d // MXU_N):
        cs = slice(c * MXU_N, (c + 1) * MXU_N)
        v = _dot(sh, wkv_ref[:, d + c * MXU_N:d + (c + 1) * MXU_N])
        if v_ref.ndim == 3:
            v_ref[0, cs, :] = v.T
        else:
            v_ref[:, cs] = v
        vb_ref[:, cs] = v.astype(BF16)
    gate_ref[...] = _dot(hq, wqg_ref[:, d:2 * d])


def _fox_proj(x2d, g_kv, g_mix, w_kv, wf2, w_qg, b_f, gk_t, gq_t, tm, seq_t=None):
    m, d = x2d.shape
    row = lambda i: (i, 0)
    big = pl.BlockSpec((tm, d), row)
    if seq_t is None:
        kv_spec, kv_shape = big, jax.ShapeDtypeStruct((m, d), F32)
    else:
        per_seq = seq_t // tm
        kv_spec = pl.BlockSpec((1, d, tm), lambda i: (i // per_seq, 0, i % per_seq))
        kv_shape = jax.ShapeDtypeStruct((m // seq_t, d, seq_t), F32)
    return pl.pallas_call(
        _fox_proj_kernel,
        grid=(m // tm,),
        in_specs=[big] + [_resident(a.shape) for a in (g_kv, g_mix, w_kv, wf2, w_qg, b_f, gk_t, gq_t)],
        out_specs=[kv_spec, kv_spec, pl.BlockSpec((tm, B_HEADS), row), big, big, big, big],
        out_shape=[kv_shape, kv_shape,
                   jax.ShapeDtypeStruct((m, B_HEADS), F32),
                   jax.ShapeDtypeStruct((m, d), BF16), jax.ShapeDtypeStruct((m, d), BF16),
                   jax.ShapeDtypeStruct((m, d), BF16), jax.ShapeDtypeStruct((m, d), F32)],
        compiler_params=_params("parallel"),
        name="fox_proj",
    )(x2d, g_kv, g_mix, w_kv, wf2, w_qg, b_f, gk_t, gq_t)


def _cum_bias_kernel(x_ref, aug_ref, carry_s):
    @pl.when(pl.program_id(1) == 0)
    def _():
        carry_s[...] = jnp.zeros(carry_s.shape, F32)

    tb = x_ref.shape[1]
    tri = (lax.broadcasted_iota(jnp.int32, (tb, tb), 1)
           <= lax.broadcasted_iota(jnp.int32, (tb, tb), 0)).astype(BF16)
    hi, mid, lo = _split3(x_ref[0])
    cum = _dot(tri, hi) + _dot(tri, mid) + _dot(tri, lo) + carry_s[...]
    carry_s[...] = cum[tb - 1:tb, :]

    h_idx = lax.broadcasted_iota(jnp.int32, (B_HEADS, aug_ref.shape[2]), 0)
    c_idx = lax.broadcasted_iota(jnp.int32, (B_HEADS, aug_ref.shape[2]), 1)
    base = N_BIAS_PIECES * h_idx
    aug = None
    for i, piece in enumerate(_split3(cum * (-LOG2E))):
        term = _dot(piece, (c_idx == base + i).astype(BF16))
        aug = term if aug is None else aug + term
    aug_ref[0] = aug.astype(BF16)


def _cum_bias(logf, tb):
    bsz, t, h = logf.shape
    width = LANES
    assert N_BIAS_PIECES * h <= width
    return pl.pallas_call(
        _cum_bias_kernel,
        grid=(bsz, t // tb),
        in_specs=[pl.BlockSpec((1, tb, h), lambda b, j: (b, j, 0))],
        out_specs=pl.BlockSpec((1, tb, width), lambda b, j: (b, j, 0)),
        out_shape=jax.ShapeDtypeStruct((bsz, t, width), BF16),
        scratch_shapes=[pltpu.VMEM((1, h), F32)],
        compiler_params=_params("parallel", "arbitrary"),
        name="cum_bias",
    )(logf)


def _attn_head_scores(h, q2, k2, ca, mask, m_s, lo_lane):
    tq = q2.shape[0]
    half = lo_lane if h % 2 == 0 else jnp.logical_not(lo_lane)
    qm = jnp.where(half, q2, jnp.zeros_like(q2))
    lane2 = lax.broadcasted_iota(jnp.int32, (tq, LANES), 1)
    pick = jnp.logical_and(lane2 >= N_BIAS_PIECES * h, lane2 < N_BIAS_PIECES * (h + 1))
    s = _dot_nt(jnp.concatenate([qm, pick.astype(BF16)], axis=1),
                jnp.concatenate([k2, ca], axis=1))
    cols = [s[:, c * LANES:(c + 1) * LANES] for c in range(s.shape[1] // LANES)]
    if mask is not None:
        cols = [jnp.where(mask[:, c * LANES:(c + 1) * LANES], col, NEG_BIG) for c, col in enumerate(cols)]
    top = functools.reduce(jnp.maximum, cols)
    m_old = m_s[h]
    m_new = jnp.maximum(m_old, jnp.max(top, axis=1, keepdims=True))
    return cols, m_old, m_new


def _attn_head_update(h, cols, m_old, m_new, v2, m_s, acc_s, lo_lane):
    half = lo_lane if h % 2 == 0 else jnp.logical_not(lo_lane)
    alpha = jnp.exp2(m_old - m_new)
    pr = jnp.concatenate([jnp.exp2(col - m_new).astype(BF16) for col in cols], axis=1)
    v_ones = jnp.where(half, v2, jnp.ones_like(v2))
    acc_s[h] = alpha * acc_s[h] + _dot(pr, v_ones)
    m_s[h] = m_new


def _attn_init(m_s, acc_s):
    m_s[...] = jnp.full(m_s.shape, NEG_BIG, F32)
    acc_s[...] = jnp.zeros(acc_s.shape, F32)


def _attn_finish(out_ref, gate_ref, acc_s, lo_lane):
    for p in range(B_HEADS // 2):
        a0, a1 = acc_s[2 * p], acc_s[2 * p + 1]
        num = jnp.where(lo_lane, a0, a1)
        den = pltpu.roll(jnp.where(lo_lane, a1, a0), HALF, axis=1)
        gate = gate_ref[0, :, p * LANES:(p + 1) * LANES]
        out_ref[0, :, p * LANES:(p + 1) * LANES] = (num / den * _sigmoid(gate)).astype(out_ref.dtype)


def _fold_block(r, c, nq):
    first = c <= r
    return jnp.where(first, r, nq - 1 - r), jnp.where(first, c, c - r - 1)


def _prompt_attn_kernel(q_ref, k_ref, ca_ref, v_ref, gate_ref, out_ref, m_s, acc_s, *, blk, nq):
    i, j = _fold_block(pl.program_id(1), pl.program_id(2), nq)
    lo_lane = lax.broadcasted_iota(jnp.int32, (1, LANES), 1) < HALF

    @pl.when(j == 0)
    def _():
        _attn_init(m_s, acc_s)

    def run(masked):
        mask = None
        if masked:
            mask = (lax.broadcasted_iota(jnp.int32, (blk, blk), 1)
                    <= lax.broadcasted_iota(jnp.int32, (blk, blk), 0))
        ca = ca_ref[0]

        def scores(h):
            cs = slice((h // 2) * LANES, (h // 2 + 1) * LANES)
            return _attn_head_scores(h, q_ref[0, :, cs], k_ref[0, :, cs], ca, mask, m_s, lo_lane)

        pending = scores(0)
        for h in range(B_HEADS):
            upcoming = scores(h + 1) if h + 1 < B_HEADS else None
            cs = slice((h // 2) * LANES, (h // 2 + 1) * LANES)
            _attn_head_update(h, *pending, v_ref[0, :, cs], m_s, acc_s, lo_lane)
            pending = upcoming

    @pl.when(j < i)
    def _():
        run(False)

    @pl.when(j == i)
    def _():
        run(True)
        _attn_finish(out_ref, gate_ref, acc_s, lo_lane)


def _prompt_attn(q, kb, cum_aug, vb, gate, blk):
    bsz, t, d = q.shape
    nq = t // blk
    rows = max(nq // 2, 1)
    assert nq == 1 or nq % 2 == 0
    qmap = lambda b, r, c: (b, _fold_block(r, c, nq)[0], 0)
    kvmap = lambda b, r, c: (b, _fold_block(r, c, nq)[1], 0)
    return pl.pallas_call(
        functools.partial(_prompt_attn_kernel, blk=blk, nq=nq),
        grid=(bsz, rows, nq + 1 if nq > 1 else 1),
        in_specs=[pl.BlockSpec((1, blk, d), qmap), pl.BlockSpec((1, blk, d), kvmap),
                  pl.BlockSpec((1, blk, cum_aug.shape[2]), kvmap), pl.BlockSpec((1, blk, d), kvmap),
                  pl.BlockSpec((1, blk, d), qmap)],
        out_specs=pl.BlockSpec((1, blk, d), qmap),
        out_shape=jax.ShapeDtypeStruct((bsz, t, d), BF16),
        scratch_shapes=[pltpu.VMEM((B_HEADS, blk, LANES), F32), pltpu.VMEM((B_HEADS, blk, LANES), F32)],
        compiler_params=_params("parallel", "arbitrary", "arbitrary"),
        name="prompt_attn",
    )(q, kb, cum_aug, vb, gate)


def _group_rows(x, tq):
    return jnp.concatenate([jnp.broadcast_to(x[r:r + 1, :], (tq, x.shape[1])) for r in range(x.shape[0])], axis=0)


def _lane_cumsum(x):
    r, n = x.shape
    nb = n // MXU_N
    stacked = jnp.concatenate([x[:, c * MXU_N:(c + 1) * MXU_N] for c in range(nb)], axis=0)
    upper = (lax.broadcasted_iota(jnp.int32, (MXU_N, MXU_N), 0)
             <= lax.broadcasted_iota(jnp.int32, (MXU_N, MXU_N), 1)).astype(BF16)
    within = sum(_dot(piece, upper) for piece in _split3(stacked))
    totals = jnp.broadcast_to(within[:, MXU_N - 1:MXU_N], (nb * r, LANES))
    ri = lax.broadcasted_iota(jnp.int32, (nb * r, nb * r), 0)
    ci = lax.broadcasted_iota(jnp.int32, (nb * r, nb * r), 1)
    earlier = jnp.logical_and(ci // r < ri // r, ci % r == ri % r).astype(BF16)
    offs = sum(_dot(earlier, piece) for piece in _split3(totals))
    full = within + jnp.concatenate([offs] * (MXU_N // LANES), axis=1)
    return [full[c * r:(c + 1) * r, :] for c in range(nb)]


def _sample_attn_kernel(q_ref, kt_ref, vt_ref, lf_ref, nk_ref, nv_ref, nlf_ref, gate_ref, out_ref,
                        cum_s, m_s, l_s, acc_s, *, tk):
    j = pl.program_id(1)
    tq = q_ref.shape[1]
    n_groups = acc_s.shape[0]
    hg = B_HEADS // n_groups
    gw = hg * B_HD
    rows = hg * tq
    row_head = lax.broadcasted_iota(jnp.int32, (rows, gw), 0) // tq
    lane_head = lax.broadcasted_iota(jnp.int32, (rows, gw), 1) // B_HD
    own = row_head == lane_head

    @pl.when(j == 0)
    def _():
        m_s[...] = jnp.full(m_s.shape, NEG_BIG, F32)
        l_s[...] = jnp.zeros(l_s.shape, F32)
        acc_s[...] = jnp.zeros(acc_s.shape, F32)
        blocks = _lane_cumsum(lf_ref[0])
        per_step = tk // MXU_N
        for c, blk in enumerate(blocks):
            cum_s[c // per_step, :, (c % per_step) * MXU_N:(c % per_step + 1) * MXU_N] = blk * LOG2E

    def q_block_diag(g):
        qg = q_ref[0, :, g * gw:(g + 1) * gw]
        qrep = jnp.concatenate([qg] * hg, axis=0)
        return jnp.where(own, qrep, jnp.zeros_like(qrep))

    def online_update(g, cols, pv_fn):
        top = functools.reduce(jnp.maximum, cols)
        m_old = m_s[g]
        m_new = jnp.maximum(m_old, jnp.max(top, axis=1, keepdims=True))
        alpha = jnp.exp2(m_old - m_new)
        probs = [jnp.exp2(col - m_new[:, 0:col.shape[1]]) for col in cols]
        if cols[0].shape[1] == LANES:
            psum = functools.reduce(jnp.add, probs)
        else:
            lane0 = lax.broadcasted_iota(jnp.int32, (rows, LANES), 1) == 0
            psum = jnp.where(lane0, jnp.sum(probs[0], axis=1, keepdims=True), 0.0)
        l_s[g] = alpha * l_s[g] + psum
        p = probs[0] if len(probs) == 1 else jnp.concatenate(probs, axis=1)
        acc_s[g] = jnp.concatenate([alpha] * (gw // LANES), axis=1) * acc_s[g] + pv_fn(p.astype(BF16))
        m_s[g] = m_new

    cum_blk = cum_s[j]
    for g in range(n_groups):
        kt = kt_ref[0, g * gw:(g + 1) * gw, :].astype(BF16)
        s = _dot(q_block_diag(g), kt) - _group_rows(cum_blk[g * hg:(g + 1) * hg, :], tq)
        cols = [s[:, c * LANES:(c + 1) * LANES] for c in range(tk // LANES)]
        vt = vt_ref[0, g * gw:(g + 1) * gw, :].astype(BF16)
        online_update(g, cols, lambda p, vt=vt: _dot_nt(p, vt))

    @pl.when(j == pl.num_programs(1) - 1)
    def _():
        t_q = lax.broadcasted_iota(jnp.int32, (rows, tq), 0) % tq
        t_k = lax.broadcasted_iota(jnp.int32, (rows, tq), 1)
        causal = t_k <= t_q
        upper = (lax.broadcasted_iota(jnp.int32, (tq, tq), 0)
                 <= lax.broadcasted_iota(jnp.int32, (tq, tq), 1)).astype(BF16)
        new_cum = sum(_dot(piece, upper) for piece in _split3(nlf_ref[0]))
        last = cum_s[pl.num_programs(1) - 1][:, tk - 1:tk]
        new_cum = new_cum * LOG2E + last
        lane_grp = lax.broadcasted_iota(jnp.int32, (tq, gw), 1) // B_HD
        for g in range(n_groups):
            kn = nk_ref[0, :, g * gw:(g + 1) * gw]
            s = _dot_nt(q_block_diag(g), kn) - _group_rows(new_cum[g * hg:(g + 1) * hg, :], tq)
            s = jnp.where(causal, s, NEG_BIG)
            vn = nv_ref[0, :, g * gw:(g + 1) * gw]
            online_update(g, [s], lambda p, vn=vn: _dot(p, vn))
            o = acc_s[g] / jnp.sum(l_s[g], axis=1, keepdims=True)
            picked = functools.reduce(
                jnp.add, [jnp.where(lane_grp == r, o[r * tq:(r + 1) * tq, :], 0.0) for r in range(hg)])
            gate = gate_ref[0, :, g * gw:(g + 1) * gw]
            out_ref[0, :, g * gw:(g + 1) * gw] = (picked * _sigmoid(gate)).astype(out_ref.dtype)


def _sample_attn(q, cache_kt, cache_vt, cache_logf_t, kb, vb, new_logf_t, gate, tk):
    bsz, tq, d = q.shape
    past = cache_kt.shape[2]
    n_groups = d // MXU_N
    one = lambda b, j: (b, 0, 0)
    blk = lambda b, j: (b, 0, j)
    return pl.pallas_call(
        functools.partial(_sample_attn_kernel, tk=tk),
        grid=(bsz, past // tk),
        in_specs=[pl.BlockSpec((1, tq, d), one), pl.BlockSpec((1, d, tk), blk), pl.BlockSpec((1, d, tk), blk),
                  pl.BlockSpec((1, B_HEADS, past), one),
                  pl.BlockSpec((1, tq, d), one), pl.BlockSpec((1, tq, d), one),
                  pl.BlockSpec((1, B_HEADS, tq), one), pl.BlockSpec((1, tq, d), one)],
        out_specs=pl.BlockSpec((1, tq, d), one),
        out_shape=jax.ShapeDtypeStruct((bsz, tq, d), BF16),
        scratch_shapes=[pltpu.VMEM((past // tk, B_HEADS, tk), F32),
                        pltpu.VMEM((n_groups, MXU_N // B_HD * tq, LANES), F32),
                        pltpu.VMEM((n_groups, MXU_N // B_HD * tq, LANES), F32),
                        pltpu.VMEM((n_groups, MXU_N // B_HD * tq, MXU_N), F32)],
        compiler_params=_params("parallel", "arbitrary"),
        name="sample_attn",
    )(q, cache_kt, cache_vt, cache_logf_t, kb, vb, new_logf_t, gate)


def _prep_weights(g_mix, g_ffn, w_in_a, b_i_a, b_f_a, g_head_a, w_out_a, g_kv, w_kvf, b_f_b, g_k,
                  w_qg_b, g_q_b, w_o_b, w_gu, w_down):
    d = g_mix.shape[1]
    main_w = 2 * A_HEADS * A_DQK + 2 * A_HEADS * A_DV
    w_in = w_in_a[0]
    gates_b = jnp.concatenate([b_i_a[0], b_f_a[0]])

    def ffn(layer):
        return w_gu[layer].astype(BF16), w_down[layer].astype(BF16)

    return dict(
        g_mix0=g_mix[0][None], g_mix1=g_mix[1][None], g_ffn0=g_ffn[0][None], g_ffn1=g_ffn[1][None],
        w_in=w_in[:, :main_w].astype(BF16), wg2=_hi_lo_cols(w_in[:, main_w:]),
        gb_col=gates_b[None, :], gb_row=gates_b[:, None], g_head=g_head_a[0][None],
        w_out=w_out_a[0].astype(BF16),
        g_kv=g_kv[None], w_kv=w_kvf[:, :2 * d].astype(BF16), wf2=_hi_lo_cols(w_kvf[:, 2 * d:]), b_f=b_f_b[None],
        gk_t=jnp.tile(g_k, B_HEADS)[None], gq_t=jnp.tile(g_q_b[0], B_HEADS)[None],
        w_qg=w_qg_b[0].astype(BF16), w_o=w_o_b[0].astype(BF16),
        ffn0=ffn(0), ffn1=ffn(1),
    )


def _state_to_pairs(c, n, m):
    bsz = c.shape[0]
    np_ = A_HEADS // 2
    ct = c.reshape(bsz, np_, 2, A_DV, A_DQK).transpose(0, 1, 2, 4, 3).reshape(bsz, np_, 2 * A_DQK, A_DV)
    n2 = n.reshape(bsz, np_, 1, 2 * A_DQK)
    m2 = jnp.broadcast_to(m[:, :, None, None], (bsz, A_HEADS, 1, LANES))
    return ct.astype(F32), n2.astype(F32), m2.astype(F32)


def _state_from_pairs(ct, n2, m2):
    bsz = ct.shape[0]
    np_ = A_HEADS // 2
    c = ct.reshape(bsz, np_, 2, A_DQK, A_DV).transpose(0, 1, 2, 4, 3).reshape(bsz, 1, A_HEADS, A_DV, A_DQK)
    n = n2.reshape(bsz, 1, A_HEADS, A_DQK)
    m = m2[:, :, 0, 0].reshape(bsz, 1, A_HEADS)
    return c, n, m


def _layer_a(x, c0, n0, m0, w, chunk, tm):
    bsz, t, d = x.shape
    x2d = x.reshape(bsz * t, d)
    q, k, v, o, gates = _in_proj(x2d, w["g_mix0"], w["w_in"], w["wg2"], tm)
    r3 = lambda z: z.reshape(bsz, t, z.shape[1])
    gates3 = r3(gates)
    ct0, n20, m20 = _state_to_pairs(c0, n0, m0)
    hn, ct, n2, m2 = _mlstm(r3(q), r3(k), r3(v), r3(o), gates3, gates3.transpose(0, 2, 1),
                            w["gb_col"], w["gb_row"], w["g_head"], ct0, n20, m20, chunk, 2 if bsz % 2 == 0 else 1)
    x2 = _mix_ffn(hn.reshape(bsz * t, -1), x2d, w["w_out"], w["g_ffn0"], *w["ffn0"], tm)
    return x2, _state_from_pairs(ct, n2, m2)


def _fox_inputs(x2, w, tm, seq_t=None):
    return _fox_proj(x2, w["g_kv"], w["g_mix1"], w["w_kv"], w["wf2"], w["w_qg"],
                     w["b_f"], w["gk_t"], w["gq_t"], tm, seq_t)


def _prompt_trunk(x, w):
    bsz, t, d = x.shape
    tm = 512
    zeros = lambda *s: jnp.zeros(s, F32)
    x2, (c, n, m) = _layer_a(x, zeros(bsz, A_HEADS, A_DV, A_DQK), zeros(bsz, A_HEADS, A_DQK),
                             zeros(bsz, A_HEADS), w, min(t, 128), tm)
    kt, vt, logf, kb, vb, q, gate = _fox_inputs(x2, w, tm, seq_t=t)
    r3 = lambda z: z.reshape(bsz, t, z.shape[1])
    logf3 = r3(logf)
    cum_aug = _cum_bias(logf3, min(t, 512))
    attn = _prompt_attn(r3(q), r3(kb), cum_aug, r3(vb), r3(gate), min(t, 512))
    y = _mix_ffn(attn.reshape(bsz * t, d), x2, w["w_o"], w["g_ffn1"], *w["ffn1"], tm)
    to_out = lambda z: z.reshape(bsz, B_HEADS, B_HD, t).transpose(0, 3, 1, 2)
    return (y.reshape(bsz, t, d), to_out(kt), to_out(vt), logf3, c, n, m)


def _sample_trunk(x, c0, n0, m0, cache_k, cache_v, cache_logf, w):
    bsz, t, d = x.shape
    past = cache_k.shape[1]
    tm = min(bsz * t, 512)
    x2, (c, n, m) = _layer_a(x, c0[:, 0], n0[:, 0], m0[:, 0], w, t, tm)
    k, v, logf, kb, vb, q, gate = _fox_inputs(x2, w, tm)
    r3 = lambda z: z.reshape(bsz, t, z.shape[1])
    logf3 = r3(logf)
    kt = cache_k.transpose(0, 2, 3, 1).reshape(bsz, d, past)
    vt = cache_v.transpose(0, 2, 3, 1).reshape(bsz, d, past)
    attn = _sample_attn(r3(q), kt, vt, cache_logf.transpose(0, 2, 1), r3(kb), r3(vb),
                        logf3.transpose(0, 2, 1), r3(gate), min(past, 2048))
    y = _mix_ffn(attn.reshape(bsz * t, d), x2, w["w_o"], w["g_ffn1"], *w["ffn1"], tm)
    return (y.reshape(bsz, t, d), k.reshape(bsz, t, B_HEADS, B_HD), v.reshape(bsz, t, B_HEADS, B_HD),
            logf3, c, n, m)


def kernel(x_prompt, x_sample, cache_k, cache_v, cache_logf, state_c, state_n, state_m, g_mix, g_ffn, w_in_a, b_i_a, b_f_a, g_head_a, w_out_a, g_kv, w_kvf, b_f_b, g_k, w_qg_b, g_q_b, w_o_b, w_gu, w_down):
    w = _prep_weights(g_mix, g_ffn, w_in_a, b_i_a, b_f_a, g_head_a, w_out_a, g_kv, w_kvf, b_f_b, g_k,
                      w_qg_b, g_q_b, w_o_b, w_gu, w_down)
    y_p, k_p, v_p, f_p, c_p, n_p, m_p = _prompt_trunk(x_prompt, w)
    y_s, k_s, v_s, f_s, c_s, n_s, m_s = _sample_trunk(x_sample, state_c, state_n, state_m,
                                                      cache_k, cache_v, cache_logf, w)
    return (y_p, y_s, k_p, v_p, f_p, c_p, n_p, m_p, k_s, v_s, f_s, c_s, n_s, m_s)
```

```python
import functools

import jax
import jax.numpy as jnp
from jax import lax
from jax.experimental import pallas as pl
from jax.experimental.pallas import tpu as pltpu

F32 = jnp.float32
BF16 = jnp.bfloat16

EPS = 1e-6
GATE_CAP = 15.0
A_HEADS = 8
A_DQK = 64
A_DV = 128
B_HEADS = 16
B_HD = 64
LANES = 128
HALF = 64
MXU_N = 256
FF_CHUNK = MXU_N
NEG_BIG = -1e30
LOG2E = 1.4426950408889634
N_BIAS_PIECES = 3
VMEM_LIMIT = 56 * 1024 * 1024

ROW_TILE = 512
MLSTM_CHUNK = 128
MLSTM_STREAMS = 2
ATTN_BLOCK = 512
CACHE_KEY_BLOCK = 2048


def _params(*sem):
    return pltpu.CompilerParams(dimension_semantics=sem, vmem_limit_bytes=VMEM_LIMIT)


def _resident(shape):
    nd = len(shape)
    return pl.BlockSpec(shape, lambda *_: (0,) * nd, pipeline_mode=pl.Buffered(1))


def _dot(a, b):
    return jnp.dot(a, b, preferred_element_type=F32)


def _dot_nt(a, b):
    return lax.dot_general(a, b, (((1,), (1,)), ((), ())), preferred_element_type=F32)


def _split2(x):
    hi = x.astype(BF16)
    lo = (x - hi.astype(F32)).astype(BF16)
    return hi, lo


def _split3(x):
    hi = x.astype(BF16)
    r = x - hi.astype(F32)
    mid = r.astype(BF16)
    lo = (r - mid.astype(F32)).astype(BF16)
    return hi, mid, lo


def _hi_lo_cols(w):
    hi, lo = _split2(w)
    both = jnp.concatenate([hi, lo], axis=1)
    return jnp.pad(both, ((0, 0), (0, LANES - both.shape[1])))


def _log_sigmoid(x):
    return jnp.minimum(x, 0.0) - jnp.log(1.0 + jnp.exp(-jnp.abs(x)))


def _sigmoid(x):
    return 1.0 / (1.0 + jnp.exp(-x))


def _soft_cap(z):
    return GATE_CAP * jnp.tanh(z / GATE_CAP)


def _rms_hat(x):
    return x * lax.rsqrt(jnp.mean(x * x, axis=-1, keepdims=True) + EPS)


def _head_norm_group(xc, gc, lo_mask):
    sq = xc * xc
    s_lo = jnp.sum(jnp.where(lo_mask, sq, 0.0), axis=-1, keepdims=True)
    s_hi = jnp.sum(jnp.where(lo_mask, 0.0, sq), axis=-1, keepdims=True)
    ms = jnp.where(lo_mask, s_lo, s_hi) * (1.0 / B_HD)
    return xc * lax.rsqrt(ms + EPS) * gc


def _in_proj_kernel(x_ref, g_ref, w_ref, wg2_ref, q_ref, k_ref, v_ref, o_ref, gates_ref):
    xh = (_rms_hat(x_ref[...]) * g_ref[...]).astype(BF16)
    qk_w = A_HEADS * A_DQK
    v_w = A_HEADS * A_DV
    q_ref[...] = _dot(xh, w_ref[:, 0:qk_w]).astype(BF16)
    k_ref[...] = (_dot(xh, w_ref[:, qk_w:2 * qk_w]) * (A_DQK ** -0.5)).astype(BF16)
    v_ref[...] = _dot(xh, w_ref[:, 2 * qk_w:2 * qk_w + v_w]).astype(BF16)
    o_ref[...] = _dot(xh, w_ref[:, 2 * qk_w + v_w:2 * qk_w + 2 * v_w])
    gz = _dot(xh, wg2_ref[...])
    gates_ref[...] = gz[:, 0:2 * A_HEADS] + gz[:, 2 * A_HEADS:4 * A_HEADS]


def _in_proj(x2d, g, w_main, wg2, tm):
    m, d = x2d.shape
    qk_w = A_HEADS * A_DQK
    v_w = A_HEADS * A_DV
    row = lambda i: (i, 0)
    return pl.pallas_call(
        _in_proj_kernel,
        grid=(m // tm,),
        in_specs=[pl.BlockSpec((tm, d), row), _resident(g.shape), _resident(w_main.shape),
                  _resident(wg2.shape)],
        out_specs=[pl.BlockSpec((tm, qk_w), row), pl.BlockSpec((tm, qk_w), row),
                   pl.BlockSpec((tm, v_w), row), pl.BlockSpec((tm, v_w), row),
                   pl.BlockSpec((tm, 2 * A_HEADS), row)],
        out_shape=[jax.ShapeDtypeStruct((m, qk_w), BF16), jax.ShapeDtypeStruct((m, qk_w), BF16),
                   jax.ShapeDtypeStruct((m, v_w), BF16), jax.ShapeDtypeStruct((m, v_w), F32),
                   jax.ShapeDtypeStruct((m, 2 * A_HEADS), F32)],
        compiler_params=_params("parallel"),
        name="in_proj",
    )(x2d, g, w_main, wg2)


def _mlstm_kernel(q_ref, k_ref, v_ref, o_ref, gc_ref, gr_ref, bc_ref, br_ref, gh_ref,
                  c0_ref, n0_ref, m0_ref,
                  h_ref, c_out, n_out, m_out, ct_s, n_s, m_s, *, chunk):
    j = pl.program_id(1)
    L = chunk
    n_streams = q_ref.shape[0]

    @pl.when(j == 0)
    def _():
        ct_s[...] = c0_ref[...]
        n_s[...] = n0_ref[...]
        m_s[...] = m0_ref[...]

    t_idx = lax.broadcasted_iota(jnp.int32, (L, L), 0)
    s_idx = lax.broadcasted_iota(jnp.int32, (L, L), 1)
    causal = s_idx <= t_idx
    lane = lax.broadcasted_iota(jnp.int32, (1, LANES), 1)
    lo_lane = lane < HALF
    lo_row = lax.broadcasted_iota(jnp.int32, (LANES, LANES), 0) < HALF
    eye = (lax.broadcasted_iota(jnp.int32, (LANES, LANES), 0)
           == lax.broadcasted_iota(jnp.int32, (LANES, LANES), 1)).astype(BF16)
    place = (lax.broadcasted_iota(jnp.int32, (2 * A_HEADS, 2 * A_HEADS * LANES), 0)
             == lax.broadcasted_iota(jnp.int32, (2 * A_HEADS, 2 * A_HEADS * LANES), 1) // LANES).astype(BF16)

    def rep(col):
        return jnp.broadcast_to(col, (col.shape[0], LANES))

    def head_stages(b, h, gates, q2, k2, kt2, ct2b, qn, done):
        cols_rep, li_r, lf_r = gates
        half = lo_lane if h % 2 == 0 else jnp.logical_not(lo_lane)
        li_col = cols_rep[:, h * LANES:(h + 1) * LANES]
        lf_col = cols_rep[:, (A_HEADS + h) * LANES:(A_HEADS + h + 1) * LANES]
        li_row = li_r[h:h + 1, :]
        lf_row = lf_r[h:h + 1, :]
        m_prev = m_s[b, h]
        v_h = v_ref[b, :, h * A_DV:(h + 1) * A_DV]
        qm = jnp.where(half, q2, jnp.zeros_like(q2))
        qk = _dot_nt(qm, k2)
        qc = _dot(qm, ct2b)
        qn_sum = rep(jnp.sum(jnp.where(half, qn, 0.0), axis=1, keepdims=True))
        b_col = rep(jnp.sum(jnp.where(causal, lf_row, 0.0), axis=1, keepdims=True))
        b_row = jnp.sum(jnp.where(t_idx <= s_idx, lf_col[:, :L], 0.0), axis=0, keepdims=True)
        b_last = rep(jnp.sum(lf_row, axis=1, keepdims=True))
        yield
        dmat = jnp.where(causal, b_col[:, :L] - b_row + li_row, NEG_BIG)
        inter = b_col + m_prev
        m_t = jnp.maximum(inter, rep(jnp.max(dmat, axis=1, keepdims=True)))
        w_col = b_last - b_col + li_col
        m_new = jnp.maximum(b_last + m_prev, jnp.max(w_col, axis=0, keepdims=True))
        yield
        g = jnp.exp(inter - m_t)
        dec = jnp.where(causal, jnp.exp(dmat - m_t[:, :L]), 0.0)
        floor = jnp.exp(-m_t)
        g_c = jnp.exp(b_last + m_prev - m_new)
        ws = jnp.exp(w_col - m_new)
        yield
        s = qk * dec
        wsv = (ws * v_h.astype(F32)).astype(BF16)
        upd = _dot(kt2, wsv)
        num = g * qc + _dot(s.astype(BF16), v_h)
        den = g * qn_sum + rep(jnp.sum(s, axis=1, keepdims=True))
        yield
        hv = num / jnp.maximum(jnp.abs(den), floor)
        ms = rep(jnp.mean(hv * hv, axis=-1, keepdims=True))
        yield
        hv = hv * lax.rsqrt(ms + EPS)
        hv = hv * gh_ref[:, h * A_DV:(h + 1) * A_DV]
        hv = hv * _sigmoid(o_ref[b, :, h * A_DV:(h + 1) * A_DV])
        h_ref[b, :, h * A_DV:(h + 1) * A_DV] = hv.astype(h_ref.dtype)
        m_s[b, h] = m_new
        done[b, h] = (upd, g_c, ws)

    pairs = []
    done = {}
    gens = []
    for b in range(n_streams):
        cap_c = _soft_cap(gc_ref[b] + bc_ref[...])
        cap_r = _soft_cap(gr_ref[b] + br_ref[...])
        li_c = cap_c[:, 0:A_HEADS]
        lf_c = _log_sigmoid(cap_c[:, A_HEADS:2 * A_HEADS])
        cols_rep = sum(_dot(piece, place) for piece in _split3(jnp.concatenate([li_c, lf_c], axis=1)))
        gates = (cols_rep, cap_r[0:A_HEADS, :], _log_sigmoid(cap_r[A_HEADS:2 * A_HEADS, :]))
        for p in range(A_HEADS // 2):
            q2 = q_ref[b, :, p * LANES:(p + 1) * LANES]
            k2 = k_ref[b, :, p * LANES:(p + 1) * LANES]
            kt2 = _dot_nt(eye, k2).astype(BF16)
            ct2 = ct_s[b, p]
            n2 = n_s[b, p]
            pairs.append((b, p, k2, ct2, n2))
            for hh in range(2):
                gens.append(head_stages(b, 2 * p + hh, gates, q2, k2, kt2, ct2.astype(BF16),
                                        q2.astype(F32) * n2, done))
    while gens:
        alive = []
        for gen in gens:
            try:
                next(gen)
                alive.append(gen)
            except StopIteration:
                pass
        gens = alive

    for b, p, k2, ct2, n2 in pairs:
        (upd0, gc0, ws0), (upd1, gc1, ws1) = done[b, 2 * p], done[b, 2 * p + 1]
        ct_s[b, p] = jnp.where(lo_row, gc0, gc1) * ct2 + jnp.where(lo_row, upd0, upd1)
        ws2 = jnp.where(lo_lane, ws0, ws1)
        n_s[b, p] = (jnp.where(lo_lane, gc0, gc1) * n2
                     + jnp.sum(ws2 * k2.astype(F32), axis=0, keepdims=True))

    @pl.when(j == pl.num_programs(1) - 1)
    def _():
        c_out[...] = ct_s[...]
        n_out[...] = n_s[...]
        m_out[...] = m_s[...]


def _mlstm(q, k, v, o, gates, gates_t, b_col, b_row, g_head, ct0, n0, m0, chunk, streams):
    bsz, t, _ = q.shape
    nc = t // chunk
    np_ = A_HEADS // 2
    sb = streams
    tok = lambda b, j: (b, j, 0)
    st4 = lambda b, j: (b, 0, 0, 0)
    return pl.pallas_call(
        functools.partial(_mlstm_kernel, chunk=chunk),
        grid=(bsz // sb, nc),
        in_specs=[pl.BlockSpec((sb, chunk, q.shape[2]), tok), pl.BlockSpec((sb, chunk, k.shape[2]), tok),
                  pl.BlockSpec((sb, chunk, v.shape[2]), tok), pl.BlockSpec((sb, chunk, o.shape[2]), tok),
                  pl.BlockSpec((sb, chunk, 2 * A_HEADS), tok),
                  pl.BlockSpec((sb, 2 * A_HEADS, chunk), lambda b, j: (b, 0, j)),
                  _resident(b_col.shape), _resident(b_row.shape), _resident(g_head.shape),
                  pl.BlockSpec((sb, np_, LANES, A_DV), st4), pl.BlockSpec((sb, np_, 1, LANES), st4),
                  pl.BlockSpec((sb, A_HEADS, 1, LANES), st4)],
        out_specs=[pl.BlockSpec((sb, chunk, v.shape[2]), tok),
                   pl.BlockSpec((sb, np_, LANES, A_DV), st4), pl.BlockSpec((sb, np_, 1, LANES), st4),
                   pl.BlockSpec((sb, A_HEADS, 1, LANES), st4)],
        out_shape=[jax.ShapeDtypeStruct((bsz, t, v.shape[2]), BF16),
                   jax.ShapeDtypeStruct((bsz, np_, LANES, A_DV), F32),
                   jax.ShapeDtypeStruct((bsz, np_, 1, LANES), F32),
                   jax.ShapeDtypeStruct((bsz, A_HEADS, 1, LANES), F32)],
        scratch_shapes=[pltpu.VMEM((sb, np_, LANES, A_DV), F32), pltpu.VMEM((sb, np_, 1, LANES), F32),
                        pltpu.VMEM((sb, A_HEADS, 1, LANES), F32)],
        compiler_params=_params("parallel", "arbitrary"),
        name="mlstm",
    )(q, k, v, o, gates, gates_t, b_col, b_row, g_head, ct0, n0, m0)


def _mix_ffn_kernel(a_ref, res_ref, wo_ref, g_ref, wgu_ref, wd_ref, out_ref, xn_s, acc_s):
    x1 = res_ref[...] + _dot(a_ref[...], wo_ref[...])
    xn_s[...] = (_rms_hat(x1) * g_ref[...]).astype(BF16)
    acc_s[...] = x1

    d_ff = wd_ref.shape[0]
    for c in range(d_ff // FF_CHUNK):
        lo, hi = c * FF_CHUNK, (c + 1) * FF_CHUNK
        xn = xn_s[...]
        gate = _dot(xn, wgu_ref[:, lo:hi])
        up = _dot(xn, wgu_ref[:, d_ff + lo:d_ff + hi])
        act = (gate * _sigmoid(gate) * up).astype(BF16)
        acc_s[...] += _dot(act, wd_ref[lo:hi, :])
    out_ref[...] = acc_s[...]


def _mix_ffn(a, res, w_o, g, wgu, wd, tm):
    m, d = res.shape
    row = lambda i: (i, 0)
    return pl.pallas_call(
        _mix_ffn_kernel,
        grid=(m // tm,),
        in_specs=[pl.BlockSpec((tm, a.shape[1]), row), pl.BlockSpec((tm, d), row),
                  _resident(w_o.shape), _resident(g.shape), _resident(wgu.shape), _resident(wd.shape)],
        out_specs=pl.BlockSpec((tm, d), row),
        out_shape=jax.ShapeDtypeStruct((m, d), F32),
        scratch_shapes=[pltpu.VMEM((tm, d), BF16), pltpu.VMEM((tm, d), F32)],
        compiler_params=_params("parallel"),
        name="mix_ffn",
    )(a, res, w_o, g, wgu, wd)


def _fox_proj_kernel(x_ref, gkv_ref, gmix_ref, wkv_ref, wf2_ref, wqg_ref, bf_ref, gk_ref, gq_ref,
                     k_ref, v_ref, logf_ref, kb_ref, vb_ref, q_ref, gate_ref):
    d = x_ref.shape[1]
    xhat = _rms_hat(x_ref[...])
    sh = (xhat * gkv_ref[...]).astype(BF16)
    hq = (xhat * gmix_ref[...]).astype(BF16)
    lo_lane = lax.broadcasted_iota(jnp.int32, (1, LANES), 1) < HALF

    fz = _dot(sh, wf2_ref[...])
    logf_ref[...] = _log_sigmoid(fz[:, 0:B_HEADS] + fz[:, B_HEADS:2 * B_HEADS] + bf_ref[...])

    for c in range(d // MXU_N):
        kraw = _dot(sh, wkv_ref[:, c * MXU_N:(c + 1) * MXU_N])
        qraw = _dot(hq, wqg_ref[:, c * MXU_N:(c + 1) * MXU_N])
        for u in range(MXU_N // LANES):
            us = slice(u * LANES, (u + 1) * LANES)
            cs = slice(c * MXU_N + u * LANES, c * MXU_N + (u + 1) * LANES)
            kn = _head_norm_group(kraw[:, us], gk_ref[:, cs], lo_lane)
            if k_ref.ndim == 3:
                k_ref[0, cs, :] = kn.T
            else:
                k_ref[:, cs] = kn
            kb_ref[:, cs] = kn.astype(BF16)
            qn = _head_norm_group(qraw[:, us], gq_ref[:, cs], lo_lane)
            q_ref[:, cs] = (qn * (LOG2E * B_HD ** -0.5)).astype(BF16)
    for c in range(d // MXU_N):
        cs = slice(c * MXU_N, (c + 1) * MXU_N)
        v = _dot(sh, wkv_ref[:, d + c * MXU_N:d + (c + 1) * MXU_N])
        if v_ref.ndim == 3:
            v_ref[0, cs, :] = v.T
        else:
            v_ref[:, cs] = v
        vb_ref[:, cs] = v.astype(BF16)
    gate_ref[...] = _dot(hq, wqg_ref[:, d:2 * d])


def _fox_proj(x2d, g_kv, g_mix, w_kv, wf2, w_qg, b_f, gk_t, gq_t, tm, seq_t=None):
    m, d = x2d.shape
    row = lambda i: (i, 0)
    big = pl.BlockSpec((tm, d), row)
    if seq_t is None:
        kv_spec, kv_shape = big, jax.ShapeDtypeStruct((m, d), F32)
    else:
        per_seq = seq_t // tm
        kv_spec = pl.BlockSpec((1, d, tm), lambda i: (i // per_seq, 0, i % per_seq))
        kv_shape = jax.ShapeDtypeStruct((m // seq_t, d, seq_t), F32)
    return pl.pallas_call(
        _fox_proj_kernel,
        grid=(m // tm,),
        in_specs=[big] + [_resident(a.shape) for a in (g_kv, g_mix, w_kv, wf2, w_qg, b_f, gk_t, gq_t)],
        out_specs=[kv_spec, kv_spec, pl.BlockSpec((tm, B_HEADS), row), big, big, big, big],
        out_shape=[kv_shape, kv_shape,
                   jax.ShapeDtypeStruct((m, B_HEADS), F32),
                   jax.ShapeDtypeStruct((m, d), BF16), jax.ShapeDtypeStruct((m, d), BF16),
                   jax.ShapeDtypeStruct((m, d), BF16), jax.ShapeDtypeStruct((m, d), F32)],
        compiler_params=_params("parallel"),
        name="fox_proj",
    )(x2d, g_kv, g_mix, w_kv, wf2, w_qg, b_f, gk_t, gq_t)


def _cum_bias_kernel(x_ref, aug_ref, carry_s):
    @pl.when(pl.program_id(1) == 0)
    def _():
        carry_s[...] = jnp.zeros(carry_s.shape, F32)

    tb = x_ref.shape[1]
    tri = (lax.broadcasted_iota(jnp.int32, (tb, tb), 1)
           <= lax.broadcasted_iota(jnp.int32, (tb, tb), 0)).astype(BF16)
    hi, mid, lo = _split3(x_ref[0])
    cum = _dot(tri, hi) + _dot(tri, mid) + _dot(tri, lo) + carry_s[...]
    carry_s[...] = cum[tb - 1:tb, :]

    h_idx = lax.broadcasted_iota(jnp.int32, (B_HEADS, aug_ref.shape[2]), 0)
    c_idx = lax.broadcasted_iota(jnp.int32, (B_HEADS, aug_ref.shape[2]), 1)
    base = N_BIAS_PIECES * h_idx
    aug = None
    for i, piece in enumerate(_split3(cum * (-LOG2E))):
        term = _dot(piece, (c_idx == base + i).astype(BF16))
        aug = term if aug is None else aug + term
    aug_ref[0] = aug.astype(BF16)


def _cum_bias(logf, tb):
    bsz, t, h = logf.shape
    width = LANES
    assert N_BIAS_PIECES * h <= width
    return pl.pallas_call(
        _cum_bias_kernel,
        grid=(bsz, t // tb),
        in_specs=[pl.BlockSpec((1, tb, h), lambda b, j: (b, j, 0))],
        out_specs=pl.BlockSpec((1, tb, width), lambda b, j: (b, j, 0)),
        out_shape=jax.ShapeDtypeStruct((bsz, t, width), BF16),
        scratch_shapes=[pltpu.VMEM((1, h), F32)],
        compiler_params=_params("parallel", "arbitrary"),
        name="cum_bias",
    )(logf)


def _attn_pair_scores(p, q2, k2, ca, mask, m_s, lo_lane):
    tq = q2.shape[0]
    lane2 = lax.broadcasted_iota(jnp.int32, (tq, LANES), 1)
    rows = []
    for hh in range(2):
        h = 2 * p + hh
        half = lo_lane if hh == 0 else jnp.logical_not(lo_lane)
        qm = jnp.where(half, q2, jnp.zeros_like(q2))
        pick = jnp.logical_and(lane2 >= N_BIAS_PIECES * h, lane2 < N_BIAS_PIECES * (h + 1))
        rows.append(jnp.concatenate([qm, pick.astype(BF16)], axis=1))
    s2 = _dot_nt(jnp.concatenate(rows, axis=0),
                 jnp.concatenate([k2, ca], axis=1))
    out = []
    for hh in range(2):
        s = s2[hh * tq:(hh + 1) * tq, :]
        cols = [s[:, c * LANES:(c + 1) * LANES] for c in range(s.shape[1] // LANES)]
        if mask is not None:
            cols = [jnp.where(mask[:, c * LANES:(c + 1) * LANES], col, NEG_BIG) for c, col in enumerate(cols)]
        top = functools.reduce(jnp.maximum, cols)
        m_old = m_s[2 * p + hh]
        m_new = jnp.maximum(m_old, jnp.max(top, axis=1, keepdims=True))
        out.append((cols, m_old, m_new))
    return out


def _attn_head_update(h, cols, m_old, m_new, v2, m_s, acc_s, lo_lane):
    half = lo_lane if h % 2 == 0 else jnp.logical_not(lo_lane)
    alpha = jnp.exp2(m_old - m_new)
    pr = jnp.concatenate([jnp.exp2(col - m_new).astype(BF16) for col in cols], axis=1)
    v_ones = jnp.where(half, v2, jnp.ones_like(v2))
    acc_s[h] = alpha * acc_s[h] + _dot(pr, v_ones)
    m_s[h] = m_new


def _attn_init(m_s, acc_s):
    m_s[...] = jnp.full(m_s.shape, NEG_BIG, F32)
    acc_s[...] = jnp.zeros(acc_s.shape, F32)


def _attn_finish(out_ref, gate_ref, acc_s, lo_lane):
    for p in range(B_HEADS // 2):
        a0, a1 = acc_s[2 * p], acc_s[2 * p + 1]
        num = jnp.where(lo_lane, a0, a1)
        den = pltpu.roll(jnp.where(lo_lane, a1, a0), HALF, axis=1)
        gate = gate_ref[0, :, p * LANES:(p + 1) * LANES]
        out_ref[0, :, p * LANES:(p + 1) * LANES] = (num / den * _sigmoid(gate)).astype(out_ref.dtype)


def _fold_block(r, c, nq):
    first = c <= r
    return jnp.where(first, r, nq - 1 - r), jnp.where(first, c, c - r - 1)


def _prompt_attn_kernel(q_ref, k_ref, ca_ref, v_ref, gate_ref, out_ref, m_s, acc_s, *, blk, nq):
    i, j = _fold_block(pl.program_id(1), pl.program_id(2), nq)
    lo_lane = lax.broadcasted_iota(jnp.int32, (1, LANES), 1) < HALF

    @pl.when(j == 0)
    def _():
        _attn_init(m_s, acc_s)

    def run(masked):
        mask = None
        if masked:
            mask = (lax.broadcasted_iota(jnp.int32, (blk, blk), 1)
                    <= lax.broadcasted_iota(jnp.int32, (blk, blk), 0))
        ca = ca_ref[0]

        def scores(p):
            cs = slice(p * LANES, (p + 1) * LANES)
            return _attn_pair_scores(p, q_ref[0, :, cs], k_ref[0, :, cs], ca, mask, m_s, lo_lane)

        pending = scores(0)
        for p in range(B_HEADS // 2):
            upcoming = scores(p + 1) if p + 1 < B_HEADS // 2 else None
            cs = slice(p * LANES, (p + 1) * LANES)
            for hh in range(2):
                _attn_head_update(2 * p + hh, *pending[hh], v_ref[0, :, cs], m_s, acc_s, lo_lane)
            pending = upcoming

    @pl.when(j < i)
    def _():
        run(False)

    @pl.when(j == i)
    def _():
        run(True)
        _attn_finish(out_ref, gate_ref, acc_s, lo_lane)


def _prompt_attn(q, kb, cum_aug, vb, gate, blk):
    bsz, t, d = q.shape
    nq = t // blk
    rows = max(nq // 2, 1)
    assert nq == 1 or nq % 2 == 0
    qmap = lambda b, r, c: (b, _fold_block(r, c, nq)[0], 0)
    kvmap = lambda b, r, c: (b, _fold_block(r, c, nq)[1], 0)
    return pl.pallas_call(
        functools.partial(_prompt_attn_kernel, blk=blk, nq=nq),
        grid=(bsz, rows, nq + 1 if nq > 1 else 1),
        in_specs=[pl.BlockSpec((1, blk, d), qmap), pl.BlockSpec((1, blk, d), kvmap),
                  pl.BlockSpec((1, blk, cum_aug.shape[2]), kvmap), pl.BlockSpec((1, blk, d), kvmap),
                  pl.BlockSpec((1, blk, d), qmap)],
        out_specs=pl.BlockSpec((1, blk, d), qmap),
        out_shape=jax.ShapeDtypeStruct((bsz, t, d), BF16),
        scratch_shapes=[pltpu.VMEM((B_HEADS, blk, LANES), F32), pltpu.VMEM((B_HEADS, blk, LANES), F32)],
        compiler_params=_params("parallel", "arbitrary", "arbitrary"),
        name="prompt_attn",
    )(q, kb, cum_aug, vb, gate)


def _group_rows(x, tq):
    return jnp.concatenate([jnp.broadcast_to(x[r:r + 1, :], (tq, x.shape[1])) for r in range(x.shape[0])], axis=0)


def _lane_cumsum(x):
    r, n = x.shape
    nb = n // MXU_N
    stacked = jnp.concatenate([x[:, c * MXU_N:(c + 1) * MXU_N] for c in range(nb)], axis=0)
    upper = (lax.broadcasted_iota(jnp.int32, (MXU_N, MXU_N), 0)
             <= lax.broadcasted_iota(jnp.int32, (MXU_N, MXU_N), 1)).astype(BF16)
    within = sum(_dot(piece, upper) for piece in _split3(stacked))
    totals = jnp.broadcast_to(within[:, MXU_N - 1:MXU_N], (nb * r, LANES))
    ri = lax.broadcasted_iota(jnp.int32, (nb * r, nb * r), 0)
    ci = lax.broadcasted_iota(jnp.int32, (nb * r, nb * r), 1)
    earlier = jnp.logical_and(ci // r < ri // r, ci % r == ri % r).astype(BF16)
    offs = sum(_dot(earlier, piece) for piece in _split3(totals))
    full = within + jnp.concatenate([offs] * (MXU_N // LANES), axis=1)
    return [full[c * r:(c + 1) * r, :] for c in range(nb)]


def _sample_attn_kernel(q_ref, kt_ref, vt_ref, lf_ref, nk_ref, nv_ref, nlf_ref, gate_ref, out_ref,
                        cum_s, m_s, l_s, acc_s, *, tk):
    j = pl.program_id(1)
    tq = q_ref.shape[1]
    n_groups = acc_s.shape[0]
    hg = B_HEADS // n_groups
    gw = hg * B_HD
    rows = hg * tq
    row_head = lax.broadcasted_iota(jnp.int32, (rows, gw), 0) // tq
    lane_head = lax.broadcasted_iota(jnp.int32, (rows, gw), 1) // B_HD
    own = row_head == lane_head

    @pl.when(j == 0)
    def _():
        m_s[...] = jnp.full(m_s.shape, NEG_BIG, F32)
        l_s[...] = jnp.zeros(l_s.shape, F32)
        acc_s[...] = jnp.zeros(acc_s.shape, F32)
        blocks = _lane_cumsum(lf_ref[0])
        per_step = tk // MXU_N
        for c, blk in enumerate(blocks):
            cum_s[c // per_step, :, (c % per_step) * MXU_N:(c % per_step + 1) * MXU_N] = blk * LOG2E

    def q_block_diag(g):
        qg = q_ref[0, :, g * gw:(g + 1) * gw]
        qrep = jnp.concatenate([qg] * hg, axis=0)
        return jnp.where(own, qrep, jnp.zeros_like(qrep))

    def online_update(g, cols, pv_fn):
        top = functools.reduce(jnp.maximum, cols)
        m_old = m_s[g]
        m_new = jnp.maximum(m_old, jnp.max(top, axis=1, keepdims=True))
        alpha = jnp.exp2(m_old - m_new)
        probs = [jnp.exp2(col - m_new[:, 0:col.shape[1]]) for col in cols]
        if cols[0].shape[1] == LANES:
            psum = functools.reduce(jnp.add, probs)
        else:
            lane0 = lax.broadcasted_iota(jnp.int32, (rows, LANES), 1) == 0
            psum = jnp.where(lane0, jnp.sum(probs[0], axis=1, keepdims=True), 0.0)
        l_s[g] = alpha * l_s[g] + psum
        p = probs[0] if len(probs) == 1 else jnp.concatenate(probs, axis=1)
        acc_s[g] = jnp.concatenate([alpha] * (gw // LANES), axis=1) * acc_s[g] + pv_fn(p.astype(BF16))
        m_s[g] = m_new

    cum_blk = cum_s[j]
    for g in range(n_groups):
        kt = kt_ref[0, g * gw:(g + 1) * gw, :].astype(BF16)
        s = _dot(q_block_diag(g), kt) - _group_rows(cum_blk[g * hg:(g + 1) * hg, :], tq)
        cols = [s[:, c * LANES:(c + 1) * LANES] for c in range(tk // LANES)]
        vt = vt_ref[0, g * gw:(g + 1) * gw, :].astype(BF16)
        online_update(g, cols, lambda p, vt=vt: _dot_nt(p, vt))

    @pl.when(j == pl.num_programs(1) - 1)
    def _():
        t_q = lax.broadcasted_iota(jnp.int32, (rows, tq), 0) % tq
        t_k = lax.broadcasted_iota(jnp.int32, (rows, tq), 1)
        causal = t_k <= t_q
        upper = (lax.broadcasted_iota(jnp.int32, (tq, tq), 0)
                 <= lax.broadcasted_iota(jnp.int32, (tq, tq), 1)).astype(BF16)
        new_cum = sum(_dot(piece, upper) for piece in _split3(nlf_ref[0]))
        last = cum_s[pl.num_programs(1) - 1][:, tk - 1:tk]
        new_cum = new_cum * LOG2E + last
        lane_grp = lax.broadcasted_iota(jnp.int32, (tq, gw), 1) // B_HD
        for g in range(n_groups):
            kn = nk_ref[0, :, g * gw:(g + 1) * gw]
            s = _dot_nt(q_block_diag(g), kn) - _group_rows(new_cum[g * hg:(g + 1) * hg, :], tq)
            s = jnp.where(causal, s, NEG_BIG)
            vn = nv_ref[0, :, g * gw:(g + 1) * gw]
            online_update(g, [s], lambda p, vn=vn: _dot(p, vn))
            o = acc_s[g] / jnp.sum(l_s[g], axis=1, keepdims=True)
            picked = functools.reduce(
                jnp.add, [jnp.where(lane_grp == r, o[r * tq:(r + 1) * tq, :], 0.0) for r in range(hg)])
            gate = gate_ref[0, :, g * gw:(g + 1) * gw]
            out_ref[0, :, g * gw:(g + 1) * gw] = (picked * _sigmoid(gate)).astype(out_ref.dtype)


def _sample_attn(q, cache_kt, cache_vt, cache_logf_t, kb, vb, new_logf_t, gate, tk):
    bsz, tq, d = q.shape
    past = cache_kt.shape[2]
    n_groups = d // MXU_N
    one = lambda b, j: (b, 0, 0)
    blk = lambda b, j: (b, 0, j)
    return pl.pallas_call(
        functools.partial(_sample_attn_kernel, tk=tk),
        grid=(bsz, past // tk),
        in_specs=[pl.BlockSpec((1, tq, d), one), pl.BlockSpec((1, d, tk), blk), pl.BlockSpec((1, d, tk), blk),
                  pl.BlockSpec((1, B_HEADS, past), one),
                  pl.BlockSpec((1, tq, d), one), pl.BlockSpec((1, tq, d), one),
                  pl.BlockSpec((1, B_HEADS, tq), one), pl.BlockSpec((1, tq, d), one)],
        out_specs=pl.BlockSpec((1, tq, d), one),
        out_shape=jax.ShapeDtypeStruct((bsz, tq, d), BF16),
        scratch_shapes=[pltpu.VMEM((past // tk, B_HEADS, tk), F32),
                        pltpu.VMEM((n_groups, MXU_N // B_HD * tq, LANES), F32),
                        pltpu.VMEM((n_groups, MXU_N // B_HD * tq, LANES), F32),
                        pltpu.VMEM((n_groups, MXU_N // B_HD * tq, MXU_N), F32)],
        compiler_params=_params("parallel", "arbitrary"),
        name="sample_attn",
    )(q, cache_kt, cache_vt, cache_logf_t, kb, vb, new_logf_t, gate)


def _prep_weights(g_mix, g_ffn, w_in_a, b_i_a, b_f_a, g_head_a, w_out_a, g_kv, w_kvf, b_f_b, g_k,
                  w_qg_b, g_q_b, w_o_b, w_gu, w_down):
    d = g_mix.shape[1]
    main_w = 2 * A_HEADS * A_DQK + 2 * A_HEADS * A_DV
    w_in = w_in_a[0]
    gates_b = jnp.concatenate([b_i_a[0], b_f_a[0]])

    def ffn(layer):
        return w_gu[layer].astype(BF16), w_down[layer].astype(BF16)

    return dict(
        g_mix0=g_mix[0][None], g_mix1=g_mix[1][None], g_ffn0=g_ffn[0][None], g_ffn1=g_ffn[1][None],
        w_in=w_in[:, :main_w].astype(BF16), wg2=_hi_lo_cols(w_in[:, main_w:]),
        gb_col=gates_b[None, :], gb_row=gates_b[:, None], g_head=g_head_a[0][None],
        w_out=w_out_a[0].astype(BF16),
        g_kv=g_kv[None], w_kv=w_kvf[:, :2 * d].astype(BF16), wf2=_hi_lo_cols(w_kvf[:, 2 * d:]), b_f=b_f_b[None],
        gk_t=jnp.tile(g_k, B_HEADS)[None], gq_t=jnp.tile(g_q_b[0], B_HEADS)[None],
        w_qg=w_qg_b[0].astype(BF16), w_o=w_o_b[0].astype(BF16),
        ffn0=ffn(0), ffn1=ffn(1),
    )


def _state_to_pairs(c, n, m):
    bsz = c.shape[0]
    np_ = A_HEADS // 2
    ct = c.reshape(bsz, np_, 2, A_DV, A_DQK).transpose(0, 1, 2, 4, 3).reshape(bsz, np_, 2 * A_DQK, A_DV)
    n2 = n.reshape(bsz, np_, 1, 2 * A_DQK)
    m2 = jnp.broadcast_to(m[:, :, None, None], (bsz, A_HEADS, 1, LANES))
    return ct.astype(F32), n2.astype(F32), m2.astype(F32)


def _state_from_pairs(ct, n2, m2):
    bsz = ct.shape[0]
    np_ = A_HEADS // 2
    c = ct.reshape(bsz, np_, 2, A_DQK, A_DV).transpose(0, 1, 2, 4, 3).reshape(bsz, 1, A_HEADS, A_DV, A_DQK)
    n = n2.reshape(bsz, 1, A_HEADS, A_DQK)
    m = m2[:, :, 0, 0].reshape(bsz, 1, A_HEADS)
    return c, n, m


def _layer_a(x, c0, n0, m0, w, chunk, tm):
    bsz, t, d = x.shape
    x2d = x.reshape(bsz * t, d)
    q, k, v, o, gates = _in_proj(x2d, w["g_mix0"], w["w_in"], w["wg2"], tm)
    r3 = lambda z: z.reshape(bsz, t, z.shape[1])
    gates3 = r3(gates)
    ct0, n20, m20 = _state_to_pairs(c0, n0, m0)
    hn, ct, n2, m2 = _mlstm(r3(q), r3(k), r3(v), r3(o), gates3, gates3.transpose(0, 2, 1),
                            w["gb_col"], w["gb_row"], w["g_head"], ct0, n20, m20, chunk,
                            MLSTM_STREAMS if bsz % MLSTM_STREAMS == 0 else 1)
    x2 = _mix_ffn(hn.reshape(bsz * t, -1), x2d, w["w_out"], w["g_ffn0"], *w["ffn0"], tm)
    return x2, _state_from_pairs(ct, n2, m2)


def _fox_inputs(x2, w, tm, seq_t=None):
    return _fox_proj(x2, w["g_kv"], w["g_mix1"], w["w_kv"], w["wf2"], w["w_qg"],
                     w["b_f"], w["gk_t"], w["gq_t"], tm, seq_t)


def _prompt_trunk(x, w):
    bsz, t, d = x.shape
    tm = min(bsz * t, ROW_TILE)
    zeros = lambda *s: jnp.zeros(s, F32)
    x2, (c, n, m) = _layer_a(x, zeros(bsz, A_HEADS, A_DV, A_DQK), zeros(bsz, A_HEADS, A_DQK),
                             zeros(bsz, A_HEADS), w, min(t, MLSTM_CHUNK), tm)
    kt, vt, logf, kb, vb, q, gate = _fox_inputs(x2, w, tm, seq_t=t)
    r3 = lambda z: z.reshape(bsz, t, z.shape[1])
    logf3 = r3(logf)
    cum_aug = _cum_bias(logf3, min(t, ATTN_BLOCK))
    attn = _prompt_attn(r3(q), r3(kb), cum_aug, r3(vb), r3(gate), min(t, ATTN_BLOCK))
    y = _mix_ffn(attn.reshape(bsz * t, d), x2, w["w_o"], w["g_ffn1"], *w["ffn1"], tm)
    to_out = lambda z: z.reshape(bsz, B_HEADS, B_HD, t).transpose(0, 3, 1, 2)
    return (y.reshape(bsz, t, d), to_out(kt), to_out(vt), logf3, c, n, m)


def _sample_trunk(x, c0, n0, m0, cache_k, cache_v, cache_logf, w):
    bsz, t, d = x.shape
    past = cache_k.shape[1]
    tm = min(bsz * t, ROW_TILE)
    x2, (c, n, m) = _layer_a(x, c0[:, 0], n0[:, 0], m0[:, 0], w, min(t, MLSTM_CHUNK), tm)
    k, v, logf, kb, vb, q, gate = _fox_inputs(x2, w, tm)
    r3 = lambda z: z.reshape(bsz, t, z.shape[1])
    logf3 = r3(logf)
    kt = cache_k.transpose(0, 2, 3, 1).reshape(bsz, d, past)
    vt = cache_v.transpose(0, 2, 3, 1).reshape(bsz, d, past)
    attn = _sample_attn(r3(q), kt, vt, cache_logf.transpose(0, 2, 1), r3(kb), r3(vb),
                        logf3.transpose(0, 2, 1), r3(gate), min(past, CACHE_KEY_BLOCK))
    y = _mix_ffn(attn.reshape(bsz * t, d), x2, w["w_o"], w["g_ffn1"], *w["ffn1"], tm)
    return (y.reshape(bsz, t, d), k.reshape(bsz, t, B_HEADS, B_HD), v.reshape(bsz, t, B_HEADS, B_HD),
            logf3, c, n, m)


def kernel(x_prompt, x_sample, cache_k, cache_v, cache_logf, state_c, state_n, state_m, g_mix, g_ffn, w_in_a, b_i_a, b_f_a, g_head_a, w_out_a, g_kv, w_kvf, b_f_b, g_k, w_qg_b, g_q_b, w_o_b, w_gu, w_down):
    w = _prep_weights(g_mix, g_ffn, w_in_a, b_i_a, b_f_a, g_head_a, w_out_a, g_kv, w_kvf, b_f_b, g_k,
                      w_qg_b, g_q_b, w_o_b, w_gu, w_down)
    y_p, k_p, v_p, f_p, c_p, n_p, m_p = _prompt_trunk(x_prompt, w)
    y_s, k_s, v_s, f_s, c_s, n_s, m_s = _sample_trunk(x_sample, state_c, state_n, state_m,
                                                      cache_k, cache_v, cache_logf, w)
    return (y_p, y_s, k_p, v_p, f_p, c_p, n_p, m_p, k_s, v_s, f_s, c_s, n_s, m_s)
```

```python
import functools

import jax
import jax.numpy as jnp
from jax import lax
from jax.experimental import pallas as pl
from jax.experimental.pallas import tpu as pltpu

F32 = jnp.float32
BF16 = jnp.bfloat16

EPS = 1e-6
GATE_CAP = 15.0
A_HEADS = 8
A_DQK = 64
A_DV = 128
B_HEADS = 16
B_HD = 64
LANES = 128
HALF = 64
MXU_N = 256
FF_CHUNK = MXU_N
NEG_BIG = -1e30
LOG2E = 1.4426950408889634
N_BIAS_PIECES = 3
VMEM_LIMIT = 56 * 1024 * 1024

ROW_TILE = 512
MLSTM_CHUNK = 128
MLSTM_STREAMS = 2
ATTN_BLOCK = 512
CACHE_KEY_BLOCK = 2048


def _params(*sem):
    return pltpu.CompilerParams(dimension_semantics=sem, vmem_limit_bytes=VMEM_LIMIT)


def _resident(shape):
    nd = len(shape)
    return pl.BlockSpec(shape, lambda *_: (0,) * nd, pipeline_mode=pl.Buffered(1))


def _dot(a, b):
    return jnp.dot(a, b, preferred_element_type=F32)


def _dot_nt(a, b):
    return lax.dot_general(a, b, (((1,), (1,)), ((), ())), preferred_element_type=F32)


def _split2(x):
    hi = x.astype(BF16)
    lo = (x - hi.astype(F32)).astype(BF16)
    return hi, lo


def _split3(x):
    hi = x.astype(BF16)
    r = x - hi.astype(F32)
    mid = r.astype(BF16)
    lo = (r - mid.astype(F32)).astype(BF16)
    return hi, mid, lo


def _hi_lo_cols(w):
    hi, lo = _split2(w)
    both = jnp.concatenate([hi, lo], axis=1)
    return jnp.pad(both, ((0, 0), (0, LANES - both.shape[1])))


def _log_sigmoid(x):
    return jnp.minimum(x, 0.0) - jnp.log(1.0 + jnp.exp(-jnp.abs(x)))


def _sigmoid(x):
    return 1.0 / (1.0 + jnp.exp(-x))


def _soft_cap(z):
    return GATE_CAP * jnp.tanh(z / GATE_CAP)


def _rms_hat(x):
    return x * lax.rsqrt(jnp.mean(x * x, axis=-1, keepdims=True) + EPS)


def _head_norm_group(xc, gc, lo_mask):
    sq = xc * xc
    s_lo = jnp.sum(jnp.where(lo_mask, sq, 0.0), axis=-1, keepdims=True)
    s_hi = jnp.sum(jnp.where(lo_mask, 0.0, sq), axis=-1, keepdims=True)
    ms = jnp.where(lo_mask, s_lo, s_hi) * (1.0 / B_HD)
    return xc * lax.rsqrt(ms + EPS) * gc


def _in_proj_kernel(x_ref, g_ref, w_ref, wg2_ref, q_ref, k_ref, v_ref, o_ref, gates_ref):
    xh = (_rms_hat(x_ref[...]) * g_ref[...]).astype(BF16)
    qk_w = A_HEADS * A_DQK
    v_w = A_HEADS * A_DV
    q_ref[...] = _dot(xh, w_ref[:, 0:qk_w]).astype(BF16)
    k_ref[...] = (_dot(xh, w_ref[:, qk_w:2 * qk_w]) * (A_DQK ** -0.5)).astype(BF16)
    v_ref[...] = _dot(xh, w_ref[:, 2 * qk_w:2 * qk_w + v_w]).astype(BF16)
    o_ref[...] = _dot(xh, w_ref[:, 2 * qk_w + v_w:2 * qk_w + 2 * v_w])
    gz = _dot(xh, wg2_ref[...])
    gates_ref[...] = gz[:, 0:2 * A_HEADS] + gz[:, 2 * A_HEADS:4 * A_HEADS]


def _in_proj(x2d, g, w_main, wg2, tm):
    m, d = x2d.shape
    qk_w = A_HEADS * A_DQK
    v_w = A_HEADS * A_DV
    row = lambda i: (i, 0)
    return pl.pallas_call(
        _in_proj_kernel,
        grid=(m // tm,),
        in_specs=[pl.BlockSpec((tm, d), row), _resident(g.shape), _resident(w_main.shape),
                  _resident(wg2.shape)],
        out_specs=[pl.BlockSpec((tm, qk_w), row), pl.BlockSpec((tm, qk_w), row),
                   pl.BlockSpec((tm, v_w), row), pl.BlockSpec((tm, v_w), row),
                   pl.BlockSpec((tm, 2 * A_HEADS), row)],
        out_shape=[jax.ShapeDtypeStruct((m, qk_w), BF16), jax.ShapeDtypeStruct((m, qk_w), BF16),
                   jax.ShapeDtypeStruct((m, v_w), BF16), jax.ShapeDtypeStruct((m, v_w), F32),
                   jax.ShapeDtypeStruct((m, 2 * A_HEADS), F32)],
        compiler_params=_params("parallel"),
        name="in_proj",
    )(x2d, g, w_main, wg2)


def _mlstm_kernel(q_ref, k_ref, v_ref, o_ref, gc_ref, gr_ref, bc_ref, br_ref, gh_ref,
                  c0_ref, n0_ref, m0_ref,
                  h_ref, c_out, n_out, m_out, ct_s, n_s, m_s, *, chunk):
    j = pl.program_id(1)
    L = chunk
    n_streams = q_ref.shape[0]

    @pl.when(j == 0)
    def _():
        ct_s[...] = c0_ref[...]
        n_s[...] = n0_ref[...]
        m_s[...] = m0_ref[...]

    t_idx = lax.broadcasted_iota(jnp.int32, (L, L), 0)
    s_idx = lax.broadcasted_iota(jnp.int32, (L, L), 1)
    causal = s_idx <= t_idx
    lane = lax.broadcasted_iota(jnp.int32, (1, LANES), 1)
    lo_lane = lane < HALF
    lo_row = lax.broadcasted_iota(jnp.int32, (LANES, LANES), 0) < HALF
    eye = (lax.broadcasted_iota(jnp.int32, (LANES, LANES), 0)
           == lax.broadcasted_iota(jnp.int32, (LANES, LANES), 1)).astype(BF16)
    place = (lax.broadcasted_iota(jnp.int32, (2 * A_HEADS, 2 * A_HEADS * LANES), 0)
             == lax.broadcasted_iota(jnp.int32, (2 * A_HEADS, 2 * A_HEADS * LANES), 1) // LANES).astype(BF16)

    def rep(col):
        return jnp.broadcast_to(col, (col.shape[0], LANES))

    def head_stages(b, h, gates, q2, k2, kt2, ct2b, qn, done):
        cols_rep, li_r, lf_r = gates
        half = lo_lane if h % 2 == 0 else jnp.logical_not(lo_lane)
        li_col = cols_rep[:, h * LANES:(h + 1) * LANES]
        lf_col = cols_rep[:, (A_HEADS + h) * LANES:(A_HEADS + h + 1) * LANES]
        li_row = li_r[h:h + 1, :]
        lf_row = lf_r[h:h + 1, :]
        m_prev = m_s[b, h]
        v_h = v_ref[b, :, h * A_DV:(h + 1) * A_DV]
        qm = jnp.where(half, q2, jnp.zeros_like(q2))
        qk = _dot_nt(qm, k2)
        qc = _dot(qm, ct2b)
        qn_sum = rep(jnp.sum(jnp.where(half, qn, 0.0), axis=1, keepdims=True))
        b_col = rep(jnp.sum(jnp.where(causal, lf_row, 0.0), axis=1, keepdims=True))
        b_row = jnp.sum(jnp.where(t_idx <= s_idx, lf_col[:, :L], 0.0), axis=0, keepdims=True)
        b_last = rep(jnp.sum(lf_row, axis=1, keepdims=True))
        yield
        dmat = jnp.where(causal, b_col[:, :L] - b_row + li_row, NEG_BIG)
        inter = b_col + m_prev
        m_t = jnp.maximum(inter, rep(jnp.max(dmat, axis=1, keepdims=True)))
        w_col = b_last - b_col + li_col
        m_new = jnp.maximum(b_last + m_prev, jnp.max(w_col, axis=0, keepdims=True))
        yield
        g = jnp.exp(inter - m_t)
        dec = jnp.where(causal, jnp.exp(dmat - m_t[:, :L]), 0.0)
        floor = jnp.exp(-m_t)
        g_c = jnp.exp(b_last + m_prev - m_new)
        ws = jnp.exp(w_col - m_new)
        yield
        s = qk * dec
        wsv = (ws * v_h.astype(F32)).astype(BF16)
        upd = _dot(kt2, wsv)
        num = g * qc + _dot(s.astype(BF16), v_h)
        den = g * qn_sum + rep(jnp.sum(s, axis=1, keepdims=True))
        yield
        hv = num / jnp.maximum(jnp.abs(den), floor)
        ms = rep(jnp.mean(hv * hv, axis=-1, keepdims=True))
        yield
        hv = hv * lax.rsqrt(ms + EPS)
        hv = hv * gh_ref[:, h * A_DV:(h + 1) * A_DV]
        hv = hv * _sigmoid(o_ref[b, :, h * A_DV:(h + 1) * A_DV])
        h_ref[b, :, h * A_DV:(h + 1) * A_DV] = hv.astype(h_ref.dtype)
        m_s[b, h] = m_new
        done[b, h] = (upd, g_c, ws)

    pairs = []
    done = {}
    gens = []
    for b in range(n_streams):
        cap_c = _soft_cap(gc_ref[b] + bc_ref[...])
        cap_r = _soft_cap(gr_ref[b] + br_ref[...])
        li_c = cap_c[:, 0:A_HEADS]
        lf_c = _log_sigmoid(cap_c[:, A_HEADS:2 * A_HEADS])
        cols_rep = sum(_dot(piece, place) for piece in _split3(jnp.concatenate([li_c, lf_c], axis=1)))
        gates = (cols_rep, cap_r[0:A_HEADS, :], _log_sigmoid(cap_r[A_HEADS:2 * A_HEADS, :]))
        for p in range(A_HEADS // 2):
            q2 = q_ref[b, :, p * LANES:(p + 1) * LANES]
            k2 = k_ref[b, :, p * LANES:(p + 1) * LANES]
            kt2 = _dot_nt(eye, k2).astype(BF16)
            ct2 = ct_s[b, p]
            n2 = n_s[b, p]
            pairs.append((b, p, k2, ct2, n2))
            for hh in range(2):
                gens.append(head_stages(b, 2 * p + hh, gates, q2, k2, kt2, ct2.astype(BF16),
                                        q2.astype(F32) * n2, done))
    while gens:
        alive = []
        for gen in gens:
            try:
                next(gen)
                alive.append(gen)
            except StopIteration:
                pass
        gens = alive

    for b, p, k2, ct2, n2 in pairs:
        (upd0, gc0, ws0), (upd1, gc1, ws1) = done[b, 2 * p], done[b, 2 * p + 1]
        ct_s[b, p] = jnp.where(lo_row, gc0, gc1) * ct2 + jnp.where(lo_row, upd0, upd1)
        ws2 = jnp.where(lo_lane, ws0, ws1)
        n_s[b, p] = (jnp.where(lo_lane, gc0, gc1) * n2
                     + jnp.sum(ws2 * k2.astype(F32), axis=0, keepdims=True))

    @pl.when(j == pl.num_programs(1) - 1)
    def _():
        c_out[...] = ct_s[...]
        n_out[...] = n_s[...]
        m_out[...] = m_s[...]


def _mlstm(q, k, v, o, gates, gates_t, b_col, b_row, g_head, ct0, n0, m0, chunk, streams):
    bsz, t, _ = q.shape
    nc = t // chunk
    np_ = A_HEADS // 2
    sb = streams
    tok = lambda b, j: (b, j, 0)
    st4 = lambda b, j: (b, 0, 0, 0)
    return pl.pallas_call(
        functools.partial(_mlstm_kernel, chunk=chunk),
        grid=(bsz // sb, nc),
        in_specs=[pl.BlockSpec((sb, chunk, q.shape[2]), tok), pl.BlockSpec((sb, chunk, k.shape[2]), tok),
                  pl.BlockSpec((sb, chunk, v.shape[2]), tok), pl.BlockSpec((sb, chunk, o.shape[2]), tok),
                  pl.BlockSpec((sb, chunk, 2 * A_HEADS), tok),
                  pl.BlockSpec((sb, 2 * A_HEADS, chunk), lambda b, j: (b, 0, j)),
                  _resident(b_col.shape), _resident(b_row.shape), _resident(g_head.shape),
                  pl.BlockSpec((sb, np_, LANES, A_DV), st4), pl.BlockSpec((sb, np_, 1, LANES), st4),
                  pl.BlockSpec((sb, A_HEADS, 1, LANES), st4)],
        out_specs=[pl.BlockSpec((sb, chunk, v.shape[2]), tok),
                   pl.BlockSpec((sb, np_, LANES, A_DV), st4), pl.BlockSpec((sb, np_, 1, LANES), st4),
                   pl.BlockSpec((sb, A_HEADS, 1, LANES), st4)],
        out_shape=[jax.ShapeDtypeStruct((bsz, t, v.shape[2]), BF16),
                   jax.ShapeDtypeStruct((bsz, np_, LANES, A_DV), F32),
                   jax.ShapeDtypeStruct((bsz, np_, 1, LANES), F32),
                   jax.ShapeDtypeStruct((bsz, A_HEADS, 1, LANES), F32)],
        scratch_shapes=[pltpu.VMEM((sb, np_, LANES, A_DV), F32), pltpu.VMEM((sb, np_, 1, LANES), F32),
                        pltpu.VMEM((sb, A_HEADS, 1, LANES), F32)],
        compiler_params=_params("parallel", "arbitrary"),
        name="mlstm",
    )(q, k, v, o, gates, gates_t, b_col, b_row, g_head, ct0, n0, m0)


def _mix_ffn_kernel(a_ref, res_ref, wo_ref, g_ref, wgu_ref, wd_ref, out_ref, xn_s, acc_s):
    x1 = res_ref[...] + _dot(a_ref[...], wo_ref[...])
    xn_s[...] = (_rms_hat(x1) * g_ref[...]).astype(BF16)
    acc_s[...] = x1

    d_ff = wd_ref.shape[0]
    for c in range(d_ff // FF_CHUNK):
        lo, hi = c * FF_CHUNK, (c + 1) * FF_CHUNK
        xn = xn_s[...]
        gate = _dot(xn, wgu_ref[:, lo:hi])
        up = _dot(xn, wgu_ref[:, d_ff + lo:d_ff + hi])
        act = (gate * _sigmoid(gate) * up).astype(BF16)
        acc_s[...] += _dot(act, wd_ref[lo:hi, :])
    out_ref[...] = acc_s[...]


def _mix_ffn(a, res, w_o, g, wgu, wd, tm):
    m, d = res.shape
    row = lambda i: (i, 0)
    return pl.pallas_call(
        _mix_ffn_kernel,
        grid=(m // tm,),
        in_specs=[pl.BlockSpec((tm, a.shape[1]), row), pl.BlockSpec((tm, d), row),
                  _resident(w_o.shape), _resident(g.shape), _resident(wgu.shape), _resident(wd.shape)],
        out_specs=pl.BlockSpec((tm, d), row),
        out_shape=jax.ShapeDtypeStruct((m, d), F32),
        scratch_shapes=[pltpu.VMEM((tm, d), BF16), pltpu.VMEM((tm, d), F32)],
        compiler_params=_params("parallel"),
        name="mix_ffn",
    )(a, res, w_o, g, wgu, wd)


def _fox_proj_kernel(x_ref, gkv_ref, gmix_ref, wkv_ref, wf2_ref, wqg_ref, bf_ref, gk_ref, gq_ref,
                     k_ref, v_ref, logf_ref, kb_ref, vb_ref, q_ref, gate_ref):
    d = x_ref.shape[1]
    xhat = _rms_hat(x_ref[...])
    sh = (xhat * gkv_ref[...]).astype(BF16)
    hq = (xhat * gmix_ref[...]).astype(BF16)
    lo_lane = lax.broadcasted_iota(jnp.int32, (1, LANES), 1) < HALF

    fz = _dot(sh, wf2_ref[...])
    logf_ref[...] = _log_sigmoid(fz[:, 0:B_HEADS] + fz[:, B_HEADS:2 * B_HEADS] + bf_ref[...])

    for c in range(d // MXU_N):
        kraw = _dot(sh, wkv_ref[:, c * MXU_N:(c + 1) * MXU_N])
        qraw = _dot(hq, wqg_ref[:, c * MXU_N:(c + 1) * MXU_N])
        for u in range(MXU_N // LANES):
            us = slice(u * LANES, (u + 1) * LANES)
            cs = slice(c * MXU_N + u * LANES, c * MXU_N + (u + 1) * LANES)
            kn = _head_norm_group(kraw[:, us], gk_ref[:, cs], lo_lane)
            if k_ref.ndim == 3:
                kn_t = kn.T
                k_ref[0, cs, :] = kn_t
                kb_ref[0, cs, :] = kn_t.astype(BF16)
            else:
                k_ref[:, cs] = kn
                kb_ref[:, cs] = kn.astype(BF16)
            qn = _head_norm_group(qraw[:, us], gq_ref[:, cs], lo_lane)
            q_ref[:, cs] = (qn * (LOG2E * B_HD ** -0.5)).astype(BF16)
    for c in range(d // MXU_N):
        cs = slice(c * MXU_N, (c + 1) * MXU_N)
        v = _dot(sh, wkv_ref[:, d + c * MXU_N:d + (c + 1) * MXU_N])
        if v_ref.ndim == 3:
            v_ref[0, cs, :] = v.T
        else:
            v_ref[:, cs] = v
        vb_ref[:, cs] = v.astype(BF16)
    gate_ref[...] = _dot(hq, wqg_ref[:, d:2 * d])


def _fox_proj(x2d, g_kv, g_mix, w_kv, wf2, w_qg, b_f, gk_t, gq_t, tm, seq_t=None):
    m, d = x2d.shape
    row = lambda i: (i, 0)
    big = pl.BlockSpec((tm, d), row)
    if seq_t is None:
        kv_spec, kv_shape, kb_spec, kb_shape = big, (m, d), big, (m, d)
    else:
        per_seq = seq_t // tm
        kv_spec = kb_spec = pl.BlockSpec((1, d, tm), lambda i: (i // per_seq, 0, i % per_seq))
        kv_shape = kb_shape = (m // seq_t, d, seq_t)
    return pl.pallas_call(
        _fox_proj_kernel,
        grid=(m // tm,),
        in_specs=[big] + [_resident(a.shape) for a in (g_kv, g_mix, w_kv, wf2, w_qg, b_f, gk_t, gq_t)],
        out_specs=[kv_spec, kv_spec, pl.BlockSpec((tm, B_HEADS), row), kb_spec, big, big, big],
        out_shape=[jax.ShapeDtypeStruct(kv_shape, F32), jax.ShapeDtypeStruct(kv_shape, F32),
                   jax.ShapeDtypeStruct((m, B_HEADS), F32),
                   jax.ShapeDtypeStruct(kb_shape, BF16), jax.ShapeDtypeStruct((m, d), BF16),
                   jax.ShapeDtypeStruct((m, d), BF16), jax.ShapeDtypeStruct((m, d), F32)],
        compiler_params=_params("parallel"),
        name="fox_proj",
    )(x2d, g_kv, g_mix, w_kv, wf2, w_qg, b_f, gk_t, gq_t)


def _cum_bias_kernel(x_ref, aug_ref, carry_s):
    @pl.when(pl.program_id(1) == 0)
    def _():
        carry_s[...] = jnp.zeros(carry_s.shape, F32)

    tb = x_ref.shape[1]
    tri = (lax.broadcasted_iota(jnp.int32, (tb, tb), 1)
           <= lax.broadcasted_iota(jnp.int32, (tb, tb), 0)).astype(BF16)
    hi, mid, lo = _split3(x_ref[0])
    cum = _dot(tri, hi) + _dot(tri, mid) + _dot(tri, lo) + carry_s[...]
    carry_s[...] = cum[tb - 1:tb, :]

    h_idx = lax.broadcasted_iota(jnp.int32, (B_HEADS, aug_ref.shape[1]), 0)
    c_idx = lax.broadcasted_iota(jnp.int32, (B_HEADS, aug_ref.shape[1]), 1)
    base = N_BIAS_PIECES * h_idx
    aug = None
    for i, piece in enumerate(_split3(cum * (-LOG2E))):
        term = _dot(piece, (c_idx == base + i).astype(BF16))
        aug = term if aug is None else aug + term
    aug_ref[0] = aug.T.astype(BF16)


def _cum_bias(logf, tb):
    bsz, t, h = logf.shape
    width = LANES
    assert N_BIAS_PIECES * h <= width
    return pl.pallas_call(
        _cum_bias_kernel,
        grid=(bsz, t // tb),
        in_specs=[pl.BlockSpec((1, tb, h), lambda b, j: (b, j, 0))],
        out_specs=pl.BlockSpec((1, width, tb), lambda b, j: (b, 0, j)),
        out_shape=jax.ShapeDtypeStruct((bsz, width, t), BF16),
        scratch_shapes=[pltpu.VMEM((1, h), F32)],
        compiler_params=_params("parallel", "arbitrary"),
        name="cum_bias",
    )(logf)


def _attn_pair_scores(p, q2, kt2, ca_t, mask, m_s, lo_lane):
    tq = q2.shape[0]
    lane2 = lax.broadcasted_iota(jnp.int32, (tq, LANES), 1)
    rows = []
    for hh in range(2):
        h = 2 * p + hh
        half = lo_lane if hh == 0 else jnp.logical_not(lo_lane)
        qm = jnp.where(half, q2, jnp.zeros_like(q2))
        pick = jnp.logical_and(lane2 >= N_BIAS_PIECES * h, lane2 < N_BIAS_PIECES * (h + 1))
        rows.append(jnp.concatenate([qm, pick.astype(BF16)], axis=1))
    s2 = _dot(jnp.concatenate(rows, axis=0),
              jnp.concatenate([kt2, ca_t], axis=0))
    out = []
    for hh in range(2):
        s = s2[hh * tq:(hh + 1) * tq, :]
        cols = [s[:, c * LANES:(c + 1) * LANES] for c in range(s.shape[1] // LANES)]
        if mask is not None:
            cols = [jnp.where(mask[:, c * LANES:(c + 1) * LANES], col, NEG_BIG) for c, col in enumerate(cols)]
        top = functools.reduce(jnp.maximum, cols)
        m_old = m_s[2 * p + hh]
        m_new = jnp.maximum(m_old, jnp.max(top, axis=1, keepdims=True))
        out.append((cols, m_old, m_new))
    return out


def _attn_head_update(h, cols, m_old, m_new, v2, m_s, acc_s, lo_lane):
    half = lo_lane if h % 2 == 0 else jnp.logical_not(lo_lane)
    alpha = jnp.exp2(m_old - m_new)
    pr = jnp.concatenate([jnp.exp2(col - m_new).astype(BF16) for col in cols], axis=1)
    v_ones = jnp.where(half, v2, jnp.ones_like(v2))
    acc_s[h] = alpha * acc_s[h] + _dot(pr, v_ones)
    m_s[h] = m_new


def _attn_init(m_s, acc_s):
    m_s[...] = jnp.full(m_s.shape, NEG_BIG, F32)
    acc_s[...] = jnp.zeros(acc_s.shape, F32)


def _attn_finish(out_ref, gate_ref, acc_s, lo_lane):
    for p in range(B_HEADS // 2):
        a0, a1 = acc_s[2 * p], acc_s[2 * p + 1]
        num = jnp.where(lo_lane, a0, a1)
        den = pltpu.roll(jnp.where(lo_lane, a1, a0), HALF, axis=1)
        gate = gate_ref[0, :, p * LANES:(p + 1) * LANES]
        out_ref[0, :, p * LANES:(p + 1) * LANES] = (num / den * _sigmoid(gate)).astype(out_ref.dtype)


def _fold_block(r, c, nq):
    first = c <= r
    return jnp.where(first, r, nq - 1 - r), jnp.where(first, c, c - r - 1)


def _prompt_attn_kernel(q_ref, kt_ref, ca_ref, v_ref, gate_ref, out_ref, m_s, acc_s, *, blk, nq):
    i, j = _fold_block(pl.program_id(1), pl.program_id(2), nq)
    lo_lane = lax.broadcasted_iota(jnp.int32, (1, LANES), 1) < HALF

    @pl.when(j == 0)
    def _():
        _attn_init(m_s, acc_s)

    def run(masked):
        mask = None
        if masked:
            mask = (lax.broadcasted_iota(jnp.int32, (blk, blk), 1)
                    <= lax.broadcasted_iota(jnp.int32, (blk, blk), 0))
        ca = ca_ref[0]

        def scores(p):
            cs = slice(p * LANES, (p + 1) * LANES)
            return _attn_pair_scores(p, q_ref[0, :, cs], kt_ref[0, cs, :], ca, mask, m_s, lo_lane)

        pending = scores(0)
        for p in range(B_HEADS // 2):
            upcoming = scores(p + 1) if p + 1 < B_HEADS // 2 else None
            cs = slice(p * LANES, (p + 1) * LANES)
            for hh in range(2):
                _attn_head_update(2 * p + hh, *pending[hh], v_ref[0, :, cs], m_s, acc_s, lo_lane)
            pending = upcoming

    @pl.when(j < i)
    def _():
        run(False)

    @pl.when(j == i)
    def _():
        run(True)
        _attn_finish(out_ref, gate_ref, acc_s, lo_lane)


def _prompt_attn(q, kb_t, cum_aug, vb, gate, blk):
    bsz, t, d = q.shape
    nq = t // blk
    rows = max(nq // 2, 1)
    assert nq == 1 or nq % 2 == 0
    qmap = lambda b, r, c: (b, _fold_block(r, c, nq)[0], 0)
    kvmap = lambda b, r, c: (b, _fold_block(r, c, nq)[1], 0)
    ktmap = lambda b, r, c: (b, 0, _fold_block(r, c, nq)[1])
    return pl.pallas_call(
        functools.partial(_prompt_attn_kernel, blk=blk, nq=nq),
        grid=(bsz, rows, nq + 1 if nq > 1 else 1),
        in_specs=[pl.BlockSpec((1, blk, d), qmap), pl.BlockSpec((1, d, blk), ktmap),
                  pl.BlockSpec((1, cum_aug.shape[1], blk), ktmap), pl.BlockSpec((1, blk, d), kvmap),
                  pl.BlockSpec((1, blk, d), qmap)],
        out_specs=pl.BlockSpec((1, blk, d), qmap),
        out_shape=jax.ShapeDtypeStruct((bsz, t, d), BF16),
        scratch_shapes=[pltpu.VMEM((B_HEADS, blk, LANES), F32), pltpu.VMEM((B_HEADS, blk, LANES), F32)],
        compiler_params=_params("parallel", "arbitrary", "arbitrary"),
        name="prompt_attn",
    )(q, kb_t, cum_aug, vb, gate)


def _group_rows(x, tq):
    return jnp.concatenate([jnp.broadcast_to(x[r:r + 1, :], (tq, x.shape[1])) for r in range(x.shape[0])], axis=0)


def _lane_cumsum(x):
    r, n = x.shape
    nb = n // MXU_N
    stacked = jnp.concatenate([x[:, c * MXU_N:(c + 1) * MXU_N] for c in range(nb)], axis=0)
    upper = (lax.broadcasted_iota(jnp.int32, (MXU_N, MXU_N), 0)
             <= lax.broadcasted_iota(jnp.int32, (MXU_N, MXU_N), 1)).astype(BF16)
    within = sum(_dot(piece, upper) for piece in _split3(stacked))
    totals = jnp.broadcast_to(within[:, MXU_N - 1:MXU_N], (nb * r, LANES))
    ri = lax.broadcasted_iota(jnp.int32, (nb * r, nb * r), 0)
    ci = lax.broadcasted_iota(jnp.int32, (nb * r, nb * r), 1)
    earlier = jnp.logical_and(ci // r < ri // r, ci % r == ri % r).astype(BF16)
    offs = sum(_dot(earlier, piece) for piece in _split3(totals))
    full = within + jnp.concatenate([offs] * (MXU_N // LANES), axis=1)
    return [full[c * r:(c + 1) * r, :] for c in range(nb)]


def _sample_attn_kernel(q_ref, kt_ref, vt_ref, lf_ref, nk_ref, nv_ref, nlf_ref, gate_ref, out_ref,
                        cum_s, m_s, l_s, acc_s, *, tk):
    j = pl.program_id(1)
    tq = q_ref.shape[1]
    n_groups = acc_s.shape[0]
    hg = B_HEADS // n_groups
    gw = hg * B_HD
    rows = hg * tq
    row_head = lax.broadcasted_iota(jnp.int32, (rows, gw), 0) // tq
    lane_head = lax.broadcasted_iota(jnp.int32, (rows, gw), 1) // B_HD
    own = row_head == lane_head

    @pl.when(j == 0)
    def _():
        m_s[...] = jnp.full(m_s.shape, NEG_BIG, F32)
        l_s[...] = jnp.zeros(l_s.shape, F32)
        acc_s[...] = jnp.zeros(acc_s.shape, F32)
        blocks = _lane_cumsum(lf_ref[0])
        per_step = tk // MXU_N
        for c, blk in enumerate(blocks):
            cum_s[c // per_step, :, (c % per_step) * MXU_N:(c % per_step + 1) * MXU_N] = blk * LOG2E

    def q_block_diag(g):
        qg = q_ref[0, :, g * gw:(g + 1) * gw]
        qrep = jnp.concatenate([qg] * hg, axis=0)
        return jnp.where(own, qrep, jnp.zeros_like(qrep))

    def online_update(g, cols, pv_fn):
        top = functools.reduce(jnp.maximum, cols)
        m_old = m_s[g]
        m_new = jnp.maximum(m_old, jnp.max(top, axis=1, keepdims=True))
        alpha = jnp.exp2(m_old - m_new)
        probs = [jnp.exp2(col - m_new[:, 0:col.shape[1]]) for col in cols]
        if cols[0].shape[1] == LANES:
            psum = functools.reduce(jnp.add, probs)
        else:
            lane0 = lax.broadcasted_iota(jnp.int32, (rows, LANES), 1) == 0
            psum = jnp.where(lane0, jnp.sum(probs[0], axis=1, keepdims=True), 0.0)
        l_s[g] = alpha * l_s[g] + psum
        p = probs[0] if len(probs) == 1 else jnp.concatenate(probs, axis=1)
        acc_s[g] = jnp.concatenate([alpha] * (gw // LANES), axis=1) * acc_s[g] + pv_fn(p.astype(BF16))
        m_s[g] = m_new

    cum_blk = cum_s[j]
    for g in range(n_groups):
        kt = kt_ref[0, g * gw:(g + 1) * gw, :].astype(BF16)
        s = _dot(q_block_diag(g), kt) - _group_rows(cum_blk[g * hg:(g + 1) * hg, :], tq)
        cols = [s[:, c * LANES:(c + 1) * LANES] for c in range(tk // LANES)]
        vt = vt_ref[0, g * gw:(g + 1) * gw, :].astype(BF16)
        online_update(g, cols, lambda p, vt=vt: _dot_nt(p, vt))

    @pl.when(j == pl.num_programs(1) - 1)
    def _():
        t_q = lax.broadcasted_iota(jnp.int32, (rows, tq), 0) % tq
        t_k = lax.broadcasted_iota(jnp.int32, (rows, tq), 1)
        causal = t_k <= t_q
        upper = (lax.broadcasted_iota(jnp.int32, (tq, tq), 0)
                 <= lax.broadcasted_iota(jnp.int32, (tq, tq), 1)).astype(BF16)
        new_cum = sum(_dot(piece, upper) for piece in _split3(nlf_ref[0]))
        last = cum_s[pl.num_programs(1) - 1][:, tk - 1:tk]
        new_cum = new_cum * LOG2E + last
        lane_grp = lax.broadcasted_iota(jnp.int32, (tq, gw), 1) // B_HD
        for g in range(n_groups):
            kn = nk_ref[0, :, g * gw:(g + 1) * gw]
            s = _dot_nt(q_block_diag(g), kn) - _group_rows(new_cum[g * hg:(g + 1) * hg, :], tq)
            s = jnp.where(causal, s, NEG_BIG)
            vn = nv_ref[0, :, g * gw:(g + 1) * gw]
            online_update(g, [s], lambda p, vn=vn: _dot(p, vn))
            o = acc_s[g] / jnp.sum(l_s[g], axis=1, keepdims=True)
            picked = functools.reduce(
                jnp.add, [jnp.where(lane_grp == r, o[r * tq:(r + 1) * tq, :], 0.0) for r in range(hg)])
            gate = gate_ref[0, :, g * gw:(g + 1) * gw]
            out_ref[0, :, g * gw:(g + 1) * gw] = (picked * _sigmoid(gate)).astype(out_ref.dtype)


def _sample_attn(q, cache_kt, cache_vt, cache_logf_t, kb, vb, new_logf_t, gate, tk):
    bsz, tq, d = q.shape
    past = cache_kt.shape[2]
    n_groups = d // MXU_N
    one = lambda b, j: (b, 0, 0)
    blk = lambda b, j: (b, 0, j)
    return pl.pallas_call(
        functools.partial(_sample_attn_kernel, tk=tk),
        grid=(bsz, past // tk),
        in_specs=[pl.BlockSpec((1, tq, d), one), pl.BlockSpec((1, d, tk), blk), pl.BlockSpec((1, d, tk), blk),
                  pl.BlockSpec((1, B_HEADS, past), one),
                  pl.BlockSpec((1, tq, d), one), pl.BlockSpec((1, tq, d), one),
                  pl.BlockSpec((1, B_HEADS, tq), one), pl.BlockSpec((1, tq, d), one)],
        out_specs=pl.BlockSpec((1, tq, d), one),
        out_shape=jax.ShapeDtypeStruct((bsz, tq, d), BF16),
        scratch_shapes=[pltpu.VMEM((past // tk, B_HEADS, tk), F32),
                        pltpu.VMEM((n_groups, MXU_N // B_HD * tq, LANES), F32),
                        pltpu.VMEM((n_groups, MXU_N // B_HD * tq, LANES), F32),
                        pltpu.VMEM((n_groups, MXU_N // B_HD * tq, MXU_N), F32)],
        compiler_params=_params("parallel", "arbitrary"),
        name="sample_attn",
    )(q, cache_kt, cache_vt, cache_logf_t, kb, vb, new_logf_t, gate)


def _prep_weights(g_mix, g_ffn, w_in_a, b_i_a, b_f_a, g_head_a, w_out_a, g_kv, w_kvf, b_f_b, g_k,
                  w_qg_b, g_q_b, w_o_b, w_gu, w_down):
    d = g_mix.shape[1]
    main_w = 2 * A_HEADS * A_DQK + 2 * A_HEADS * A_DV
    w_in = w_in_a[0]
    gates_b = jnp.concatenate([b_i_a[0], b_f_a[0]])

    def ffn(layer):
        return w_gu[layer].astype(BF16), w_down[layer].astype(BF16)

    return dict(
        g_mix0=g_mix[0][None], g_mix1=g_mix[1][None], g_ffn0=g_ffn[0][None], g_ffn1=g_ffn[1][None],
        w_in=w_in[:, :main_w].astype(BF16), wg2=_hi_lo_cols(w_in[:, main_w:]),
        gb_col=gates_b[None, :], gb_row=gates_b[:, None], g_head=g_head_a[0][None],
        w_out=w_out_a[0].astype(BF16),
        g_kv=g_kv[None], w_kv=w_kvf[:, :2 * d].astype(BF16), wf2=_hi_lo_cols(w_kvf[:, 2 * d:]), b_f=b_f_b[None],
        gk_t=jnp.tile(g_k, B_HEADS)[None], gq_t=jnp.tile(g_q_b[0], B_HEADS)[None],
        w_qg=w_qg_b[0].astype(BF16), w_o=w_o_b[0].astype(BF16),
        ffn0=ffn(0), ffn1=ffn(1),
    )


def _state_to_pairs(c, n, m):
    bsz = c.shape[0]
    np_ = A_HEADS // 2
    ct = c.reshape(bsz, np_, 2, A_DV, A_DQK).transpose(0, 1, 2, 4, 3).reshape(bsz, np_, 2 * A_DQK, A_DV)
    n2 = n.reshape(bsz, np_, 1, 2 * A_DQK)
    m2 = jnp.broadcast_to(m[:, :, None, None], (bsz, A_HEADS, 1, LANES))
    return ct.astype(F32), n2.astype(F32), m2.astype(F32)


def _state_from_pairs(ct, n2, m2):
    bsz = ct.shape[0]
    np_ = A_HEADS // 2
    c = ct.reshape(bsz, np_, 2, A_DQK, A_DV).transpose(0, 1, 2, 4, 3).reshape(bsz, 1, A_HEADS, A_DV, A_DQK)
    n = n2.reshape(bsz, 1, A_HEADS, A_DQK)
    m = m2[:, :, 0, 0].reshape(bsz, 1, A_HEADS)
    return c, n, m


def _layer_a(x, c0, n0, m0, w, chunk, tm):
    bsz, t, d = x.shape
    x2d = x.reshape(bsz * t, d)
    q, k, v, o, gates = _in_proj(x2d, w["g_mix0"], w["w_in"], w["wg2"], tm)
    r3 = lambda z: z.reshape(bsz, t, z.shape[1])
    gates3 = r3(gates)
    ct0, n20, m20 = _state_to_pairs(c0, n0, m0)
    hn, ct, n2, m2 = _mlstm(r3(q), r3(k), r3(v), r3(o), gates3, gates3.transpose(0, 2, 1),
                            w["gb_col"], w["gb_row"], w["g_head"], ct0, n20, m20, chunk,
                            MLSTM_STREAMS if bsz % MLSTM_STREAMS == 0 else 1)
    x2 = _mix_ffn(hn.reshape(bsz * t, -1), x2d, w["w_out"], w["g_ffn0"], *w["ffn0"], tm)
    return x2, _state_from_pairs(ct, n2, m2)


def _fox_inputs(x2, w, tm, seq_t=None):
    return _fox_proj(x2, w["g_kv"], w["g_mix1"], w["w_kv"], w["wf2"], w["w_qg"],
                     w["b_f"], w["gk_t"], w["gq_t"], tm, seq_t)


def _prompt_trunk(x, w):
    bsz, t, d = x.shape
    tm = min(bsz * t, ROW_TILE)
    zeros = lambda *s: jnp.zeros(s, F32)
    x2, (c, n, m) = _layer_a(x, zeros(bsz, A_HEADS, A_DV, A_DQK), zeros(bsz, A_HEADS, A_DQK),
                             zeros(bsz, A_HEADS), w, min(t, MLSTM_CHUNK), tm)
    kt, vt, logf, kb_t, vb, q, gate = _fox_inputs(x2, w, tm, seq_t=t)
    r3 = lambda z: z.reshape(bsz, t, z.shape[1])
    logf3 = r3(logf)
    cum_aug = _cum_bias(logf3, min(t, ATTN_BLOCK))
    attn = _prompt_attn(r3(q), kb_t, cum_aug, r3(vb), r3(gate), min(t, ATTN_BLOCK))
    y = _mix_ffn(attn.reshape(bsz * t, d), x2, w["w_o"], w["g_ffn1"], *w["ffn1"], tm)
    to_out = lambda z: z.reshape(bsz, B_HEADS, B_HD, t).transpose(0, 3, 1, 2)
    return (y.reshape(bsz, t, d), to_out(kt), to_out(vt), logf3, c, n, m)


def _sample_trunk(x, c0, n0, m0, cache_k, cache_v, cache_logf, w):
    bsz, t, d = x.shape
    past = cache_k.shape[1]
    tm = min(bsz * t, ROW_TILE)
    x2, (c, n, m) = _layer_a(x, c0[:, 0], n0[:, 0], m0[:, 0], w, min(t, MLSTM_CHUNK), tm)
    k, v, logf, kb, vb, q, gate = _fox_inputs(x2, w, tm)
    r3 = lambda z: z.reshape(bsz, t, z.shape[1])
    logf3 = r3(logf)
    kt = cache_k.transpose(0, 2, 3, 1).reshape(bsz, d, past)
    vt = cache_v.transpose(0, 2, 3, 1).reshape(bsz, d, past)
    attn = _sample_attn(r3(q), kt, vt, cache_logf.transpose(0, 2, 1), r3(kb), r3(vb),
                        logf3.transpose(0, 2, 1), r3(gate), min(past, CACHE_KEY_BLOCK))
    y = _mix_ffn(attn.reshape(bsz * t, d), x2, w["w_o"], w["g_ffn1"], *w["ffn1"], tm)
    return (y.reshape(bsz, t, d), k.reshape(bsz, t, B_HEADS, B_HD), v.reshape(bsz, t, B_HEADS, B_HD),
            logf3, c, n, m)


def kernel(x_prompt, x_sample, cache_k, cache_v, cache_logf, state_c, state_n, state_m, g_mix, g_ffn, w_in_a, b_i_a, b_f_a, g_head_a, w_out_a, g_kv, w_kvf, b_f_b, g_k, w_qg_b, g_q_b, w_o_b, w_gu, w_down):
    w = _prep_weights(g_mix, g_ffn, w_in_a, b_i_a, b_f_a, g_head_a, w_out_a, g_kv, w_kvf, b_f_b, g_k,
                      w_qg_b, g_q_b, w_o_b, w_gu, w_down)
    y_p, k_p, v_p, f_p, c_p, n_p, m_p = _prompt_trunk(x_prompt, w)
    y_s, k_s, v_s, f_s, c_s, n_s, m_s = _sample_trunk(x_sample, state_c, state_n, state_m,
                                                      cache_k, cache_v, cache_logf, w)
    return (y_p, y_s, k_p, v_p, f_p, c_p, n_p, m_p, k_s, v_s, f_s, c_s, n_s, m_s)
```

```python
import functools

import jax
import jax.numpy as jnp
from jax import lax
from jax.experimental import pallas as pl
from jax.experimental.pallas import tpu as pltpu

F32 = jnp.float32
BF16 = jnp.bfloat16

EPS = 1e-6
GATE_CAP = 15.0
A_HEADS = 8
A_DQK = 64
A_DV = 128
B_HEADS = 16
B_HD = 64
LANES = 128
HALF = 64
MXU_N = 256
FF_CHUNK = MXU_N
NEG_BIG = -1e30
LOG2E = 1.4426950408889634
N_BIAS_PIECES = 3
VMEM_LIMIT = 56 * 1024 * 1024

ROW_TILE = 512
MLSTM_CHUNK = 128
MLSTM_STREAMS = 2
ATTN_BLOCK = 512
CACHE_KEY_BLOCK = 2048


def _params(*sem):
    return pltpu.CompilerParams(dimension_semantics=sem, vmem_limit_bytes=VMEM_LIMIT)


def _resident(shape):
    nd = len(shape)
    return pl.BlockSpec(shape, lambda *_: (0,) * nd, pipeline_mode=pl.Buffered(1))


def _dot(a, b):
    return jnp.dot(a, b, preferred_element_type=F32)


def _dot_nt(a, b):
    return lax.dot_general(a, b, (((1,), (1,)), ((), ())), preferred_element_type=F32)


def _split2(x):
    hi = x.astype(BF16)
    lo = (x - hi.astype(F32)).astype(BF16)
    return hi, lo


def _split3(x):
    hi = x.astype(BF16)
    r = x - hi.astype(F32)
    mid = r.astype(BF16)
    lo = (r - mid.astype(F32)).astype(BF16)
    return hi, mid, lo


def _hi_lo_cols(w):
    hi, lo = _split2(w)
    both = jnp.concatenate([hi, lo], axis=1)
    return jnp.pad(both, ((0, 0), (0, LANES - both.shape[1])))


def _log_sigmoid(x):
    return jnp.minimum(x, 0.0) - jnp.log(1.0 + jnp.exp(-jnp.abs(x)))


def _sigmoid(x):
    return 1.0 / (1.0 + jnp.exp(-x))


def _soft_cap(z):
    return GATE_CAP * jnp.tanh(z / GATE_CAP)


def _rms_hat(x):
    return x * lax.rsqrt(jnp.mean(x * x, axis=-1, keepdims=True) + EPS)


def _head_norm_group(xc, gc, lo_mask):
    sq = xc * xc
    s_lo = jnp.sum(jnp.where(lo_mask, sq, 0.0), axis=-1, keepdims=True)
    s_hi = jnp.sum(jnp.where(lo_mask, 0.0, sq), axis=-1, keepdims=True)
    ms = jnp.where(lo_mask, s_lo, s_hi) * (1.0 / B_HD)
    return xc * lax.rsqrt(ms + EPS) * gc


def _in_proj_kernel(x_ref, g_ref, w_ref, wg2_ref, q_ref, k_ref, v_ref, o_ref, gates_ref):
    xh = (_rms_hat(x_ref[...]) * g_ref[...]).astype(BF16)
    qk_w = A_HEADS * A_DQK
    v_w = A_HEADS * A_DV
    q_ref[...] = _dot(xh, w_ref[:, 0:qk_w]).astype(BF16)
    k_ref[...] = (_dot(xh, w_ref[:, qk_w:2 * qk_w]) * (A_DQK ** -0.5)).astype(BF16)
    v_ref[...] = _dot(xh, w_ref[:, 2 * qk_w:2 * qk_w + v_w]).astype(BF16)
    o_ref[...] = _dot(xh, w_ref[:, 2 * qk_w + v_w:2 * qk_w + 2 * v_w])
    gz = _dot(xh, wg2_ref[...])
    gates_ref[...] = gz[:, 0:2 * A_HEADS] + gz[:, 2 * A_HEADS:4 * A_HEADS]


def _in_proj(x2d, g, w_main, wg2, tm):
    m, d = x2d.shape
    qk_w = A_HEADS * A_DQK
    v_w = A_HEADS * A_DV
    row = lambda i: (i, 0)
    return pl.pallas_call(
        _in_proj_kernel,
        grid=(m // tm,),
        in_specs=[pl.BlockSpec((tm, d), row), _resident(g.shape), _resident(w_main.shape),
                  _resident(wg2.shape)],
        out_specs=[pl.BlockSpec((tm, qk_w), row), pl.BlockSpec((tm, qk_w), row),
                   pl.BlockSpec((tm, v_w), row), pl.BlockSpec((tm, v_w), row),
                   pl.BlockSpec((tm, 2 * A_HEADS), row)],
        out_shape=[jax.ShapeDtypeStruct((m, qk_w), BF16), jax.ShapeDtypeStruct((m, qk_w), BF16),
                   jax.ShapeDtypeStruct((m, v_w), BF16), jax.ShapeDtypeStruct((m, v_w), F32),
                   jax.ShapeDtypeStruct((m, 2 * A_HEADS), F32)],
        compiler_params=_params("parallel"),
        name="in_proj",
    )(x2d, g, w_main, wg2)


def _mlstm_kernel(q_ref, k_ref, v_ref, o_ref, gc_ref, gr_ref, bc_ref, br_ref, gh_ref,
                  c0_ref, n0_ref, m0_ref,
                  h_ref, c_out, n_out, m_out, ct_s, n_s, m_s, *, chunk):
    j = pl.program_id(1)
    L = chunk
    n_streams = q_ref.shape[0]

    @pl.when(j == 0)
    def _():
        ct_s[...] = c0_ref[...]
        n_s[...] = n0_ref[...]
        m_s[...] = m0_ref[...]

    t_idx = lax.broadcasted_iota(jnp.int32, (L, L), 0)
    s_idx = lax.broadcasted_iota(jnp.int32, (L, L), 1)
    causal = s_idx <= t_idx
    lane = lax.broadcasted_iota(jnp.int32, (1, LANES), 1)
    lo_lane = lane < HALF
    lo_row = lax.broadcasted_iota(jnp.int32, (LANES, LANES), 0) < HALF
    eye = (lax.broadcasted_iota(jnp.int32, (LANES, LANES), 0)
           == lax.broadcasted_iota(jnp.int32, (LANES, LANES), 1)).astype(BF16)
    place = (lax.broadcasted_iota(jnp.int32, (2 * A_HEADS, 2 * A_HEADS * LANES), 0)
             == lax.broadcasted_iota(jnp.int32, (2 * A_HEADS, 2 * A_HEADS * LANES), 1) // LANES).astype(BF16)

    def rep(col):
        return jnp.broadcast_to(col, (col.shape[0], LANES))

    def head_stages(b, h, gates, q2, k2, kt2, ct2b, qn, done):
        cols_rep, li_r, lf_r = gates
        half = lo_lane if h % 2 == 0 else jnp.logical_not(lo_lane)
        li_col = cols_rep[:, h * LANES:(h + 1) * LANES]
        lf_col = cols_rep[:, (A_HEADS + h) * LANES:(A_HEADS + h + 1) * LANES]
        li_row = li_r[h:h + 1, :]
        lf_row = lf_r[h:h + 1, :]
        m_prev = m_s[b, h]
        v_h = v_ref[b, :, h * A_DV:(h + 1) * A_DV]
        qm = jnp.where(half, q2, jnp.zeros_like(q2))
        qk = _dot(qm, kt2)
        qc = _dot(qm, ct2b)
        qn_sum = rep(jnp.sum(jnp.where(half, qn, 0.0), axis=1, keepdims=True))
        b_col = rep(jnp.sum(jnp.where(causal, lf_row, 0.0), axis=1, keepdims=True))
        b_row = jnp.sum(jnp.where(t_idx <= s_idx, lf_col[:, :L], 0.0), axis=0, keepdims=True)
        b_last = rep(jnp.sum(lf_row, axis=1, keepdims=True))
        yield
        dmat = jnp.where(causal, b_col[:, :L] - b_row + li_row, NEG_BIG)
        inter = b_col + m_prev
        m_t = jnp.maximum(inter, rep(jnp.max(dmat, axis=1, keepdims=True)))
        w_col = b_last - b_col + li_col
        m_new = jnp.maximum(b_last + m_prev, jnp.max(w_col, axis=0, keepdims=True))
        yield
        g = jnp.exp(inter - m_t)
        dec = jnp.where(causal, jnp.exp(dmat - m_t[:, :L]), 0.0)
        floor = jnp.exp(-m_t)
        g_c = jnp.exp(b_last + m_prev - m_new)
        ws = jnp.exp(w_col - m_new)
        yield
        s = qk * dec
        wsv = (ws * v_h.astype(F32)).astype(BF16)
        upd = _dot(kt2, wsv)
        num = g * qc + _dot(s.astype(BF16), v_h)
        den = g * qn_sum + rep(jnp.sum(s, axis=1, keepdims=True))
        yield
        hv = num / jnp.maximum(jnp.abs(den), floor)
        ms = rep(jnp.mean(hv * hv, axis=-1, keepdims=True))
        yield
        hv = hv * lax.rsqrt(ms + EPS)
        hv = hv * gh_ref[:, h * A_DV:(h + 1) * A_DV]
        hv = hv * _sigmoid(o_ref[b, :, h * A_DV:(h + 1) * A_DV])
        h_ref[b, :, h * A_DV:(h + 1) * A_DV] = hv.astype(h_ref.dtype)
        m_s[b, h] = m_new
        done[b, h] = (upd, g_c, ws)

    pairs = []
    done = {}
    gens = []
    for b in range(n_streams):
        cap_c = _soft_cap(gc_ref[b] + bc_ref[...])
        cap_r = _soft_cap(gr_ref[b] + br_ref[...])
        li_c = cap_c[:, 0:A_HEADS]
        lf_c = _log_sigmoid(cap_c[:, A_HEADS:2 * A_HEADS])
        cols_rep = sum(_dot(piece, place) for piece in _split3(jnp.concatenate([li_c, lf_c], axis=1)))
        gates = (cols_rep, cap_r[0:A_HEADS, :], _log_sigmoid(cap_r[A_HEADS:2 * A_HEADS, :]))
        for p in range(A_HEADS // 2):
            q2 = q_ref[b, :, p * LANES:(p + 1) * LANES]
            k2 = k_ref[b, :, p * LANES:(p + 1) * LANES]
            kt2 = _dot_nt(eye, k2).astype(BF16)
            ct2 = ct_s[b, p]
            n2 = n_s[b, p]
            pairs.append((b, p, k2, ct2, n2))
            for hh in range(2):
                gens.append(head_stages(b, 2 * p + hh, gates, q2, k2, kt2, ct2.astype(BF16),
                                        q2.astype(F32) * n2, done))
    while gens:
        alive = []
        for gen in gens:
            try:
                next(gen)
                alive.append(gen)
            except StopIteration:
                pass
        gens = alive

    for b, p, k2, ct2, n2 in pairs:
        (upd0, gc0, ws0), (upd1, gc1, ws1) = done[b, 2 * p], done[b, 2 * p + 1]
        ct_s[b, p] = jnp.where(lo_row, gc0, gc1) * ct2 + jnp.where(lo_row, upd0, upd1)
        ws2 = jnp.where(lo_lane, ws0, ws1)
        n_s[b, p] = (jnp.where(lo_lane, gc0, gc1) * n2
                     + jnp.sum(ws2 * k2.astype(F32), axis=0, keepdims=True))

    @pl.when(j == pl.num_programs(1) - 1)
    def _():
        c_out[...] = ct_s[...]
        n_out[...] = n_s[...]
        m_out[...] = m_s[...]


def _mlstm(q, k, v, o, gates, gates_t, b_col, b_row, g_head, ct0, n0, m0, chunk, streams):
    bsz, t, _ = q.shape
    nc = t // chunk
    np_ = A_HEADS // 2
    sb = streams
    tok = lambda b, j: (b, j, 0)
    st4 = lambda b, j: (b, 0, 0, 0)
    return pl.pallas_call(
        functools.partial(_mlstm_kernel, chunk=chunk),
        grid=(bsz // sb, nc),
        in_specs=[pl.BlockSpec((sb, chunk, q.shape[2]), tok), pl.BlockSpec((sb, chunk, k.shape[2]), tok),
                  pl.BlockSpec((sb, chunk, v.shape[2]), tok), pl.BlockSpec((sb, chunk, o.shape[2]), tok),
                  pl.BlockSpec((sb, chunk, 2 * A_HEADS), tok),
                  pl.BlockSpec((sb, 2 * A_HEADS, chunk), lambda b, j: (b, 0, j)),
                  _resident(b_col.shape), _resident(b_row.shape), _resident(g_head.shape),
                  pl.BlockSpec((sb, np_, LANES, A_DV), st4), pl.BlockSpec((sb, np_, 1, LANES), st4),
                  pl.BlockSpec((sb, A_HEADS, 1, LANES), st4)],
        out_specs=[pl.BlockSpec((sb, chunk, v.shape[2]), tok),
                   pl.BlockSpec((sb, np_, LANES, A_DV), st4), pl.BlockSpec((sb, np_, 1, LANES), st4),
                   pl.BlockSpec((sb, A_HEADS, 1, LANES), st4)],
        out_shape=[jax.ShapeDtypeStruct((bsz, t, v.shape[2]), BF16),
                   jax.ShapeDtypeStruct((bsz, np_, LANES, A_DV), F32),
                   jax.ShapeDtypeStruct((bsz, np_, 1, LANES), F32),
                   jax.ShapeDtypeStruct((bsz, A_HEADS, 1, LANES), F32)],
        scratch_shapes=[pltpu.VMEM((sb, np_, LANES, A_DV), F32), pltpu.VMEM((sb, np_, 1, LANES), F32),
                        pltpu.VMEM((sb, A_HEADS, 1, LANES), F32)],
        compiler_params=_params("parallel", "arbitrary"),
        name="mlstm",
    )(q, k, v, o, gates, gates_t, b_col, b_row, g_head, ct0, n0, m0)


def _mix_ffn_kernel(a_ref, res_ref, wo_ref, g_ref, wgu_ref, wd_ref, out_ref, xn_s, acc_s):
    x1 = res_ref[...] + _dot(a_ref[...], wo_ref[...])
    xn_s[...] = (_rms_hat(x1) * g_ref[...]).astype(BF16)
    acc_s[...] = x1

    d_ff = wd_ref.shape[0]
    for c in range(d_ff // FF_CHUNK):
        lo, hi = c * FF_CHUNK, (c + 1) * FF_CHUNK
        xn = xn_s[...]
        gate = _dot(xn, wgu_ref[:, lo:hi])
        up = _dot(xn, wgu_ref[:, d_ff + lo:d_ff + hi])
        act = (gate * _sigmoid(gate) * up).astype(BF16)
        acc_s[...] += _dot(act, wd_ref[lo:hi, :])
    out_ref[...] = acc_s[...]


def _mix_ffn(a, res, w_o, g, wgu, wd, tm):
    m, d = res.shape
    row = lambda i: (i, 0)
    return pl.pallas_call(
        _mix_ffn_kernel,
        grid=(m // tm,),
        in_specs=[pl.BlockSpec((tm, a.shape[1]), row), pl.BlockSpec((tm, d), row),
                  _resident(w_o.shape), _resident(g.shape), _resident(wgu.shape), _resident(wd.shape)],
        out_specs=pl.BlockSpec((tm, d), row),
        out_shape=jax.ShapeDtypeStruct((m, d), F32),
        scratch_shapes=[pltpu.VMEM((tm, d), BF16), pltpu.VMEM((tm, d), F32)],
        compiler_params=_params("parallel"),
        name="mix_ffn",
    )(a, res, w_o, g, wgu, wd)


def _fox_proj_kernel(x_ref, gkv_ref, gmix_ref, wkv_ref, wf2_ref, wqg_ref, bf_ref, gk_ref, gq_ref,
                     k_ref, v_ref, logf_ref, kb_ref, vb_ref, q_ref, gate_ref):
    d = x_ref.shape[1]
    xhat = _rms_hat(x_ref[...])
    sh = (xhat * gkv_ref[...]).astype(BF16)
    hq = (xhat * gmix_ref[...]).astype(BF16)
    lo_lane = lax.broadcasted_iota(jnp.int32, (1, LANES), 1) < HALF

    fz = _dot(sh, wf2_ref[...])
    logf_ref[...] = _log_sigmoid(fz[:, 0:B_HEADS] + fz[:, B_HEADS:2 * B_HEADS] + bf_ref[...])

    for c in range(d // MXU_N):
        kraw = _dot(sh, wkv_ref[:, c * MXU_N:(c + 1) * MXU_N])
        qraw = _dot(hq, wqg_ref[:, c * MXU_N:(c + 1) * MXU_N])
        for u in range(MXU_N // LANES):
            us = slice(u * LANES, (u + 1) * LANES)
            cs = slice(c * MXU_N + u * LANES, c * MXU_N + (u + 1) * LANES)
            kn = _head_norm_group(kraw[:, us], gk_ref[:, cs], lo_lane)
            if k_ref.ndim == 3:
                kn_t = kn.T
                k_ref[0, cs, :] = kn_t
                kb_ref[0, cs, :] = kn_t.astype(BF16)
            else:
                k_ref[:, cs] = kn
                kb_ref[:, cs] = kn.astype(BF16)
            qn = _head_norm_group(qraw[:, us], gq_ref[:, cs], lo_lane)
            q_ref[:, cs] = (qn * (LOG2E * B_HD ** -0.5)).astype(BF16)
    for c in range(d // MXU_N):
        cs = slice(c * MXU_N, (c + 1) * MXU_N)
        v = _dot(sh, wkv_ref[:, d + c * MXU_N:d + (c + 1) * MXU_N])
        if v_ref.ndim == 3:
            v_ref[0, cs, :] = v.T
        else:
            v_ref[:, cs] = v
        vb_ref[:, cs] = v.astype(BF16)
    gate_ref[...] = _dot(hq, wqg_ref[:, d:2 * d])


def _fox_proj(x2d, g_kv, g_mix, w_kv, wf2, w_qg, b_f, gk_t, gq_t, tm, seq_t=None):
    m, d = x2d.shape
    row = lambda i: (i, 0)
    big = pl.BlockSpec((tm, d), row)
    if seq_t is None:
        kv_spec, kv_shape, kb_spec, kb_shape = big, (m, d), big, (m, d)
    else:
        per_seq = seq_t // tm
        kv_spec = kb_spec = pl.BlockSpec((1, d, tm), lambda i: (i // per_seq, 0, i % per_seq))
        kv_shape = kb_shape = (m // seq_t, d, seq_t)
    return pl.pallas_call(
        _fox_proj_kernel,
        grid=(m // tm,),
        in_specs=[big] + [_resident(a.shape) for a in (g_kv, g_mix, w_kv, wf2, w_qg, b_f, gk_t, gq_t)],
        out_specs=[kv_spec, kv_spec, pl.BlockSpec((tm, B_HEADS), row), kb_spec, big, big, big],
        out_shape=[jax.ShapeDtypeStruct(kv_shape, F32), jax.ShapeDtypeStruct(kv_shape, F32),
                   jax.ShapeDtypeStruct((m, B_HEADS), F32),
                   jax.ShapeDtypeStruct(kb_shape, BF16), jax.ShapeDtypeStruct((m, d), BF16),
                   jax.ShapeDtypeStruct((m, d), BF16), jax.ShapeDtypeStruct((m, d), F32)],
        compiler_params=_params("parallel"),
        name="fox_proj",
    )(x2d, g_kv, g_mix, w_kv, wf2, w_qg, b_f, gk_t, gq_t)


def _cum_bias_kernel(x_ref, aug_ref, carry_s):
    @pl.when(pl.program_id(1) == 0)
    def _():
        carry_s[...] = jnp.zeros(carry_s.shape, F32)

    tb = x_ref.shape[1]
    tri = (lax.broadcasted_iota(jnp.int32, (tb, tb), 1)
           <= lax.broadcasted_iota(jnp.int32, (tb, tb), 0)).astype(BF16)
    hi, mid, lo = _split3(x_ref[0])
    cum = _dot(tri, hi) + _dot(tri, mid) + _dot(tri, lo) + carry_s[...]
    carry_s[...] = cum[tb - 1:tb, :]

    h_idx = lax.broadcasted_iota(jnp.int32, (B_HEADS, aug_ref.shape[1]), 0)
    c_idx = lax.broadcasted_iota(jnp.int32, (B_HEADS, aug_ref.shape[1]), 1)
    base = N_BIAS_PIECES * h_idx
    aug = None
    for i, piece in enumerate(_split3(cum * (-LOG2E))):
        term = _dot(piece, (c_idx == base + i).astype(BF16))
        aug = term if aug is None else aug + term
    aug_ref[0] = aug.T.astype(BF16)


def _cum_bias(logf, tb):
    bsz, t, h = logf.shape
    width = LANES
    assert N_BIAS_PIECES * h <= width
    return pl.pallas_call(
        _cum_bias_kernel,
        grid=(bsz, t // tb),
        in_specs=[pl.BlockSpec((1, tb, h), lambda b, j: (b, j, 0))],
        out_specs=pl.BlockSpec((1, width, tb), lambda b, j: (b, 0, j)),
        out_shape=jax.ShapeDtypeStruct((bsz, width, t), BF16),
        scratch_shapes=[pltpu.VMEM((1, h), F32)],
        compiler_params=_params("parallel", "arbitrary"),
        name="cum_bias",
    )(logf)


def _attn_pair_scores(p, q2, kt2, ca_t, mask, m_s, lo_lane):
    tq = q2.shape[0]
    lane2 = lax.broadcasted_iota(jnp.int32, (tq, LANES), 1)
    rows = []
    for hh in range(2):
        h = 2 * p + hh
        half = lo_lane if hh == 0 else jnp.logical_not(lo_lane)
        qm = jnp.where(half, q2, jnp.zeros_like(q2))
        pick = jnp.logical_and(lane2 >= N_BIAS_PIECES * h, lane2 < N_BIAS_PIECES * (h + 1))
        rows.append(jnp.concatenate([qm, pick.astype(BF16)], axis=1))
    s2 = _dot(jnp.concatenate(rows, axis=0),
              jnp.concatenate([kt2, ca_t], axis=0))
    out = []
    for hh in range(2):
        s = s2[hh * tq:(hh + 1) * tq, :]
        cols = [s[:, c * LANES:(c + 1) * LANES] for c in range(s.shape[1] // LANES)]
        if mask is not None:
            cols = [jnp.where(mask[:, c * LANES:(c + 1) * LANES], col, NEG_BIG) for c, col in enumerate(cols)]
        top = functools.reduce(jnp.maximum, cols)
        m_old = m_s[2 * p + hh]
        m_new = jnp.maximum(m_old, jnp.max(top, axis=1, keepdims=True))
        out.append((cols, m_old, m_new))
    return out


def _attn_head_update(h, cols, m_old, m_new, v2, m_s, acc_s, lo_lane):
    half = lo_lane if h % 2 == 0 else jnp.logical_not(lo_lane)
    alpha = jnp.exp2(m_old - m_new)
    pr = jnp.concatenate([jnp.exp2(col - m_new).astype(BF16) for col in cols], axis=1)
    v_ones = jnp.where(half, v2, jnp.ones_like(v2))
    acc_s[h] = alpha * acc_s[h] + _dot(pr, v_ones)
    m_s[h] = m_new


def _attn_init(m_s, acc_s):
    m_s[...] = jnp.full(m_s.shape, NEG_BIG, F32)
    acc_s[...] = jnp.zeros(acc_s.shape, F32)


def _attn_finish(out_ref, gate_ref, acc_s, lo_lane):
    for p in range(B_HEADS // 2):
        a0, a1 = acc_s[2 * p], acc_s[2 * p + 1]
        num = jnp.where(lo_lane, a0, a1)
        den = pltpu.roll(jnp.where(lo_lane, a1, a0), HALF, axis=1)
        gate = gate_ref[0, :, p * LANES:(p + 1) * LANES]
        out_ref[0, :, p * LANES:(p + 1) * LANES] = (num / den * _sigmoid(gate)).astype(out_ref.dtype)


def _fold_block(r, c, nq):
    first = c <= r
    return jnp.where(first, r, nq - 1 - r), jnp.where(first, c, c - r - 1)


def _prompt_attn_kernel(q_ref, kt_ref, ca_ref, v_ref, gate_ref, out_ref, m_s, acc_s, *, blk, nq):
    i, j = _fold_block(pl.program_id(1), pl.program_id(2), nq)
    lo_lane = lax.broadcasted_iota(jnp.int32, (1, LANES), 1) < HALF

    @pl.when(j == 0)
    def _():
        _attn_init(m_s, acc_s)

    def run(masked):
        mask = None
        if masked:
            mask = (lax.broadcasted_iota(jnp.int32, (blk, blk), 1)
                    <= lax.broadcasted_iota(jnp.int32, (blk, blk), 0))
        ca = ca_ref[0]

        def scores(p):
            cs = slice(p * LANES, (p + 1) * LANES)
            return _attn_pair_scores(p, q_ref[0, :, cs], kt_ref[0, cs, :], ca, mask, m_s, lo_lane)

        pending = scores(0)
        for p in range(B_HEADS // 2):
            upcoming = scores(p + 1) if p + 1 < B_HEADS // 2 else None
            cs = slice(p * LANES, (p + 1) * LANES)
            for hh in range(2):
                _attn_head_update(2 * p + hh, *pending[hh], v_ref[0, :, cs], m_s, acc_s, lo_lane)
            pending = upcoming

    @pl.when(j < i)
    def _():
        run(False)

    @pl.when(j == i)
    def _():
        run(True)
        _attn_finish(out_ref, gate_ref, acc_s, lo_lane)


def _prompt_attn(q, kb_t, cum_aug, vb, gate, blk):
    bsz, t, d = q.shape
    nq = t // blk
    rows = max(nq // 2, 1)
    assert nq == 1 or nq % 2 == 0
    qmap = lambda b, r, c: (b, _fold_block(r, c, nq)[0], 0)
    kvmap = lambda b, r, c: (b, _fold_block(r, c, nq)[1], 0)
    ktmap = lambda b, r, c: (b, 0, _fold_block(r, c, nq)[1])
    return pl.pallas_call(
        functools.partial(_prompt_attn_kernel, blk=blk, nq=nq),
        grid=(bsz, rows, nq + 1 if nq > 1 else 1),
        in_specs=[pl.BlockSpec((1, blk, d), qmap), pl.BlockSpec((1, d, blk), ktmap),
                  pl.BlockSpec((1, cum_aug.shape[1], blk), ktmap), pl.BlockSpec((1, blk, d), kvmap),
                  pl.BlockSpec((1, blk, d), qmap)],
        out_specs=pl.BlockSpec((1, blk, d), qmap),
        out_shape=jax.ShapeDtypeStruct((bsz, t, d), BF16),
        scratch_shapes=[pltpu.VMEM((B_HEADS, blk, LANES), F32), pltpu.VMEM((B_HEADS, blk, LANES), F32)],
        compiler_params=_params("parallel", "arbitrary", "arbitrary"),
        name="prompt_attn",
    )(q, kb_t, cum_aug, vb, gate)


def _group_rows(x, tq):
    return jnp.concatenate([jnp.broadcast_to(x[r:r + 1, :], (tq, x.shape[1])) for r in range(x.shape[0])], axis=0)


def _lane_cumsum(x):
    r, n = x.shape
    nb = n // MXU_N
    stacked = jnp.concatenate([x[:, c * MXU_N:(c + 1) * MXU_N] for c in range(nb)], axis=0)
    upper = (lax.broadcasted_iota(jnp.int32, (MXU_N, MXU_N), 0)
             <= lax.broadcasted_iota(jnp.int32, (MXU_N, MXU_N), 1)).astype(BF16)
    within = sum(_dot(piece, upper) for piece in _split3(stacked))
    totals = jnp.broadcast_to(within[:, MXU_N - 1:MXU_N], (nb * r, LANES))
    ri = lax.broadcasted_iota(jnp.int32, (nb * r, nb * r), 0)
    ci = lax.broadcasted_iota(jnp.int32, (nb * r, nb * r), 1)
    earlier = jnp.logical_and(ci // r < ri // r, ci % r == ri % r).astype(BF16)
    offs = sum(_dot(earlier, piece) for piece in _split3(totals))
    full = within + jnp.concatenate([offs] * (MXU_N // LANES), axis=1)
    return [full[c * r:(c + 1) * r, :] for c in range(nb)]


def _sample_attn_kernel(q_ref, kt_ref, vt_ref, lf_ref, nk_ref, nv_ref, nlf_ref, gate_ref, out_ref,
                        cum_s, m_s, l_s, acc_s, *, tk):
    j = pl.program_id(1)
    tq = q_ref.shape[1]
    n_groups = acc_s.shape[0]
    hg = B_HEADS // n_groups
    gw = hg * B_HD
    rows = hg * tq
    row_head = lax.broadcasted_iota(jnp.int32, (rows, gw), 0) // tq
    lane_head = lax.broadcasted_iota(jnp.int32, (rows, gw), 1) // B_HD
    own = row_head == lane_head

    @pl.when(j == 0)
    def _():
        m_s[...] = jnp.full(m_s.shape, NEG_BIG, F32)
        l_s[...] = jnp.zeros(l_s.shape, F32)
        acc_s[...] = jnp.zeros(acc_s.shape, F32)
        blocks = _lane_cumsum(lf_ref[0])
        per_step = tk // MXU_N
        for c, blk in enumerate(blocks):
            cum_s[c // per_step, :, (c % per_step) * MXU_N:(c % per_step + 1) * MXU_N] = blk * LOG2E

    def q_block_diag(g):
        qg = q_ref[0, :, g * gw:(g + 1) * gw]
        qrep = jnp.concatenate([qg] * hg, axis=0)
        return jnp.where(own, qrep, jnp.zeros_like(qrep))

    def online_update(g, cols, pv_fn):
        top = functools.reduce(jnp.maximum, cols)
        m_old = m_s[g]
        m_new = jnp.maximum(m_old, jnp.max(top, axis=1, keepdims=True))
        alpha = jnp.exp2(m_old - m_new)
        probs = [jnp.exp2(col - m_new[:, 0:col.shape[1]]) for col in cols]
        if cols[0].shape[1] == LANES:
            psum = functools.reduce(jnp.add, probs)
        else:
            lane0 = lax.broadcasted_iota(jnp.int32, (rows, LANES), 1) == 0
            psum = jnp.where(lane0, jnp.sum(probs[0], axis=1, keepdims=True), 0.0)
        l_s[g] = alpha * l_s[g] + psum
        p = probs[0] if len(probs) == 1 else jnp.concatenate(probs, axis=1)
        acc_s[g] = jnp.concatenate([alpha] * (gw // LANES), axis=1) * acc_s[g] + pv_fn(p.astype(BF16))
        m_s[g] = m_new

    cum_blk = cum_s[j]
    for g in range(n_groups):
        kt = kt_ref[0, g * gw:(g + 1) * gw, :].astype(BF16)
        s = _dot(q_block_diag(g), kt) - _group_rows(cum_blk[g * hg:(g + 1) * hg, :], tq)
        cols = [s[:, c * LANES:(c + 1) * LANES] for c in range(tk // LANES)]
        vt = vt_ref[0, g * gw:(g + 1) * gw, :].astype(BF16)
        online_update(g, cols, lambda p, vt=vt: _dot_nt(p, vt))

    @pl.when(j == pl.num_programs(1) - 1)
    def _():
        t_q = lax.broadcasted_iota(jnp.int32, (rows, tq), 0) % tq
        t_k = lax.broadcasted_iota(jnp.int32, (rows, tq), 1)
        causal = t_k <= t_q
        upper = (lax.broadcasted_iota(jnp.int32, (tq, tq), 0)
                 <= lax.broadcasted_iota(jnp.int32, (tq, tq), 1)).astype(BF16)
        new_cum = sum(_dot(piece, upper) for piece in _split3(nlf_ref[0]))
        last = cum_s[pl.num_programs(1) - 1][:, tk - 1:tk]
        new_cum = new_cum * LOG2E + last
        lane_grp = lax.broadcasted_iota(jnp.int32, (tq, gw), 1) // B_HD
        for g in range(n_groups):
            kn = nk_ref[0, :, g * gw:(g + 1) * gw]
            s = _dot_nt(q_block_diag(g), kn) - _group_rows(new_cum[g * hg:(g + 1) * hg, :], tq)
            s = jnp.where(causal, s, NEG_BIG)
            vn = nv_ref[0, :, g * gw:(g + 1) * gw]
            online_update(g, [s], lambda p, vn=vn: _dot(p, vn))
            o = acc_s[g] / jnp.sum(l_s[g], axis=1, keepdims=True)
            picked = functools.reduce(
                jnp.add, [jnp.where(lane_grp == r, o[r * tq:(r + 1) * tq, :], 0.0) for r in range(hg)])
            gate = gate_ref[0, :, g * gw:(g + 1) * gw]
            out_ref[0, :, g * gw:(g + 1) * gw] = (picked * _sigmoid(gate)).astype(out_ref.dtype)


def _sample_attn(q, cache_kt, cache_vt, cache_logf_t, kb, vb, new_logf_t, gate, tk):
    bsz, tq, d = q.shape
    past = cache_kt.shape[2]
    n_groups = d // MXU_N
    one = lambda b, j: (b, 0, 0)
    blk = lambda b, j: (b, 0, j)
    return pl.pallas_call(
        functools.partial(_sample_attn_kernel, tk=tk),
        grid=(bsz, past // tk),
        in_specs=[pl.BlockSpec((1, tq, d), one), pl.BlockSpec((1, d, tk), blk), pl.BlockSpec((1, d, tk), blk),
                  pl.BlockSpec((1, B_HEADS, past), one),
                  pl.BlockSpec((1, tq, d), one), pl.BlockSpec((1, tq, d), one),
                  pl.BlockSpec((1, B_HEADS, tq), one), pl.BlockSpec((1, tq, d), one)],
        out_specs=pl.BlockSpec((1, tq, d), one),
        out_shape=jax.ShapeDtypeStruct((bsz, tq, d), BF16),
        scratch_shapes=[pltpu.VMEM((past // tk, B_HEADS, tk), F32),
                        pltpu.VMEM((n_groups, MXU_N // B_HD * tq, LANES), F32),
                        pltpu.VMEM((n_groups, MXU_N // B_HD * tq, LANES), F32),
                        pltpu.VMEM((n_groups, MXU_N // B_HD * tq, MXU_N), F32)],
        compiler_params=_params("parallel", "arbitrary"),
        name="sample_attn",
    )(q, cache_kt, cache_vt, cache_logf_t, kb, vb, new_logf_t, gate)


def _prep_weights(g_mix, g_ffn, w_in_a, b_i_a, b_f_a, g_head_a, w_out_a, g_kv, w_kvf, b_f_b, g_k,
                  w_qg_b, g_q_b, w_o_b, w_gu, w_down):
    d = g_mix.shape[1]
    main_w = 2 * A_HEADS * A_DQK + 2 * A_HEADS * A_DV
    w_in = w_in_a[0]
    gates_b = jnp.concatenate([b_i_a[0], b_f_a[0]])

    def ffn(layer):
        return w_gu[layer].astype(BF16), w_down[layer].astype(BF16)

    return dict(
        g_mix0=g_mix[0][None], g_mix1=g_mix[1][None], g_ffn0=g_ffn[0][None], g_ffn1=g_ffn[1][None],
        w_in=w_in[:, :main_w].astype(BF16), wg2=_hi_lo_cols(w_in[:, main_w:]),
        gb_col=gates_b[None, :], gb_row=gates_b[:, None], g_head=g_head_a[0][None],
        w_out=w_out_a[0].astype(BF16),
        g_kv=g_kv[None], w_kv=w_kvf[:, :2 * d].astype(BF16), wf2=_hi_lo_cols(w_kvf[:, 2 * d:]), b_f=b_f_b[None],
        gk_t=jnp.tile(g_k, B_HEADS)[None], gq_t=jnp.tile(g_q_b[0], B_HEADS)[None],
        w_qg=w_qg_b[0].astype(BF16), w_o=w_o_b[0].astype(BF16),
        ffn0=ffn(0), ffn1=ffn(1),
    )


def _state_to_pairs(c, n, m):
    bsz = c.shape[0]
    np_ = A_HEADS // 2
    ct = c.reshape(bsz, np_, 2, A_DV, A_DQK).transpose(0, 1, 2, 4, 3).reshape(bsz, np_, 2 * A_DQK, A_DV)
    n2 = n.reshape(bsz, np_, 1, 2 * A_DQK)
    m2 = jnp.broadcast_to(m[:, :, None, None], (bsz, A_HEADS, 1, LANES))
    return ct.astype(F32), n2.astype(F32), m2.astype(F32)


def _state_from_pairs(ct, n2, m2):
    bsz = ct.shape[0]
    np_ = A_HEADS // 2
    c = ct.reshape(bsz, np_, 2, A_DQK, A_DV).transpose(0, 1, 2, 4, 3).reshape(bsz, 1, A_HEADS, A_DV, A_DQK)
    n = n2.reshape(bsz, 1, A_HEADS, A_DQK)
    m = m2[:, :, 0, 0].reshape(bsz, 1, A_HEADS)
    return c, n, m


def _layer_a(x, c0, n0, m0, w, chunk, tm):
    bsz, t, d = x.shape
    x2d = x.reshape(bsz * t, d)
    q, k, v, o, gates = _in_proj(x2d, w["g_mix0"], w["w_in"], w["wg2"], tm)
    r3 = lambda z: z.reshape(bsz, t, z.shape[1])
    gates3 = r3(gates)
    ct0, n20, m20 = _state_to_pairs(c0, n0, m0)
    hn, ct, n2, m2 = _mlstm(r3(q), r3(k), r3(v), r3(o), gates3, gates3.transpose(0, 2, 1),
                            w["gb_col"], w["gb_row"], w["g_head"], ct0, n20, m20, chunk,
                            MLSTM_STREAMS if bsz % MLSTM_STREAMS == 0 else 1)
    x2 = _mix_ffn(hn.reshape(bsz * t, -1), x2d, w["w_out"], w["g_ffn0"], *w["ffn0"], tm)
    return x2, _state_from_pairs(ct, n2, m2)


def _fox_inputs(x2, w, tm, seq_t=None):
    return _fox_proj(x2, w["g_kv"], w["g_mix1"], w["w_kv"], w["wf2"], w["w_qg"],
                     w["b_f"], w["gk_t"], w["gq_t"], tm, seq_t)


def _prompt_trunk(x, w):
    bsz, t, d = x.shape
    tm = min(bsz * t, ROW_TILE)
    zeros = lambda *s: jnp.zeros(s, F32)
    x2, (c, n, m) = _layer_a(x, zeros(bsz, A_HEADS, A_DV, A_DQK), zeros(bsz, A_HEADS, A_DQK),
                             zeros(bsz, A_HEADS), w, min(t, MLSTM_CHUNK), tm)
    kt, vt, logf, kb_t, vb, q, gate = _fox_inputs(x2, w, tm, seq_t=t)
    r3 = lambda z: z.reshape(bsz, t, z.shape[1])
    logf3 = r3(logf)
    cum_aug = _cum_bias(logf3, min(t, ATTN_BLOCK))
    attn = _prompt_attn(r3(q), kb_t, cum_aug, r3(vb), r3(gate), min(t, ATTN_BLOCK))
    y = _mix_ffn(attn.reshape(bsz * t, d), x2, w["w_o"], w["g_ffn1"], *w["ffn1"], tm)
    to_out = lambda z: z.reshape(bsz, B_HEADS, B_HD, t).transpose(0, 3, 1, 2)
    return (y.reshape(bsz, t, d), to_out(kt), to_out(vt), logf3, c, n, m)


def _sample_trunk(x, c0, n0, m0, cache_k, cache_v, cache_logf, w):
    bsz, t, d = x.shape
    past = cache_k.shape[1]
    tm = min(bsz * t, ROW_TILE)
    x2, (c, n, m) = _layer_a(x, c0[:, 0], n0[:, 0], m0[:, 0], w, min(t, MLSTM_CHUNK), tm)
    k, v, logf, kb, vb, q, gate = _fox_inputs(x2, w, tm)
    r3 = lambda z: z.reshape(bsz, t, z.shape[1])
    logf3 = r3(logf)
    kt = cache_k.transpose(0, 2, 3, 1).reshape(bsz, d, past)
    vt = cache_v.transpose(0, 2, 3, 1).reshape(bsz, d, past)
    attn = _sample_attn(r3(q), kt, vt, cache_logf.transpose(0, 2, 1), r3(kb), r3(vb),
                        logf3.transpose(0, 2, 1), r3(gate), min(past, CACHE_KEY_BLOCK))
    y = _mix_ffn(attn.reshape(bsz * t, d), x2, w["w_o"], w["g_ffn1"], *w["ffn1"], tm)
    return (y.reshape(bsz, t, d), k.reshape(bsz, t, B_HEADS, B_HD), v.reshape(bsz, t, B_HEADS, B_HD),
            logf3, c, n, m)


def kernel(x_prompt, x_sample, cache_k, cache_v, cache_logf, state_c, state_n, state_m, g_mix, g_ffn, w_in_a, b_i_a, b_f_a, g_head_a, w_out_a, g_kv, w_kvf, b_f_b, g_k, w_qg_b, g_q_b, w_o_b, w_gu, w_down):
    w = _prep_weights(g_mix, g_ffn, w_in_a, b_i_a, b_f_a, g_head_a, w_out_a, g_kv, w_kvf, b_f_b, g_k,
                      w_qg_b, g_q_b, w_o_b, w_gu, w_down)
    y_p, k_p, v_p, f_p, c_p, n_p, m_p = _prompt_trunk(x_prompt, w)
    y_s, k_s, v_s, f_s, c_s, n_s, m_s = _sample_trunk(x_sample, state_c, state_n, state_m,
                                                      cache_k, cache_v, cache_logf, w)
    return (y_p, y_s, k_p, v_p, f_p, c_p, n_p, m_p, k_s, v_s, f_s, c_s, n_s, m_s)
```

```python
import functools

import jax
import jax.numpy as jnp
from jax import lax
from jax.experimental import pallas as pl
from jax.experimental.pallas import tpu as pltpu

F32 = jnp.float32
BF16 = jnp.bfloat16

EPS = 1e-6
GATE_CAP = 15.0
A_HEADS = 8
A_DQK = 64
A_DV = 128
B_HEADS = 16
B_HD = 64
LANES = 128
HALF = 64
MXU_N = 256
FF_CHUNK = MXU_N
NEG_BIG = -1e30
LOG2E = 1.4426950408889634
N_BIAS_PIECES = 3
VMEM_LIMIT = 56 * 1024 * 1024

ROW_TILE = 512
MLSTM_CHUNK = 128
MLSTM_STREAMS = 2
ATTN_BLOCK = 512
CACHE_KEY_BLOCK = 2048


def _params(*sem):
    return pltpu.CompilerParams(dimension_semantics=sem, vmem_limit_bytes=VMEM_LIMIT)


def _resident(shape):
    nd = len(shape)
    return pl.BlockSpec(shape, lambda *_: (0,) * nd, pipeline_mode=pl.Buffered(1))


def _dot(a, b):
    return jnp.dot(a, b, preferred_element_type=F32)


def _dot_nt(a, b):
    return lax.dot_general(a, b, (((1,), (1,)), ((), ())), preferred_element_type=F32)


def _split2(x):
    hi = x.astype(BF16)
    lo = (x - hi.astype(F32)).astype(BF16)
    return hi, lo


def _split3(x):
    hi = x.astype(BF16)
    r = x - hi.astype(F32)
    mid = r.astype(BF16)
    lo = (r - mid.astype(F32)).astype(BF16)
    return hi, mid, lo


def _hi_lo_cols(w):
    hi, lo = _split2(w)
    both = jnp.concatenate([hi, lo], axis=1)
    return jnp.pad(both, ((0, 0), (0, LANES - both.shape[1])))


def _log_sigmoid(x):
    return jnp.minimum(x, 0.0) - jnp.log(1.0 + jnp.exp(-jnp.abs(x)))


def _sigmoid(x):
    return 1.0 / (1.0 + jnp.exp(-x))


def _soft_cap(z):
    return GATE_CAP * jnp.tanh(z / GATE_CAP)


def _rms_hat(x):
    return x * lax.rsqrt(jnp.mean(x * x, axis=-1, keepdims=True) + EPS)


def _head_norm_group(xc, gc, lo_mask):
    sq = xc * xc
    s_lo = jnp.sum(jnp.where(lo_mask, sq, 0.0), axis=-1, keepdims=True)
    s_hi = jnp.sum(jnp.where(lo_mask, 0.0, sq), axis=-1, keepdims=True)
    ms = jnp.where(lo_mask, s_lo, s_hi) * (1.0 / B_HD)
    return xc * lax.rsqrt(ms + EPS) * gc


def _in_proj_kernel(x_ref, g_ref, w_ref, wg2_ref, q_ref, k_ref, v_ref, o_ref, gates_ref):
    xh = (_rms_hat(x_ref[...]) * g_ref[...]).astype(BF16)
    qk_w = A_HEADS * A_DQK
    v_w = A_HEADS * A_DV
    q_ref[...] = _dot(xh, w_ref[:, 0:qk_w]).astype(BF16)
    k_ref[...] = (_dot(xh, w_ref[:, qk_w:2 * qk_w]) * (A_DQK ** -0.5)).astype(BF16)
    v_ref[...] = _dot(xh, w_ref[:, 2 * qk_w:2 * qk_w + v_w]).astype(BF16)
    o_ref[...] = _dot(xh, w_ref[:, 2 * qk_w + v_w:2 * qk_w + 2 * v_w])
    gz = _dot(xh, wg2_ref[...])
    gates_ref[...] = gz[:, 0:2 * A_HEADS] + gz[:, 2 * A_HEADS:4 * A_HEADS]


def _in_proj(x2d, g, w_main, wg2, tm):
    m, d = x2d.shape
    qk_w = A_HEADS * A_DQK
    v_w = A_HEADS * A_DV
    row = lambda i: (i, 0)
    return pl.pallas_call(
        _in_proj_kernel,
        grid=(m // tm,),
        in_specs=[pl.BlockSpec((tm, d), row), _resident(g.shape), _resident(w_main.shape),
                  _resident(wg2.shape)],
        out_specs=[pl.BlockSpec((tm, qk_w), row), pl.BlockSpec((tm, qk_w), row),
                   pl.BlockSpec((tm, v_w), row), pl.BlockSpec((tm, v_w), row),
                   pl.BlockSpec((tm, 2 * A_HEADS), row)],
        out_shape=[jax.ShapeDtypeStruct((m, qk_w), BF16), jax.ShapeDtypeStruct((m, qk_w), BF16),
                   jax.ShapeDtypeStruct((m, v_w), BF16), jax.ShapeDtypeStruct((m, v_w), F32),
                   jax.ShapeDtypeStruct((m, 2 * A_HEADS), F32)],
        compiler_params=_params("parallel"),
        name="in_proj",
    )(x2d, g, w_main, wg2)


def _mlstm_kernel(q_ref, k_ref, v_ref, o_ref, gc_ref, gr_ref, bc_ref, br_ref, gh_ref,
                  c0_ref, n0_ref, m0_ref,
                  h_ref, c_out, n_out, m_out, ct_s, n_s, m_s, *, chunk):
    j = pl.program_id(1)
    L = chunk
    n_streams = q_ref.shape[0]

    @pl.when(j == 0)
    def _():
        ct_s[...] = c0_ref[...]
        n_s[...] = n0_ref[...]
        m_s[...] = m0_ref[...] * LOG2E

    t_idx = lax.broadcasted_iota(jnp.int32, (L, L), 0)
    s_idx = lax.broadcasted_iota(jnp.int32, (L, L), 1)
    causal = s_idx <= t_idx
    lane = lax.broadcasted_iota(jnp.int32, (1, LANES), 1)
    lo_lane = lane < HALF
    lo_row = lax.broadcasted_iota(jnp.int32, (LANES, LANES), 0) < HALF
    eye = (lax.broadcasted_iota(jnp.int32, (LANES, LANES), 0)
           == lax.broadcasted_iota(jnp.int32, (LANES, LANES), 1)).astype(BF16)
    place = (lax.broadcasted_iota(jnp.int32, (2 * A_HEADS, 2 * A_HEADS * LANES), 0)
             == lax.broadcasted_iota(jnp.int32, (2 * A_HEADS, 2 * A_HEADS * LANES), 1) // LANES).astype(BF16)

    def rep(col):
        return jnp.broadcast_to(col, (col.shape[0], LANES))

    def head_stages(b, h, gates, q2, k2, kt2, ct2b, qn, done):
        cols_rep, li_r, lf_r = gates
        half = lo_lane if h % 2 == 0 else jnp.logical_not(lo_lane)
        li_col = cols_rep[:, h * LANES:(h + 1) * LANES]
        lf_col = cols_rep[:, (A_HEADS + h) * LANES:(A_HEADS + h + 1) * LANES]
        li_row = li_r[h:h + 1, :]
        lf_row = lf_r[h:h + 1, :]
        m_prev = m_s[b, h]
        v_h = v_ref[b, :, h * A_DV:(h + 1) * A_DV]
        qm = jnp.where(half, q2, jnp.zeros_like(q2))
        qk = _dot(qm, kt2)
        qc = _dot(qm, ct2b)
        qn_sum = rep(jnp.sum(jnp.where(half, qn, 0.0), axis=1, keepdims=True))
        b_col = rep(jnp.sum(jnp.where(causal, lf_row, 0.0), axis=1, keepdims=True))
        b_row = jnp.sum(jnp.where(t_idx <= s_idx, lf_col[:, :L], 0.0), axis=0, keepdims=True)
        b_last = rep(jnp.sum(lf_row, axis=1, keepdims=True))
        yield
        dmat = jnp.where(causal, b_col[:, :L] - b_row + li_row, NEG_BIG)
        inter = b_col + m_prev
        m_t = jnp.maximum(inter, rep(jnp.max(dmat, axis=1, keepdims=True)))
        w_col = b_last - b_col + li_col
        m_new = jnp.maximum(b_last + m_prev, jnp.max(w_col, axis=0, keepdims=True))
        yield
        g = jnp.exp2(inter - m_t)
        dec = jnp.where(causal, jnp.exp2(dmat - m_t[:, :L]), 0.0)
        floor = jnp.exp2(-m_t)
        g_c = jnp.exp2(b_last + m_prev - m_new)
        ws = jnp.exp2(w_col - m_new)
        yield
        s = qk * dec
        wsv = (ws * v_h.astype(F32)).astype(BF16)
        upd = _dot(kt2, wsv)
        num = g * qc + _dot(s.astype(BF16), v_h)
        den = g * qn_sum + rep(jnp.sum(s, axis=1, keepdims=True))
        yield
        hv = num / jnp.maximum(jnp.abs(den), floor)
        ms = rep(jnp.mean(hv * hv, axis=-1, keepdims=True))
        yield
        hv = hv * lax.rsqrt(ms + EPS)
        hv = hv * gh_ref[:, h * A_DV:(h + 1) * A_DV]
        hv = hv * _sigmoid(o_ref[b, :, h * A_DV:(h + 1) * A_DV])
        h_ref[b, :, h * A_DV:(h + 1) * A_DV] = hv.astype(h_ref.dtype)
        m_s[b, h] = m_new
        done[b, h] = (upd, g_c, ws)

    pairs = []
    done = {}
    gens = []
    for b in range(n_streams):
        cap_c = _soft_cap(gc_ref[b] + bc_ref[...])
        cap_r = _soft_cap(gr_ref[b] + br_ref[...])
        li_c = cap_c[:, 0:A_HEADS] * LOG2E
        lf_c = _log_sigmoid(cap_c[:, A_HEADS:2 * A_HEADS]) * LOG2E
        cols_rep = sum(_dot(piece, place) for piece in _split3(jnp.concatenate([li_c, lf_c], axis=1)))
        gates = (cols_rep, cap_r[0:A_HEADS, :] * LOG2E,
                 _log_sigmoid(cap_r[A_HEADS:2 * A_HEADS, :]) * LOG2E)
        for p in range(A_HEADS // 2):
            q2 = q_ref[b, :, p * LANES:(p + 1) * LANES]
            k2 = k_ref[b, :, p * LANES:(p + 1) * LANES]
            kt2 = _dot_nt(eye, k2).astype(BF16)
            ct2 = ct_s[b, p]
            n2 = n_s[b, p]
            pairs.append((b, p, k2, ct2, n2))
            for hh in range(2):
                gens.append(head_stages(b, 2 * p + hh, gates, q2, k2, kt2, ct2.astype(BF16),
                                        q2.astype(F32) * n2, done))
    while gens:
        alive = []
        for gen in gens:
            try:
                next(gen)
                alive.append(gen)
            except StopIteration:
                pass
        gens = alive

    for b, p, k2, ct2, n2 in pairs:
        (upd0, gc0, ws0), (upd1, gc1, ws1) = done[b, 2 * p], done[b, 2 * p + 1]
        ct_s[b, p] = jnp.where(lo_row, gc0, gc1) * ct2 + jnp.where(lo_row, upd0, upd1)
        ws2 = jnp.where(lo_lane, ws0, ws1)
        n_s[b, p] = (jnp.where(lo_lane, gc0, gc1) * n2
                     + jnp.sum(ws2 * k2.astype(F32), axis=0, keepdims=True))

    @pl.when(j == pl.num_programs(1) - 1)
    def _():
        c_out[...] = ct_s[...]
        n_out[...] = n_s[...]
        m_out[...] = m_s[...] * (1.0 / LOG2E)


def _mlstm(q, k, v, o, gates, gates_t, b_col, b_row, g_head, ct0, n0, m0, chunk, streams):
    bsz, t, _ = q.shape
    nc = t // chunk
    np_ = A_HEADS // 2
    sb = streams
    tok = lambda b, j: (b, j, 0)
    st4 = lambda b, j: (b, 0, 0, 0)
    return pl.pallas_call(
        functools.partial(_mlstm_kernel, chunk=chunk),
        grid=(bsz // sb, nc),
        in_specs=[pl.BlockSpec((sb, chunk, q.shape[2]), tok), pl.BlockSpec((sb, chunk, k.shape[2]), tok),
                  pl.BlockSpec((sb, chunk, v.shape[2]), tok), pl.BlockSpec((sb, chunk, o.shape[2]), tok),
                  pl.BlockSpec((sb, chunk, 2 * A_HEADS), tok),
                  pl.BlockSpec((sb, 2 * A_HEADS, chunk), lambda b, j: (b, 0, j)),
                  _resident(b_col.shape), _resident(b_row.shape), _resident(g_head.shape),
                  pl.BlockSpec((sb, np_, LANES, A_DV), st4), pl.BlockSpec((sb, np_, 1, LANES), st4),
                  pl.BlockSpec((sb, A_HEADS, 1, LANES), st4)],
        out_specs=[pl.BlockSpec((sb, chunk, v.shape[2]), tok),
                   pl.BlockSpec((sb, np_, LANES, A_DV), st4), pl.BlockSpec((sb, np_, 1, LANES), st4),
                   pl.BlockSpec((sb, A_HEADS, 1, LANES), st4)],
        out_shape=[jax.ShapeDtypeStruct((bsz, t, v.shape[2]), BF16),
                   jax.ShapeDtypeStruct((bsz, np_, LANES, A_DV), F32),
                   jax.ShapeDtypeStruct((bsz, np_, 1, LANES), F32),
                   jax.ShapeDtypeStruct((bsz, A_HEADS, 1, LANES), F32)],
        scratch_shapes=[pltpu.VMEM((sb, np_, LANES, A_DV), F32), pltpu.VMEM((sb, np_, 1, LANES), F32),
                        pltpu.VMEM((sb, A_HEADS, 1, LANES), F32)],
        compiler_params=_params("parallel", "arbitrary"),
        name="mlstm",
    )(q, k, v, o, gates, gates_t, b_col, b_row, g_head, ct0, n0, m0)


def _mix_ffn_kernel(a_ref, res_ref, wo_ref, g_ref, wgu_ref, wd_ref, out_ref, xn_s, acc_s):
    x1 = res_ref[...] + _dot(a_ref[...], wo_ref[...])
    xn_s[...] = (_rms_hat(x1) * g_ref[...]).astype(BF16)
    acc_s[...] = x1

    d_ff = wd_ref.shape[0]
    for c in range(d_ff // FF_CHUNK):
        lo, hi = c * FF_CHUNK, (c + 1) * FF_CHUNK
        xn = xn_s[...]
        gate = _dot(xn, wgu_ref[:, lo:hi])
        up = _dot(xn, wgu_ref[:, d_ff + lo:d_ff + hi])
        act = (gate * _sigmoid(gate) * up).astype(BF16)
        acc_s[...] += _dot(act, wd_ref[lo:hi, :])
    out_ref[...] = acc_s[...]


def _mix_ffn(a, res, w_o, g, wgu, wd, tm):
    m, d = res.shape
    row = lambda i: (i, 0)
    return pl.pallas_call(
        _mix_ffn_kernel,
        grid=(m // tm,),
        in_specs=[pl.BlockSpec((tm, a.shape[1]), row), pl.BlockSpec((tm, d), row),
                  _resident(w_o.shape), _resident(g.shape), _resident(wgu.shape), _resident(wd.shape)],
        out_specs=pl.BlockSpec((tm, d), row),
        out_shape=jax.ShapeDtypeStruct((m, d), F32),
        scratch_shapes=[pltpu.VMEM((tm, d), BF16), pltpu.VMEM((tm, d), F32)],
        compiler_params=_params("parallel"),
        name="mix_ffn",
    )(a, res, w_o, g, wgu, wd)


def _fox_proj_kernel(x_ref, gkv_ref, gmix_ref, wkv_ref, wf2_ref, wqg_ref, bf_ref, gk_ref, gq_ref,
                     k_ref, v_ref, logf_ref, kb_ref, vb_ref, q_ref, gate_ref):
    d = x_ref.shape[1]
    xhat = _rms_hat(x_ref[...])
    sh = (xhat * gkv_ref[...]).astype(BF16)
    hq = (xhat * gmix_ref[...]).astype(BF16)
    lo_lane = lax.broadcasted_iota(jnp.int32, (1, LANES), 1) < HALF

    fz = _dot(sh, wf2_ref[...])
    logf_ref[...] = _log_sigmoid(fz[:, 0:B_HEADS] + fz[:, B_HEADS:2 * B_HEADS] + bf_ref[...])

    for c in range(d // MXU_N):
        kraw = _dot(sh, wkv_ref[:, c * MXU_N:(c + 1) * MXU_N])
        qraw = _dot(hq, wqg_ref[:, c * MXU_N:(c + 1) * MXU_N])
        for u in range(MXU_N // LANES):
            us = slice(u * LANES, (u + 1) * LANES)
            cs = slice(c * MXU_N + u * LANES, c * MXU_N + (u + 1) * LANES)
            kn = _head_norm_group(kraw[:, us], gk_ref[:, cs], lo_lane)
            if k_ref.ndim == 3:
                kn_t = kn.T
                k_ref[0, cs, :] = kn_t
                kb_ref[0, cs, :] = kn_t.astype(BF16)
            else:
                k_ref[:, cs] = kn
                kb_ref[:, cs] = kn.astype(BF16)
            qn = _head_norm_group(qraw[:, us], gq_ref[:, cs], lo_lane)
            q_ref[:, cs] = (qn * (LOG2E * B_HD ** -0.5)).astype(BF16)
    for c in range(d // MXU_N):
        cs = slice(c * MXU_N, (c + 1) * MXU_N)
        v = _dot(sh, wkv_ref[:, d + c * MXU_N:d + (c + 1) * MXU_N])
        if v_ref.ndim == 3:
            v_ref[0, cs, :] = v.T
        else:
            v_ref[:, cs] = v
        vb_ref[:, cs] = v.astype(BF16)
    gate_ref[...] = _dot(hq, wqg_ref[:, d:2 * d])


def _fox_proj(x2d, g_kv, g_mix, w_kv, wf2, w_qg, b_f, gk_t, gq_t, tm, seq_t=None):
    m, d = x2d.shape
    row = lambda i: (i, 0)
    big = pl.BlockSpec((tm, d), row)
    if seq_t is None:
        kv_spec, kv_shape, kb_spec, kb_shape = big, (m, d), big, (m, d)
    else:
        per_seq = seq_t // tm
        kv_spec = kb_spec = pl.BlockSpec((1, d, tm), lambda i: (i // per_seq, 0, i % per_seq))
        kv_shape = kb_shape = (m // seq_t, d, seq_t)
    return pl.pallas_call(
        _fox_proj_kernel,
        grid=(m // tm,),
        in_specs=[big] + [_resident(a.shape) for a in (g_kv, g_mix, w_kv, wf2, w_qg, b_f, gk_t, gq_t)],
        out_specs=[kv_spec, kv_spec, pl.BlockSpec((tm, B_HEADS), row), kb_spec, big, big, big],
        out_shape=[jax.ShapeDtypeStruct(kv_shape, F32), jax.ShapeDtypeStruct(kv_shape, F32),
                   jax.ShapeDtypeStruct((m, B_HEADS), F32),
                   jax.ShapeDtypeStruct(kb_shape, BF16), jax.ShapeDtypeStruct((m, d), BF16),
                   jax.ShapeDtypeStruct((m, d), BF16), jax.ShapeDtypeStruct((m, d), F32)],
        compiler_params=_params("parallel"),
        name="fox_proj",
    )(x2d, g_kv, g_mix, w_kv, wf2, w_qg, b_f, gk_t, gq_t)


def _cum_bias_kernel(x_ref, aug_ref, carry_s):
    @pl.when(pl.program_id(1) == 0)
    def _():
        carry_s[...] = jnp.zeros(carry_s.shape, F32)

    tb = x_ref.shape[1]
    tri = (lax.broadcasted_iota(jnp.int32, (tb, tb), 1)
           <= lax.broadcasted_iota(jnp.int32, (tb, tb), 0)).astype(BF16)
    hi, mid, lo = _split3(x_ref[0])
    cum = _dot(tri, hi) + _dot(tri, mid) + _dot(tri, lo) + carry_s[...]
    carry_s[...] = cum[tb - 1:tb, :]

    h_idx = lax.broadcasted_iota(jnp.int32, (B_HEADS, aug_ref.shape[1]), 0)
    c_idx = lax.broadcasted_iota(jnp.int32, (B_HEADS, aug_ref.shape[1]), 1)
    base = N_BIAS_PIECES * h_idx
    aug = None
    for i, piece in enumerate(_split3(cum * (-LOG2E))):
        term = _dot(piece, (c_idx == base + i).astype(BF16))
        aug = term if aug is None else aug + term
    aug_ref[0] = aug.T.astype(BF16)


def _cum_bias(logf, tb):
    bsz, t, h = logf.shape
    width = LANES
    assert N_BIAS_PIECES * h <= width
    return pl.pallas_call(
        _cum_bias_kernel,
        grid=(bsz, t // tb),
        in_specs=[pl.BlockSpec((1, tb, h), lambda b, j: (b, j, 0))],
        out_specs=pl.BlockSpec((1, width, tb), lambda b, j: (b, 0, j)),
        out_shape=jax.ShapeDtypeStruct((bsz, width, t), BF16),
        scratch_shapes=[pltpu.VMEM((1, h), F32)],
        compiler_params=_params("parallel", "arbitrary"),
        name="cum_bias",
    )(logf)


def _attn_pair_scores(p, q2, kt2, ca_t, mask, m_s, lo_lane):
    tq = q2.shape[0]
    lane2 = lax.broadcasted_iota(jnp.int32, (tq, LANES), 1)
    rows = []
    for hh in range(2):
        h = 2 * p + hh
        half = lo_lane if hh == 0 else jnp.logical_not(lo_lane)
        qm = jnp.where(half, q2, jnp.zeros_like(q2))
        pick = jnp.logical_and(lane2 >= N_BIAS_PIECES * h, lane2 < N_BIAS_PIECES * (h + 1))
        rows.append(jnp.concatenate([qm, pick.astype(BF16)], axis=1))
    s2 = _dot(jnp.concatenate(rows, axis=0),
              jnp.concatenate([kt2, ca_t], axis=0))
    out = []
    for hh in range(2):
        s = s2[hh * tq:(hh + 1) * tq, :]
        cols = [s[:, c * LANES:(c + 1) * LANES] for c in range(s.shape[1] // LANES)]
        if mask is not None:
            cols = [jnp.where(mask[:, c * LANES:(c + 1) * LANES], col, NEG_BIG) for c, col in enumerate(cols)]
        top = functools.reduce(jnp.maximum, cols)
        m_old = m_s[2 * p + hh]
        m_new = jnp.maximum(m_old, jnp.max(top, axis=1, keepdims=True))
        out.append((cols, m_old, m_new))
    return out


def _attn_head_update(h, cols, m_old, m_new, v2, m_s, acc_s, lo_lane):
    half = lo_lane if h % 2 == 0 else jnp.logical_not(lo_lane)
    alpha = jnp.exp2(m_old - m_new)
    pr = jnp.concatenate([jnp.exp2(col - m_new).astype(BF16) for col in cols], axis=1)
    v_ones = jnp.where(half, v2, jnp.ones_like(v2))
    acc_s[h] = alpha * acc_s[h] + _dot(pr, v_ones)
    m_s[h] = m_new


def _attn_init(m_s, acc_s):
    m_s[...] = jnp.full(m_s.shape, NEG_BIG, F32)
    acc_s[...] = jnp.zeros(acc_s.shape, F32)


def _attn_finish(out_ref, gate_ref, acc_s, lo_lane):
    for p in range(B_HEADS // 2):
        a0, a1 = acc_s[2 * p], acc_s[2 * p + 1]
        num = jnp.where(lo_lane, a0, a1)
        den = pltpu.roll(jnp.where(lo_lane, a1, a0), HALF, axis=1)
        gate = gate_ref[0, :, p * LANES:(p + 1) * LANES]
        out_ref[0, :, p * LANES:(p + 1) * LANES] = (num / den * _sigmoid(gate)).astype(out_ref.dtype)


def _fold_block(r, c, nq):
    first = c <= r
    return jnp.where(first, r, nq - 1 - r), jnp.where(first, c, c - r - 1)


def _prompt_attn_kernel(q_ref, kt_ref, ca_ref, v_ref, gate_ref, out_ref, m_s, acc_s, *, blk, nq):
    i, j = _fold_block(pl.program_id(1), pl.program_id(2), nq)
    lo_lane = lax.broadcasted_iota(jnp.int32, (1, LANES), 1) < HALF

    @pl.when(j == 0)
    def _():
        _attn_init(m_s, acc_s)

    def run(masked):
        mask = None
        if masked:
            mask = (lax.broadcasted_iota(jnp.int32, (blk, blk), 1)
                    <= lax.broadcasted_iota(jnp.int32, (blk, blk), 0))
        ca = ca_ref[0]

        def scores(p):
            cs = slice(p * LANES, (p + 1) * LANES)
            return _attn_pair_scores(p, q_ref[0, :, cs], kt_ref[0, cs, :], ca, mask, m_s, lo_lane)

        pending = scores(0)
        for p in range(B_HEADS // 2):
            upcoming = scores(p + 1) if p + 1 < B_HEADS // 2 else None
            cs = slice(p * LANES, (p + 1) * LANES)
            for hh in range(2):
                _attn_head_update(2 * p + hh, *pending[hh], v_ref[0, :, cs], m_s, acc_s, lo_lane)
            pending = upcoming

    @pl.when(j < i)
    def _():
        run(False)

    @pl.when(j == i)
    def _():
        run(True)
        _attn_finish(out_ref, gate_ref, acc_s, lo_lane)


def _prompt_attn(q, kb_t, cum_aug, vb, gate, blk):
    bsz, t, d = q.shape
    nq = t // blk
    rows = max(nq // 2, 1)
    assert nq == 1 or nq % 2 == 0
    qmap = lambda b, r, c: (b, _fold_block(r, c, nq)[0], 0)
    kvmap = lambda b, r, c: (b, _fold_block(r, c, nq)[1], 0)
    ktmap = lambda b, r, c: (b, 0, _fold_block(r, c, nq)[1])
    return pl.pallas_call(
        functools.partial(_prompt_attn_kernel, blk=blk, nq=nq),
        grid=(bsz, rows, nq + 1 if nq > 1 else 1),
        in_specs=[pl.BlockSpec((1, blk, d), qmap), pl.BlockSpec((1, d, blk), ktmap),
                  pl.BlockSpec((1, cum_aug.shape[1], blk), ktmap), pl.BlockSpec((1, blk, d), kvmap),
                  pl.BlockSpec((1, blk, d), qmap)],
        out_specs=pl.BlockSpec((1, blk, d), qmap),
        out_shape=jax.ShapeDtypeStruct((bsz, t, d), BF16),
        scratch_shapes=[pltpu.VMEM((B_HEADS, blk, LANES), F32), pltpu.VMEM((B_HEADS, blk, LANES), F32)],
        compiler_params=_params("parallel", "arbitrary", "arbitrary"),
        name="prompt_attn",
    )(q, kb_t, cum_aug, vb, gate)


def _group_rows(x, tq):
    return jnp.concatenate([jnp.broadcast_to(x[r:r + 1, :], (tq, x.shape[1])) for r in range(x.shape[0])], axis=0)


def _lane_cumsum(x):
    r, n = x.shape
    nb = n // MXU_N
    stacked = jnp.concatenate([x[:, c * MXU_N:(c + 1) * MXU_N] for c in range(nb)], axis=0)
    upper = (lax.broadcasted_iota(jnp.int32, (MXU_N, MXU_N), 0)
             <= lax.broadcasted_iota(jnp.int32, (MXU_N, MXU_N), 1)).astype(BF16)
    within = sum(_dot(piece, upper) for piece in _split3(stacked))
    totals = jnp.broadcast_to(within[:, MXU_N - 1:MXU_N], (nb * r, LANES))
    ri = lax.broadcasted_iota(jnp.int32, (nb * r, nb * r), 0)
    ci = lax.broadcasted_iota(jnp.int32, (nb * r, nb * r), 1)
    earlier = jnp.logical_and(ci // r < ri // r, ci % r == ri % r).astype(BF16)
    offs = sum(_dot(earlier, piece) for piece in _split3(totals))
    full = within + jnp.concatenate([offs] * (MXU_N // LANES), axis=1)
    return [full[c * r:(c + 1) * r, :] for c in range(nb)]


def _sample_attn_kernel(q_ref, kt_ref, vt_ref, lf_ref, nk_ref, nv_ref, nlf_ref, gate_ref, out_ref,
                        cum_s, m_s, l_s, acc_s, *, tk):
    j = pl.program_id(1)
    tq = q_ref.shape[1]
    n_groups = acc_s.shape[0]
    hg = B_HEADS // n_groups
    gw = hg * B_HD
    rows = hg * tq
    row_head = lax.broadcasted_iota(jnp.int32, (rows, gw), 0) // tq
    lane_head = lax.broadcasted_iota(jnp.int32, (rows, gw), 1) // B_HD
    own = row_head == lane_head

    @pl.when(j == 0)
    def _():
        m_s[...] = jnp.full(m_s.shape, NEG_BIG, F32)
        l_s[...] = jnp.zeros(l_s.shape, F32)
        acc_s[...] = jnp.zeros(acc_s.shape, F32)
        blocks = _lane_cumsum(lf_ref[0])
        per_step = tk // MXU_N
        for c, blk in enumerate(blocks):
            cum_s[c // per_step, :, (c % per_step) * MXU_N:(c % per_step + 1) * MXU_N] = blk * LOG2E

    def q_block_diag(g):
        qg = q_ref[0, :, g * gw:(g + 1) * gw]
        qrep = jnp.concatenate([qg] * hg, axis=0)
        return jnp.where(own, qrep, jnp.zeros_like(qrep))

    def online_update(g, cols, pv_fn):
        top = functools.reduce(jnp.maximum, cols)
        m_old = m_s[g]
        m_new = jnp.maximum(m_old, jnp.max(top, axis=1, keepdims=True))
        alpha = jnp.exp2(m_old - m_new)
        probs = [jnp.exp2(col - m_new[:, 0:col.shape[1]]) for col in cols]
        if cols[0].shape[1] == LANES:
            psum = functools.reduce(jnp.add, probs)
        else:
            lane0 = lax.broadcasted_iota(jnp.int32, (rows, LANES), 1) == 0
            psum = jnp.where(lane0, jnp.sum(probs[0], axis=1, keepdims=True), 0.0)
        l_s[g] = alpha * l_s[g] + psum
        p = probs[0] if len(probs) == 1 else jnp.concatenate(probs, axis=1)
        acc_s[g] = jnp.concatenate([alpha] * (gw // LANES), axis=1) * acc_s[g] + pv_fn(p.astype(BF16))
        m_s[g] = m_new

    cum_blk = cum_s[j]
    for g in range(n_groups):
        kt = kt_ref[0, g * gw:(g + 1) * gw, :].astype(BF16)
        s = _dot(q_block_diag(g), kt) - _group_rows(cum_blk[g * hg:(g + 1) * hg, :], tq)
        cols = [s[:, c * LANES:(c + 1) * LANES] for c in range(tk // LANES)]
        vt = vt_ref[0, g * gw:(g + 1) * gw, :].astype(BF16)
        online_update(g, cols, lambda p, vt=vt: _dot_nt(p, vt))

    @pl.when(j == pl.num_programs(1) - 1)
    def _():
        t_q = lax.broadcasted_iota(jnp.int32, (rows, tq), 0) % tq
        t_k = lax.broadcasted_iota(jnp.int32, (rows, tq), 1)
        causal = t_k <= t_q
        upper = (lax.broadcasted_iota(jnp.int32, (tq, tq), 0)
                 <= lax.broadcasted_iota(jnp.int32, (tq, tq), 1)).astype(BF16)
        new_cum = sum(_dot(piece, upper) for piece in _split3(nlf_ref[0]))
        last = cum_s[pl.num_programs(1) - 1][:, tk - 1:tk]
        new_cum = new_cum * LOG2E + last
        lane_grp = lax.broadcasted_iota(jnp.int32, (tq, gw), 1) // B_HD
        for g in range(n_groups):
            kn = nk_ref[0, :, g * gw:(g + 1) * gw]
            s = _dot_nt(q_block_diag(g), kn) - _group_rows(new_cum[g * hg:(g + 1) * hg, :], tq)
            s = jnp.where(causal, s, NEG_BIG)
            vn = nv_ref[0, :, g * gw:(g + 1) * gw]
            online_update(g, [s], lambda p, vn=vn: _dot(p, vn))
            o = acc_s[g] / jnp.sum(l_s[g], axis=1, keepdims=True)
            picked = functools.reduce(
                jnp.add, [jnp.where(lane_grp == r, o[r * tq:(r + 1) * tq, :], 0.0) for r in range(hg)])
            gate = gate_ref[0, :, g * gw:(g + 1) * gw]
            out_ref[0, :, g * gw:(g + 1) * gw] = (picked * _sigmoid(gate)).astype(out_ref.dtype)


def _sample_attn(q, cache_kt, cache_vt, cache_logf_t, kb, vb, new_logf_t, gate, tk):
    bsz, tq, d = q.shape
    past = cache_kt.shape[2]
    n_groups = d // MXU_N
    one = lambda b, j: (b, 0, 0)
    blk = lambda b, j: (b, 0, j)
    return pl.pallas_call(
        functools.partial(_sample_attn_kernel, tk=tk),
        grid=(bsz, past // tk),
        in_specs=[pl.BlockSpec((1, tq, d), one), pl.BlockSpec((1, d, tk), blk), pl.BlockSpec((1, d, tk), blk),
                  pl.BlockSpec((1, B_HEADS, past), one),
                  pl.BlockSpec((1, tq, d), one), pl.BlockSpec((1, tq, d), one),
                  pl.BlockSpec((1, B_HEADS, tq), one), pl.BlockSpec((1, tq, d), one)],
        out_specs=pl.BlockSpec((1, tq, d), one),
        out_shape=jax.ShapeDtypeStruct((bsz, tq, d), BF16),
        scratch_shapes=[pltpu.VMEM((past // tk, B_HEADS, tk), F32),
                        pltpu.VMEM((n_groups, MXU_N // B_HD * tq, LANES), F32),
                        pltpu.VMEM((n_groups, MXU_N // B_HD * tq, LANES), F32),
                        pltpu.VMEM((n_groups, MXU_N // B_HD * tq, MXU_N), F32)],
        compiler_params=_params("parallel", "arbitrary"),
        name="sample_attn",
    )(q, cache_kt, cache_vt, cache_logf_t, kb, vb, new_logf_t, gate)


def _prep_weights(g_mix, g_ffn, w_in_a, b_i_a, b_f_a, g_head_a, w_out_a, g_kv, w_kvf, b_f_b, g_k,
                  w_qg_b, g_q_b, w_o_b, w_gu, w_down):
    d = g_mix.shape[1]
    main_w = 2 * A_HEADS * A_DQK + 2 * A_HEADS * A_DV
    w_in = w_in_a[0]
    gates_b = jnp.concatenate([b_i_a[0], b_f_a[0]])

    def ffn(layer):
        return w_gu[layer].astype(BF16), w_down[layer].astype(BF16)

    return dict(
        g_mix0=g_mix[0][None], g_mix1=g_mix[1][None], g_ffn0=g_ffn[0][None], g_ffn1=g_ffn[1][None],
        w_in=w_in[:, :main_w].astype(BF16), wg2=_hi_lo_cols(w_in[:, main_w:]),
        gb_col=gates_b[None, :], gb_row=gates_b[:, None], g_head=g_head_a[0][None],
        w_out=w_out_a[0].astype(BF16),
        g_kv=g_kv[None], w_kv=w_kvf[:, :2 * d].astype(BF16), wf2=_hi_lo_cols(w_kvf[:, 2 * d:]), b_f=b_f_b[None],
        gk_t=jnp.tile(g_k, B_HEADS)[None], gq_t=jnp.tile(g_q_b[0], B_HEADS)[None],
        w_qg=w_qg_b[0].astype(BF16), w_o=w_o_b[0].astype(BF16),
        ffn0=ffn(0), ffn1=ffn(1),
    )


def _state_to_pairs(c, n, m):
    bsz = c.shape[0]
    np_ = A_HEADS // 2
    ct = c.reshape(bsz, np_, 2, A_DV, A_DQK).transpose(0, 1, 2, 4, 3).reshape(bsz, np_, 2 * A_DQK, A_DV)
    n2 = n.reshape(bsz, np_, 1, 2 * A_DQK)
    m2 = jnp.broadcast_to(m[:, :, None, None], (bsz, A_HEADS, 1, LANES))
    return ct.astype(F32), n2.astype(F32), m2.astype(F32)


def _state_from_pairs(ct, n2, m2):
    bsz = ct.shape[0]
    np_ = A_HEADS // 2
    c = ct.reshape(bsz, np_, 2, A_DQK, A_DV).transpose(0, 1, 2, 4, 3).reshape(bsz, 1, A_HEADS, A_DV, A_DQK)
    n = n2.reshape(bsz, 1, A_HEADS, A_DQK)
    m = m2[:, :, 0, 0].reshape(bsz, 1, A_HEADS)
    return c, n, m


def _layer_a(x, c0, n0, m0, w, chunk, tm):
    bsz, t, d = x.shape
    x2d = x.reshape(bsz * t, d)
    q, k, v, o, gates = _in_proj(x2d, w["g_mix0"], w["w_in"], w["wg2"], tm)
    r3 = lambda z: z.reshape(bsz, t, z.shape[1])
    gates3 = r3(gates)
    ct0, n20, m20 = _state_to_pairs(c0, n0, m0)
    hn, ct, n2, m2 = _mlstm(r3(q), r3(k), r3(v), r3(o), gates3, gates3.transpose(0, 2, 1),
                            w["gb_col"], w["gb_row"], w["g_head"], ct0, n20, m20, chunk,
                            MLSTM_STREAMS if bsz % MLSTM_STREAMS == 0 else 1)
    x2 = _mix_ffn(hn.reshape(bsz * t, -1), x2d, w["w_out"], w["g_ffn0"], *w["ffn0"], tm)
    return x2, _state_from_pairs(ct, n2, m2)


def _fox_inputs(x2, w, tm, seq_t=None):
    return _fox_proj(x2, w["g_kv"], w["g_mix1"], w["w_kv"], w["wf2"], w["w_qg"],
                     w["b_f"], w["gk_t"], w["gq_t"], tm, seq_t)


def _prompt_trunk(x, w):
    bsz, t, d = x.shape
    tm = min(bsz * t, ROW_TILE)
    zeros = lambda *s: jnp.zeros(s, F32)
    x2, (c, n, m) = _layer_a(x, zeros(bsz, A_HEADS, A_DV, A_DQK), zeros(bsz, A_HEADS, A_DQK),
                             zeros(bsz, A_HEADS), w, min(t, MLSTM_CHUNK), tm)
    kt, vt, logf, kb_t, vb, q, gate = _fox_inputs(x2, w, tm, seq_t=t)
    r3 = lambda z: z.reshape(bsz, t, z.shape[1])
    logf3 = r3(logf)
    cum_aug = _cum_bias(logf3, min(t, ATTN_BLOCK))
    attn = _prompt_attn(r3(q), kb_t, cum_aug, r3(vb), r3(gate), min(t, ATTN_BLOCK))
    y = _mix_ffn(attn.reshape(bsz * t, d), x2, w["w_o"], w["g_ffn1"], *w["ffn1"], tm)
    to_out = lambda z: z.reshape(bsz, B_HEADS, B_HD, t).transpose(0, 3, 1, 2)
    return (y.reshape(bsz, t, d), to_out(kt), to_out(vt), logf3, c, n, m)


def _sample_trunk(x, c0, n0, m0, cache_k, cache_v, cache_logf, w):
    bsz, t, d = x.shape
    past = cache_k.shape[1]
    tm = min(bsz * t, ROW_TILE)
    x2, (c, n, m) = _layer_a(x, c0[:, 0], n0[:, 0], m0[:, 0], w, min(t, MLSTM_CHUNK), tm)
    k, v, logf, kb, vb, q, gate = _fox_inputs(x2, w, tm)
    r3 = lambda z: z.reshape(bsz, t, z.shape[1])
    logf3 = r3(logf)
    kt = cache_k.transpose(0, 2, 3, 1).reshape(bsz, d, past)
    vt = cache_v.transpose(0, 2, 3, 1).reshape(bsz, d, past)
    attn = _sample_attn(r3(q), kt, vt, cache_logf.transpose(0, 2, 1), r3(kb), r3(vb),
                        logf3.transpose(0, 2, 1), r3(gate), min(past, CACHE_KEY_BLOCK))
    y = _mix_ffn(attn.reshape(bsz * t, d), x2, w["w_o"], w["g_ffn1"], *w["ffn1"], tm)
    return (y.reshape(bsz, t, d), k.reshape(bsz, t, B_HEADS, B_HD), v.reshape(bsz, t, B_HEADS, B_HD),
            logf3, c, n, m)


def kernel(x_prompt, x_sample, cache_k, cache_v, cache_logf, state_c, state_n, state_m, g_mix, g_ffn, w_in_a, b_i_a, b_f_a, g_head_a, w_out_a, g_kv, w_kvf, b_f_b, g_k, w_qg_b, g_q_b, w_o_b, w_gu, w_down):
    w = _prep_weights(g_mix, g_ffn, w_in_a, b_i_a, b_f_a, g_head_a, w_out_a, g_kv, w_kvf, b_f_b, g_k,
                      w_qg_b, g_q_b, w_o_b, w_gu, w_down)
    y_p, k_p, v_p, f_p, c_p, n_p, m_p = _prompt_trunk(x_prompt, w)
    y_s, k_s, v_s, f_s, c_s, n_s, m_s = _sample_trunk(x_sample, state_c, state_n, state_m,
                                                      cache_k, cache_v, cache_logf, w)
    return (y_p, y_s, k_p, v_p, f_p, c_p, n_p, m_p, k_s, v_s, f_s, c_s, n_s, m_s)
```
